```python
import math
import jax, jax.numpy as jnp
from jax import lax
import numpy as np

D_MODEL = 1024
BATCH = 8
SEQ = 2048
DEPTH = 2
DEC_BATCH = 128
DEC_SEQ = 8
PAST_LEN = 16384
PAGE_SIZE = 128

MIX_WIDTH = D_MODEL
DN_HEADS = 4
DN_WIDTH = MIX_WIDTH // 2
DN_HEAD_DIM = DN_WIDTH // DN_HEADS
DN_CONV = 4
DN_CHUNK = 64
SSM_WIDTH = MIX_WIDTH // 4
SSM_GROUP = 16
SSM_GROUPS = SSM_WIDTH // SSM_GROUP
SSM_STATE = 64
HG_WIDTH = MIX_WIDTH - DN_WIDTH - SSM_WIDTH
HG_HEADS = 4
HG_HEAD_DIM = HG_WIDTH // HG_HEADS
HG_CHUNK = 32
FF_DIM = 2816
FF_CONV = 3
PLE_DIM = 256
EPS = 1e-6

IN_SIZES = (3 * DN_WIDTH, DN_WIDTH, DN_HEADS, DN_HEADS, SSM_WIDTH, HG_WIDTH, HG_WIDTH, HG_WIDTH, HG_WIDTH)
IN_COLS = sum(IN_SIZES)
IN_SPLITS = tuple(int(s) for s in np.cumsum(IN_SIZES)[:-1])

kernel_name = 'hybrid_deltanet_s5_hgrn2_step'


def rmsnorm(x, g):
    xf = x.astype(jnp.float32)
    y = xf * lax.rsqrt(jnp.mean(xf * xf, axis=-1, keepdims=True) + EPS)
    return (y * g.astype(jnp.float32)).astype(x.dtype)


def l2norm(x):
    xf = x.astype(jnp.float32)
    return xf * lax.rsqrt(jnp.sum(xf * xf, axis=-1, keepdims=True) + EPS)


def causal_dwconv(x, buf, w):
    width = w.shape[0]
    t = x.shape[1]
    xp = jnp.concatenate([buf.astype(x.dtype), x], axis=1)
    y = xp[:, 0:t] * w[0]
    for j in range(1, width):
        y = y + xp[:, j:j + t] * w[j]
    return y, xp[:, t:]


def to_chunks(a, c, n):
    bsz, t = a.shape[:2]
    a = jnp.pad(a.astype(jnp.float32), [(0, 0), (0, n * c - t)] + [(0, 0)] * (a.ndim - 2))
    a = a.reshape((bsz, n, c) + a.shape[2:])
    return jnp.transpose(a, (1, 0, 3, 2) + tuple(range(4, a.ndim)))


def from_chunks(o, t):
    n, bsz, h, c, d = o.shape
    return jnp.transpose(o, (1, 0, 3, 2, 4)).reshape(bsz, n * c, h, d)[:, :t]


def gated_delta_chunked(q, k, v, g, beta, s0):
    t = q.shape[1]
    dv = v.shape[-1]
    c = min(DN_CHUNK, t)
    n = -(-t // c)
    q, k, v, g, beta = [to_chunks(a, c, n) for a in (q, k, v, g, beta)]
    causal = jnp.tril(jnp.ones((c, c), dtype=bool))
    strict = jnp.tril(jnp.ones((c, c), dtype=bool), -1)
    gc = jnp.cumsum(g, axis=-1)
    decay = jnp.exp(jnp.where(causal, gc[..., :, None] - gc[..., None, :], -jnp.inf))
    m = jnp.where(strict, beta[..., :, None] * jnp.einsum('nbhtk,nbhsk->nbhts', k, k) * decay, 0.0)
    rhs = jnp.concatenate([beta[..., None] * v, (beta * jnp.exp(gc))[..., None] * k], axis=-1)
    uw = lax.linalg.triangular_solve(m + jnp.eye(c, dtype=m.dtype), rhs, left_side=True, lower=True)
    u, w = uw[..., :dv], uw[..., dv:]
    qk = jnp.einsum('nbhtk,nbhsk->nbhts', q, k) * decay
    q_dec = q * jnp.exp(gc)[..., None]
    k_dec = k * jnp.exp(gc[..., -1:] - gc)[..., None]
    g_last = jnp.exp(gc[..., -1])[..., None, None]

    def step(s, xs):
        qk_i, q_i, k_i, u_i, w_i, gl_i = xs
        v_new = u_i - jnp.einsum('bhtk,bhkv->bhtv', w_i, s)
        o = jnp.einsum('bhtk,bhkv->bhtv', q_i, s) + jnp.einsum('bhts,bhsv->bhtv', qk_i, v_new)
        s = s * gl_i + jnp.einsum('bhtk,bhtv->bhkv', k_i, v_new)
        return s, o

    s_fin, o = lax.scan(step, s0.astype(jnp.float32), (qk, q_dec, k_dec, u, w, g_last))
    return from_chunks(o, t), s_fin


def hgrn2_chunked(q, k, v, logf, s0):
    t = q.shape[1]
    c = min(HG_CHUNK, t)
    n = -(-t // c)
    q, k, v, logf = [to_chunks(a, c, n) for a in (q, k, v, logf)]
    causal = jnp.tril(jnp.ones((c, c), dtype=bool))[..., None]
    b = jnp.cumsum(logf, axis=-2)

    def step(s, xs):
        q_i, k_i, v_i, b_i = xs
        decay = jnp.exp(jnp.where(causal, b_i[..., :, None, :] - b_i[..., None, :, :], -jnp.inf))
        a = jnp.einsum('bhtk,bhsk,bhtsk->bhts', q_i, k_i, decay)
        o = jnp.einsum('bhtk,bhkv->bhtv', q_i * jnp.exp(b_i), s) + jnp.einsum('bhts,bhsv->bhtv', a, v_i)
        s = s * jnp.exp(b_i[..., -1, :])[..., None] + jnp.einsum('bhtk,bhtv->bhkv', k_i * jnp.exp(b_i[..., -1:, :] - b_i), v_i)
        return s, o

    s_fin, o = lax.scan(step, s0.astype(jnp.float32), (q, k, v, b))
    return from_chunks(o, t), s_fin


def s5_ssm(u, x0_re, x0_im, lam_re, lam_im, log_step, b_re, b_im, c_re, c_im, d_skip):
    f32 = jnp.float32
    bsz, t, _ = u.shape
    uf = u.astype(f32).reshape(bsz, t, SSM_GROUPS, SSM_GROUP)
    lam = lax.complex(lam_re.astype(f32), lam_im.astype(f32))
    delta = jnp.exp(log_step.astype(f32))[:, None]
    lam_bar = jnp.exp(lam * delta)
    b_bar = ((lam_bar - 1.0) / lam)[..., None] * lax.complex(b_re.astype(f32), b_im.astype(f32))
    c_mat = lax.complex(c_re.astype(f32), c_im.astype(f32))
    bu = jnp.einsum('gph,btgh->btgp', b_bar, uf.astype(jnp.complex64))
    x0 = lax.complex(x0_re.astype(f32), x0_im.astype(f32))
    bu = bu.at[:, 0].add(lam_bar * x0)
    a = jnp.broadcast_to(lam_bar, bu.shape)

    def combine(e1, e2):
        a1, b1 = e1
        a2, b2 = e2
        return a1 * a2, a2 * b1 + b2

    _, xs = lax.associative_scan(combine, (a, bu), axis=1)
    y = jnp.einsum('ghp,btgp->btgh', c_mat, xs).real + d_skip.astype(f32).reshape(SSM_GROUPS, SSM_GROUP) * uf
    x_last = xs[:, -1]
    return y.reshape(bsz, t, SSM_WIDTH), jnp.real(x_last), jnp.imag(x_last)


def trunk(x, p, conv_qkv, delta, ssm_re, ssm_im, hgrn, conv_ffn, weights):
    (norm_mix, w_in, dn_conv_w, dn_a_log, dn_dt_bias, dn_norm, ssm_lam_re, ssm_lam_im, ssm_log_step,
     ssm_b_re, ssm_b_im, ssm_c_re, ssm_c_im, ssm_d, ssm_glu_w, ssm_glu_b, hg_lower, hg_norm, w_out,
     norm_ffn, ffn_w_up, ffn_conv_w, ffn_w_down, norm_ple, ple_w_gate, ple_w_proj, norm_final) = weights
    f32 = jnp.float32
    bsz, t, _ = x.shape
    lb_p = jax.nn.softmax(hg_lower.astype(f32), axis=0)
    lower_bounds = jnp.cumsum(lb_p, axis=0) - lb_p[0]
    hshape = (bsz, t, HG_HEADS, HG_HEAD_DIM)
    h = x
    n_conv_qkv, n_delta, n_ssm_re, n_ssm_im, n_hgrn, n_conv_ffn = [], [], [], [], [], []
    for i in range(DEPTH):
        hn = rmsnorm(h, norm_mix[i])
        z = hn @ w_in[i]
        z_qkv, z_gate, z_beta, z_a, z_u, z_hq, z_hf, z_hi, z_hg = jnp.split(z, IN_SPLITS, axis=-1)
        qkv, buf_qkv = causal_dwconv(z_qkv, conv_qkv[i], dn_conv_w[i])
        qkv = jax.nn.silu(qkv).reshape(bsz, t, 3, DN_HEADS, DN_HEAD_DIM)
        q = l2norm(qkv[:, :, 0]) * DN_HEAD_DIM ** -0.5
        k = l2norm(qkv[:, :, 1])
        v = qkv[:, :, 2]
        beta = jax.nn.sigmoid(z_beta.astype(f32))
        g = -jnp.exp(dn_a_log[i].astype(f32)) * jax.nn.softplus(z_a.astype(f32) + dn_dt_bias[i].astype(f32))
        o_a, s_delta = gated_delta_chunked(q, k, v, g, beta, delta[i])
        o_a = rmsnorm(o_a, dn_norm[i]) * jax.nn.silu(z_gate.astype(f32).reshape(bsz, t, DN_HEADS, DN_HEAD_DIM))
        o_a = o_a.reshape(bsz, t, DN_WIDTH).astype(x.dtype)
        y_b, x_re, x_im = s5_ssm(z_u, ssm_re[i], ssm_im[i], ssm_lam_re[i], ssm_lam_im[i], ssm_log_step[i],
                                 ssm_b_re[i], ssm_b_im[i], ssm_c_re[i], ssm_c_im[i], ssm_d[i])
        y_b = jax.nn.gelu(y_b)
        o_b = (y_b * jax.nn.sigmoid(y_b @ ssm_glu_w[i].astype(f32) + ssm_glu_b[i].astype(f32))).astype(x.dtype)
        lb = lower_bounds[i]
        f = lb + (1.0 - lb) * jax.nn.sigmoid(z_hf.astype(f32))
        hq = jax.nn.silu(z_hq.astype(f32)).reshape(hshape)
        hk = (1.0 - f).reshape(hshape)
        hv = z_hi.astype(f32).reshape(hshape)
        o_c, s_hg = hgrn2_chunked(hq, hk, hv, jnp.log(f).reshape(hshape), hgrn[i])
        o_c = rmsnorm(o_c, hg_norm[i]) * jax.nn.silu(z_hg.astype(f32).reshape(hshape))
        o_c = o_c.reshape(bsz, t, HG_WIDTH).astype(x.dtype)
        h = h + jnp.concatenate([o_a, o_b, o_c], axis=-1) @ w_out[i]
        hn = rmsnorm(h, norm_ffn[i])
        up, buf_ffn = causal_dwconv(hn @ ffn_w_up[i], conv_ffn[i], ffn_conv_w[i])
        a_ff, b_ff = jnp.split(up, 2, axis=-1)
        h = h + (jax.nn.silu(a_ff) * b_ff) @ ffn_w_down[i]
        gate = jax.nn.sigmoid(rmsnorm(h, norm_ple[i]) @ ple_w_gate[i])
        h = h + gate * (p[i] @ ple_w_proj[i])
        n_conv_qkv.append(buf_qkv)
        n_delta.append(s_delta)
        n_ssm_re.append(x_re)
        n_ssm_im.append(x_im)
        n_hgrn.append(s_hg)
        n_conv_ffn.append(buf_ffn)
    return (rmsnorm(h, norm_final), jnp.stack(n_conv_qkv), jnp.stack(n_delta), jnp.stack(n_ssm_re),
            jnp.stack(n_ssm_im), jnp.stack(n_hgrn), jnp.stack(n_conv_ffn))


def setup_inputs(seed: int = 0) -> dict:
    key = jax.random.key(seed)
    ks = iter(jax.random.split(key, 48))
    f32 = jnp.float32

    def nrm(shape, scale):
        return jax.random.normal(next(ks), shape, f32) * scale

    def gain(shape):
        return 1.0 + nrm(shape, 0.02)

    def unif(shape, lo, hi):
        return jax.random.uniform(next(ks), shape, f32, minval=lo, maxval=hi)

    dt = jnp.exp(unif((DEPTH, DN_HEADS), math.log(1e-3), math.log(1e-1)))
    return {
        'x_prompt': nrm((BATCH, SEQ, D_MODEL), 1.0),
        'x_sample': nrm((DEC_BATCH, DEC_SEQ, D_MODEL), 1.0),
        'p_prompt': nrm((DEPTH, BATCH, SEQ, PLE_DIM), 1.0),
        'p_sample': nrm((DEPTH, DEC_BATCH, DEC_SEQ, PLE_DIM), 1.0),
        'state_conv_qkv': nrm((DEPTH, DEC_BATCH, DN_CONV - 1, 3 * DN_WIDTH), 1.0),
        'state_delta': nrm((DEPTH, DEC_BATCH, DN_HEADS, DN_HEAD_DIM, DN_HEAD_DIM), 0.1),
        'state_ssm_re': nrm((DEPTH, DEC_BATCH, SSM_GROUPS, SSM_STATE), 0.5),
        'state_ssm_im': nrm((DEPTH, DEC_BATCH, SSM_GROUPS, SSM_STATE), 0.5),
        'state_hgrn': nrm((DEPTH, DEC_BATCH, HG_HEADS, HG_HEAD_DIM, HG_HEAD_DIM), 0.5),
        'state_conv_ffn': nrm((DEPTH, DEC_BATCH, FF_CONV - 1, 2 * FF_DIM), 1.0),
        'norm_mix': gain((DEPTH, D_MODEL)),
        'w_in': nrm((DEPTH, D_MODEL, IN_COLS), D_MODEL ** -0.5),
        'dn_conv_w': nrm((DEPTH, DN_CONV, 3 * DN_WIDTH), DN_CONV ** -0.5),
        'dn_a_log': jnp.log(unif((DEPTH, DN_HEADS), 1.0, 16.0)),
        'dn_dt_bias': dt + jnp.log(-jnp.expm1(-dt)),
        'dn_norm': gain((DEPTH, DN_HEAD_DIM)),
        'ssm_lam_re': -0.5 + nrm((DEPTH, SSM_GROUPS, SSM_STATE), 0.01),
        'ssm_lam_im': jnp.pi * jnp.arange(SSM_STATE, dtype=f32) + nrm((DEPTH, SSM_GROUPS, SSM_STATE), 0.01),
        'ssm_log_step': unif((DEPTH, SSM_GROUPS), math.log(1e-3), math.log(1e-1)),
        'ssm_b_re': nrm((DEPTH, SSM_GROUPS, SSM_STATE, SSM_GROUP), (2 * SSM_GROUP) ** -0.5),
        'ssm_b_im': nrm((DEPTH, SSM_GROUPS, SSM_STATE, SSM_GROUP), (2 * SSM_GROUP) ** -0.5),
        'ssm_c_re': nrm((DEPTH, SSM_GROUPS, SSM_GROUP, SSM_STATE), SSM_STATE ** -0.5),
        'ssm_c_im': nrm((DEPTH, SSM_GROUPS, SSM_GROUP, SSM_STATE), SSM_STATE ** -0.5),
        'ssm_d': nrm((DEPTH, SSM_WIDTH), 1.0),
        'ssm_glu_w': nrm((DEPTH, SSM_WIDTH, SSM_WIDTH), SSM_WIDTH ** -0.5),
        'ssm_glu_b': nrm((DEPTH, SSM_WIDTH), 0.01),
        'hg_lower': nrm((DEPTH, HG_WIDTH), 1.0),
        'hg_norm': gain((DEPTH, HG_HEAD_DIM)),
        'w_out': nrm((DEPTH, MIX_WIDTH, D_MODEL), MIX_WIDTH ** -0.5),
        'norm_ffn': gain((DEPTH, D_MODEL)),
        'ffn_w_up': nrm((DEPTH, D_MODEL, 2 * FF_DIM), D_MODEL ** -0.5),
        'ffn_conv_w': nrm((DEPTH, FF_CONV, 2 * FF_DIM), FF_CONV ** -0.5),
        'ffn_w_down': nrm((DEPTH, FF_DIM, D_MODEL), FF_DIM ** -0.5),
        'norm_ple': gain((DEPTH, D_MODEL)),
        'ple_w_gate': nrm((DEPTH, D_MODEL, D_MODEL), D_MODEL ** -0.5),
        'ple_w_proj': nrm((DEPTH, PLE_DIM, D_MODEL), PLE_DIM ** -0.5),
        'norm_final': gain((D_MODEL,)),
    }


def reference(x_prompt, x_sample, p_prompt, p_sample, state_conv_qkv, state_delta, state_ssm_re, state_ssm_im,
              state_hgrn, state_conv_ffn, norm_mix, w_in, dn_conv_w, dn_a_log, dn_dt_bias, dn_norm, ssm_lam_re,
              ssm_lam_im, ssm_log_step, ssm_b_re, ssm_b_im, ssm_c_re, ssm_c_im, ssm_d, ssm_glu_w, ssm_glu_b,
              hg_lower, hg_norm, w_out, norm_ffn, ffn_w_up, ffn_conv_w, ffn_w_down, norm_ple, ple_w_gate,
              ple_w_proj, norm_final):
    weights = (norm_mix, w_in, dn_conv_w, dn_a_log, dn_dt_bias, dn_norm, ssm_lam_re, ssm_lam_im, ssm_log_step,
               ssm_b_re, ssm_b_im, ssm_c_re, ssm_c_im, ssm_d, ssm_glu_w, ssm_glu_b, hg_lower, hg_norm, w_out,
               norm_ffn, ffn_w_up, ffn_conv_w, ffn_w_down, norm_ple, ple_w_gate, ple_w_proj, norm_final)
    bp = x_prompt.shape[0]
    dt = x_prompt.dtype
    z_conv_qkv = jnp.zeros((DEPTH, bp, DN_CONV - 1, 3 * DN_WIDTH), dt)
    z_delta = jnp.zeros((DEPTH, bp, DN_HEADS, DN_HEAD_DIM, DN_HEAD_DIM), dt)
    z_ssm = jnp.zeros((DEPTH, bp, SSM_GROUPS, SSM_STATE), dt)
    z_hgrn = jnp.zeros((DEPTH, bp, HG_HEADS, HG_HEAD_DIM, HG_HEAD_DIM), dt)
    z_conv_ffn = jnp.zeros((DEPTH, bp, FF_CONV - 1, 2 * FF_DIM), dt)
    y_prompt, p_conv_qkv, p_delta, p_ssm_re, p_ssm_im, p_hgrn, p_conv_ffn = trunk(
        x_prompt, p_prompt, z_conv_qkv, z_delta, z_ssm, z_ssm, z_hgrn, z_conv_ffn, weights)
    y_sample, s_conv_qkv, s_delta, s_ssm_re, s_ssm_im, s_hgrn, s_conv_ffn = trunk(
        x_sample, p_sample, state_conv_qkv, state_delta, state_ssm_re, state_ssm_im, state_hgrn, state_conv_ffn,
        weights)
    return (y_prompt, y_sample, p_conv_qkv, p_delta, p_ssm_re, p_ssm_im, p_hgrn, p_conv_ffn,
            s_conv_qkv, s_delta, s_ssm_re, s_ssm_im, s_hgrn, s_conv_ffn)
```

```python
import functools
import math

import jax
import jax.numpy as jnp
from jax import lax
from jax.experimental import pallas as pl
from jax.experimental.pallas import tpu as pltpu

F32 = jnp.float32
BF16 = jnp.bfloat16

D_MODEL = 1024
DEPTH = 2
DN_HEADS = 4
DN_WIDTH = 512
DN_HD = 128
DN_CONV = 4
DN_CHUNK = 64
SSM_WIDTH = 256
SSM_GROUP = 16
SSM_GROUPS = 16
SSM_STATE = 64
SSM_FLAT = SSM_GROUPS * SSM_STATE
HG_WIDTH = 256
HG_HEADS = 4
HG_HD = 64
HG_CHUNK = 32
FF_DIM = 2816
FF_CONV = 3
PLE_DIM = 256
EPS = 1e-6

QKV_W = 3 * DN_WIDTH
ZH_W = 4 * HG_WIDTH
BA_W = 128
IN_PACKED = QKV_W + DN_WIDTH + SSM_WIDTH + ZH_W + BA_W

VMEM_LIMIT = 56 * 1024 * 1024


def _dot(a, b):
    return jnp.dot(a.astype(BF16), b.astype(BF16), preferred_element_type=F32)


def _dot_nt(a, b):
    return lax.dot_general(a.astype(BF16), b.astype(BF16), (((1,), (1,)), ((), ())),
                           preferred_element_type=F32)


def _dot_tn(a, b):
    return lax.dot_general(a.astype(BF16), b.astype(BF16), (((0,), (0,)), ((), ())),
                           preferred_element_type=F32)


def _split3(x):
    hi = x.astype(BF16)
    r1 = x - hi.astype(F32)
    mid = r1.astype(BF16)
    lo = (r1 - mid.astype(F32)).astype(BF16)
    return hi, mid, lo


def _dot_exact_rhs(x, m):
    hi, mid, lo = _split3(x)
    return (jnp.dot(hi, m, preferred_element_type=F32) + jnp.dot(mid, m, preferred_element_type=F32)
            + jnp.dot(lo, m, preferred_element_type=F32))


def _dot_exact_lhs(m, x):
    hi, mid, lo = _split3(x)
    return (jnp.dot(m, hi, preferred_element_type=F32) + jnp.dot(m, mid, preferred_element_type=F32)
            + jnp.dot(m, lo, preferred_element_type=F32))


def _sigmoid(x):
    return 1.0 / (1.0 + jnp.exp(-x))


def _silu(x):
    return x * _sigmoid(x)


def _softplus(x):
    return jnp.maximum(x, 0.0) + jnp.log1p(jnp.exp(-jnp.abs(x)))


def _rms(x, g):
    ms = jnp.mean(x * x, axis=-1, keepdims=True)
    return x * lax.rsqrt(ms + EPS) * g


def _block_tri(n, chunk):
    r = lax.broadcasted_iota(jnp.int32, (n, n), 0)
    c = lax.broadcasted_iota(jnp.int32, (n, n), 1)
    sh = int(math.log2(chunk))
    same = (r >> sh) == (c >> sh)
    return jnp.where(same & (c <= r), 1.0, 0.0).astype(BF16)


def _proj_in_body(x_ref, g_ref, w_ref, zq_ref, zg_ref, zu_ref, zh_ref, zba_ref, *copy_ref):
    x = x_ref[...]
    xn = _rms(x, g_ref[...]).astype(BF16)
    c0 = 0
    for ref, width in ((zq_ref, QKV_W), (zg_ref, DN_WIDTH), (zu_ref, SSM_WIDTH), (zh_ref, ZH_W),
                       (zba_ref, BA_W)):
        ref[...] = jnp.dot(xn, w_ref[:, c0:c0 + width], preferred_element_type=F32)
        c0 += width
    if copy_ref:
        copy_ref[0][...] = x


_Z_WIDTHS = (QKV_W, DN_WIDTH, SSM_WIDTH, ZH_W, BA_W)


def _proj_in_seq(x, g, w, bsz, t, tt, x_is_btd):
    grid = (bsz, t // tt)
    if x_is_btd:
        x_spec = pl.BlockSpec((None, tt, D_MODEL), lambda b, i: (b, i, 0))
    else:
        x_spec = pl.BlockSpec((tt, D_MODEL), lambda b, i: (i, b))
    out_shape = [jax.ShapeDtypeStruct((t, bsz * wd), F32) for wd in _Z_WIDTHS]
    out_specs = [pl.BlockSpec((tt, wd), lambda b, i: (i, b)) for wd in _Z_WIDTHS]
    if x_is_btd:
        out_shape.append(jax.ShapeDtypeStruct((t, bsz * D_MODEL), F32))
        out_specs.append(pl.BlockSpec((tt, D_MODEL), lambda b, i: (i, b)))
    return pl.pallas_call(
        _proj_in_body,
        grid=grid,
        in_specs=[x_spec,
                  pl.BlockSpec((1, D_MODEL), lambda b, i: (0, 0)),
                  pl.BlockSpec((D_MODEL, IN_PACKED), lambda b, i: (0, 0))],
        out_specs=out_specs,
        out_shape=out_shape,
        compiler_params=pltpu.CompilerParams(dimension_semantics=("arbitrary", "arbitrary"),
                                             vmem_limit_bytes=VMEM_LIMIT),
        name="proj_in_seq",
    )(x, g, w)


def _proj_in_rows(x, g, w, rows):
    n = x.shape[0]
    grid = (n // rows,)
    out_shape = [jax.ShapeDtypeStruct((n, wd), F32) for wd in _Z_WIDTHS]
    out_specs = [pl.BlockSpec((rows, wd), lambda i: (i, 0)) for wd in _Z_WIDTHS]
    return pl.pallas_call(
        _proj_in_body,
        grid=grid,
        in_specs=[pl.BlockSpec((rows, D_MODEL), lambda i: (i, 0)),
                  pl.BlockSpec((1, D_MODEL), lambda i: (0, 0)),
                  pl.BlockSpec((D_MODEL, IN_PACKED), lambda i: (0, 0))],
        out_specs=out_specs,
        out_shape=out_shape,
        compiler_params=pltpu.CompilerParams(dimension_semantics=("arbitrary",),
                                             vmem_limit_bytes=VMEM_LIMIT),
        name="proj_in_rows",
    )(x, g, w)


def _delta_body(zq_ref, zg_ref, zba_ref, cw_ref, coef_ref, dnw_ref, cs_ref, s0_ref,
                o_ref, cso_ref, s_ref,
                buf, carry, qb, kb, vb, pb, gcb, growb, ub, wb, qdb, kdb, qkb, ob,
                *, tt, chunk, n_tiles):
    ti = pl.program_id(1)
    n_chunks = tt // chunk
    levels = int(math.log2(chunk))

    @pl.when(ti == 0)
    def _():
        s_ref[...] = s0_ref[...]
        carry[...] = cs_ref[...]

    zq = zq_ref[...]
    buf[0:8, :] = carry[...]
    buf[8:8 + tt, :] = zq
    cw = cw_ref[...]
    y = (cw[3:4] * zq + cw[2:3] * buf[7:7 + tt, :] + cw[1:2] * buf[6:6 + tt, :]
         + cw[0:1] * buf[5:5 + tt, :])
    carry[...] = buf[tt:tt + 8, :]

    @pl.when(ti == n_tiles - 1)
    def _():
        cso_ref[...] = buf[tt:tt + 8, :]

    y = _silu(y)
    for h in range(DN_HEADS):
        qs = slice(h * DN_HD, (h + 1) * DN_HD)
        ks = slice(DN_WIDTH + h * DN_HD, DN_WIDTH + (h + 1) * DN_HD)
        qh = y[:, qs]
        kh = y[:, ks]
        qb[:, qs] = qh * (lax.rsqrt(jnp.sum(qh * qh, axis=-1, keepdims=True) + EPS) * DN_HD ** -0.5)
        kb[:, qs] = kh * lax.rsqrt(jnp.sum(kh * kh, axis=-1, keepdims=True) + EPS)
    vb[...] = y[:, 2 * DN_WIDTH:3 * DN_WIDTH]

    zba = zba_ref[...]
    coef = coef_ref[...]
    lane = lax.broadcasted_iota(jnp.int32, zba.shape, 1)
    g = coef[0:1] * _softplus(zba + coef[1:2])
    pb[...] = jnp.where(lane < DN_HEADS, _sigmoid(zba), g)
    gc = _dot_exact_lhs(_block_tri(tt, chunk), g)
    gcb[...] = gc
    sel = jnp.where(lax.broadcasted_iota(jnp.int32, (8, BA_W), 0)
                    == lax.broadcasted_iota(jnp.int32, (8, BA_W), 1), 1.0, 0.0).astype(BF16)
    nt = (((1,), (1,)), ((), ()))
    gct = sum(lax.dot_general(sel, part, nt, preferred_element_type=F32) for part in _split3(gc))
    for c in range(n_chunks):
        growb[c] = gct[:, c * chunk:(c + 1) * chunk]

    ri = lax.broadcasted_iota(jnp.int32, (chunk, chunk), 0)
    ci = lax.broadcasted_iota(jnp.int32, (chunk, chunk), 1)
    causal = ci <= ri
    strict = ci < ri
    eye = jnp.where(ci == ri, 1.0, 0.0).astype(F32)

    def phase1(c, carry_):
        rows = pl.ds(pl.multiple_of(c * chunk, chunk), chunk)
        grow8 = growb[c]
        pbv = pb[rows, :]
        gcv = gcb[rows, :]
        for h in range(DN_HEADS):
            hs = slice(h * DN_HD, (h + 1) * DN_HD)
            q = qb[rows, hs]
            k = kb[rows, hs]
            v = vb[rows, hs]
            beta = pbv[:, h:h + 1]
            gcol = gcv[:, DN_HEADS + h:DN_HEADS + h + 1]
            grow = grow8[DN_HEADS + h:DN_HEADS + h + 1, :]
            glast = grow[:, chunk - 1:chunk]
            dec = jnp.where(causal, jnp.exp(jnp.where(causal, gcol - grow, 0.0)), 0.0)
            qkk = _dot_nt(jnp.concatenate([q, k], axis=0), k)
            qk = qkk[0:chunk]
            kk = qkk[chunk:2 * chunk]
            p = -jnp.where(strict, beta * kk * dec, 0.0)
            tmat = eye + p
            for _ in range(levels - 1):
                p = _dot(p, p)
                tmat = tmat + _dot(p, tmat)
            eg = jnp.exp(gcol)
            rhs = jnp.concatenate([beta * v, (beta * eg) * k], axis=1)
            uw = _dot(tmat, rhs)
            ub[rows, hs] = uw[:, 0:DN_HD]
            wb[rows, hs] = uw[:, DN_HD:2 * DN_HD]
            qdb[rows, hs] = q * eg
            kdb[rows, hs] = k * jnp.exp(glast - gcol)
            qkb[c * DN_HEADS + h] = qk * dec
        return carry_

    lax.fori_loop(0, n_chunks, phase1, 0)

    def phase2(c, carry_):
        rows = pl.ds(pl.multiple_of(c * chunk, chunk), chunk)
        grow8 = growb[c]
        for h in range(DN_HEADS):
            hs = slice(h * DN_HD, (h + 1) * DN_HD)
            s = s_ref[h]
            r = _dot(jnp.concatenate([wb[rows, hs], qdb[rows, hs]], axis=0), s)
            v_new = ub[rows, hs] - r[0:chunk]
            ob[rows, hs] = r[chunk:2 * chunk] + _dot(qkb[c * DN_HEADS + h], v_new)
            glast = grow8[DN_HEADS + h:DN_HEADS + h + 1, chunk - 1:chunk]
            s_ref[h] = s * jnp.exp(glast) + _dot_tn(kdb[rows, hs], v_new)
        return carry_

    lax.fori_loop(0, n_chunks, phase2, 0)

    o = ob[...]
    gate = zg_ref[...]
    dnw = dnw_ref[...]
    for h in range(DN_HEADS):
        hs = slice(h * DN_HD, (h + 1) * DN_HD)
        o_ref[:, hs] = _rms(o[:, hs], dnw) * _silu(gate[:, hs])


def _delta_call(zq, zg, zba, cw, coef, dnw, cs, s0, bsz, t, tt, chunk):
    n_tiles = t // tt
    n_chunks = tt // chunk
    body = functools.partial(_delta_body, tt=tt, chunk=chunk, n_tiles=n_tiles)
    seq = lambda wd: pl.BlockSpec((tt, wd), lambda b, i: (i, b))
    full = lambda shp: pl.BlockSpec(shp, lambda b, i: tuple(0 for _ in shp))
    return pl.pallas_call(
        body,
        grid=(bsz, n_tiles),
        in_specs=[seq(QKV_W), seq(DN_WIDTH), seq(BA_W),
                  full((DN_CONV, QKV_W)), full((2, BA_W)), full((1, DN_HD)),
                  pl.BlockSpec((None, 8, QKV_W), lambda b, i: (b, 0, 0)),
                  pl.BlockSpec((None, DN_HEADS, DN_HD, DN_HD), lambda b, i: (b, 0, 0, 0))],
        out_specs=[seq(DN_WIDTH),
                   pl.BlockSpec((None, 8, QKV_W), lambda b, i: (b, 0, 0)),
                   pl.BlockSpec((None, DN_HEADS, DN_HD, DN_HD), lambda b, i: (b, 0, 0, 0))],
        out_shape=[jax.ShapeDtypeStruct((t, bsz * DN_WIDTH), F32),
                   jax.ShapeDtypeStruct((bsz, 8, QKV_W), F32),
                   jax.ShapeDtypeStruct((bsz, DN_HEADS, DN_HD, DN_HD), F32)],
        scratch_shapes=[pltpu.VMEM((tt + 8, QKV_W), F32),
                        pltpu.VMEM((8, QKV_W), F32),
                        pltpu.VMEM((tt, DN_WIDTH), F32),
                        pltpu.VMEM((tt, DN_WIDTH), F32),
                        pltpu.VMEM((tt, DN_WIDTH), F32),
                        pltpu.VMEM((tt, BA_W), F32),
                        pltpu.VMEM((tt, BA_W), F32),
                        pltpu.VMEM((n_chunks, 8, chunk), F32),
                        pltpu.VMEM((tt, DN_WIDTH), F32),
                        pltpu.VMEM((tt, DN_WIDTH), F32),
                        pltpu.VMEM((tt, DN_WIDTH), F32),
                        pltpu.VMEM((tt, DN_WIDTH), F32),
                        pltpu.VMEM((n_chunks * DN_HEADS, chunk, chunk), F32),
                        pltpu.VMEM((tt, DN_WIDTH), F32)],
        compiler_params=pltpu.CompilerParams(dimension_semantics=("arbitrary", "arbitrary"),
                                             vmem_limit_bytes=VMEM_LIMIT),
        name="delta",
    )(zq, zg, zba, cw, coef, dnw, cs, s0)


def _hgrn_body(zh_ref, lb_ref, nw_ref, gm_ref, s0_ref, o_ref, s_ref,
               qb, kb, bb, ob, *, tt, chunk):
    ti = pl.program_id(1)
    n_chunks = tt // chunk
    mid = chunk // 2 - 1

    @pl.when(ti == 0)
    def _():
        s_ref[...] = s0_ref[...]

    zh = zh_ref[...]
    lb = lb_ref[...]
    f = lb + (1.0 - lb) * _sigmoid(zh[:, HG_WIDTH:2 * HG_WIDTH])
    qb[...] = _silu(zh[:, 0:HG_WIDTH])
    kb[...] = 1.0 - f
    bb[...] = _dot_exact_lhs(_block_tri(tt, chunk), jnp.log(f))

    sh_c = int(math.log2(chunk))
    sh_h = int(math.log2(HG_HD))
    ri = lax.broadcasted_iota(jnp.int32, (HG_HEADS * chunk, chunk), 0)
    ci = lax.broadcasted_iota(jnp.int32, (HG_HEADS * chunk, chunk), 1)
    causal = ci <= (ri & (chunk - 1))
    lane_head = lax.broadcasted_iota(jnp.int32, (chunk, HG_WIDTH), 1) >> sh_h
    hmask = [jnp.where(lane_head == h, 1.0, 0.0).astype(F32) for h in range(HG_HEADS)]
    bd = ((lax.broadcasted_iota(jnp.int32, (HG_WIDTH, HG_WIDTH), 0) >> sh_h)
          == (lax.broadcasted_iota(jnp.int32, (HG_WIDTH, HG_WIDTH), 1) >> sh_h))
    del sh_c

    def step(c, carry_):
        r0 = pl.multiple_of(c * chunk, chunk)
        rows = pl.ds(r0, chunk)
        b = bb[rows, :]
        bm = bb[pl.ds(r0 + mid, 1), :]
        bl = bb[pl.ds(r0 + chunk - 1, 1), :]
        q = qb[rows, :]
        k = kb[rows, :]
        v = zh_ref[rows, 2 * HG_WIDTH:3 * HG_WIDTH]
        qs = q * jnp.exp(b - bm)
        ks = k * jnp.exp(bm - b)
        qe = q * jnp.exp(b)
        kd = k * jnp.exp(bl - b)
        el = jnp.exp(bl)
        a = jnp.where(causal, _dot_nt(jnp.concatenate([qs * m for m in hmask], axis=0), ks), 0.0)
        st = s_ref[...]
        o = _dot_nt(qe, st)
        for h in range(HG_HEADS):
            o = o + hmask[h] * _dot(a[h * chunk:(h + 1) * chunk], v)
        ob[rows, :] = o
        s_ref[...] = st * el + jnp.where(bd, _dot_tn(v, kd), 0.0)
        return carry_

    lax.fori_loop(0, n_chunks, step, 0)

    o = ob[...]
    ms = _dot_exact_rhs(o * o, gm_ref[...]) * (1.0 / HG_HD)
    o_ref[...] = o * lax.rsqrt(ms + EPS) * nw_ref[...] * _silu(zh[:, 3 * HG_WIDTH:4 * HG_WIDTH])


def _hgrn_call(zh, lb, nw, gm, s0, bsz, t, tt, chunk):
    body = functools.partial(_hgrn_body, tt=tt, chunk=chunk)
    full = lambda shp: pl.BlockSpec(shp, lambda b, i: tuple(0 for _ in shp))
    st_spec = pl.BlockSpec((None, HG_WIDTH, HG_WIDTH), lambda b, i: (b, 0, 0))
    return pl.pallas_call(
        body,
        grid=(bsz, t // tt),
        in_specs=[pl.BlockSpec((tt, ZH_W), lambda b, i: (i, b)),
                  full((1, HG_WIDTH)), full((1, HG_WIDTH)), full((HG_WIDTH, HG_WIDTH)), st_spec],
        out_specs=[pl.BlockSpec((tt, HG_WIDTH), lambda b, i: (i, b)), st_spec],
        out_shape=[jax.ShapeDtypeStruct((t, bsz * HG_WIDTH), F32),
                   jax.ShapeDtypeStruct((bsz, HG_WIDTH, HG_WIDTH), F32)],
        scratch_shapes=[pltpu.VMEM((tt, HG_WIDTH), F32), pltpu.VMEM((tt, HG_WIDTH), F32),
                        pltpu.VMEM((tt, HG_WIDTH), F32), pltpu.VMEM((tt, HG_WIDTH), F32)],
        compiler_params=pltpu.CompilerParams(dimension_semantics=("arbitrary", "arbitrary"),
                                             vmem_limit_bytes=VMEM_LIMIT),
        name="hgrn",
    )(zh, lb, nw, gm, s0)


def _s5_body(u_ref, bh_ref, bl_ref, lam_ref, c_ref, d_ref, gw_ref, gb_ref, x0r_ref, x0i_ref,
             o_ref, xr_ref, xi_ref, xs, *, bsz, steps):
    i = pl.program_id(0)

    @pl.when(i == 0)
    def _():
        xr_ref[...] = x0r_ref[...]
        xi_ref[...] = x0i_ref[...]

    u = u_ref[...]
    u_hi = u.astype(BF16)
    u_lo = (u - u_hi.astype(F32)).astype(BF16)
    bh = bh_ref[...]
    xs[...] = (jnp.dot(u_hi, bh, preferred_element_type=F32) + jnp.dot(u_lo, bh, preferred_element_type=F32)
               + jnp.dot(u_hi, bl_ref[...], preferred_element_type=F32))

    lr = jnp.broadcast_to(lam_ref[0:1, :], (8, SSM_FLAT))
    li = jnp.broadcast_to(lam_ref[1:2, :], (8, SSM_FLAT))

    def row_block(rb, carry_):
        r0 = pl.multiple_of(rb * 8, 8)
        xr0 = xr_ref[pl.ds(r0, 8), :]
        xi0 = xi_ref[pl.ds(r0, 8), :]

        def tstep(t, st):
            xr, xi = st
            row = pl.ds(pl.multiple_of(t * bsz + r0, 8), 8)
            nr = lr * xr - li * xi + xs[row, 0:SSM_FLAT]
            ni = lr * xi + li * xr + xs[row, SSM_FLAT:2 * SSM_FLAT]
            xs[row, 0:SSM_FLAT] = nr
            xs[row, SSM_FLAT:2 * SSM_FLAT] = ni
            return nr, ni

        xr1, xi1 = lax.fori_loop(0, steps, tstep, (xr0, xi0))
        xr_ref[pl.ds(r0, 8), :] = xr1
        xi_ref[pl.ds(r0, 8), :] = xi1
        return carry_

    lax.fori_loop(0, bsz // 8, row_block, 0)

    y = _dot(xs[...], c_ref[...]) + d_ref[...] * u
    y = 0.5 * y * (1.0 + jnp.tanh(math.sqrt(2.0 / math.pi) * (y + 0.044715 * (y * y * y))))
    o_ref[...] = y * _sigmoid(_dot(y, gw_ref[...]) + gb_ref[...])


def _s5_call(u, bh, bl, lam, cm, d, gw, gb, x0r, x0i, bsz, t, steps):
    rows = steps * bsz
    body = functools.partial(_s5_body, bsz=bsz, steps=steps)
    full = lambda shp: pl.BlockSpec(shp, lambda i: tuple(0 for _ in shp))
    return pl.pallas_call(
        body,
        grid=(t // steps,),
        in_specs=[pl.BlockSpec((rows, SSM_WIDTH), lambda i: (i, 0)),
                  full((SSM_WIDTH, 2 * SSM_FLAT)), full((SSM_WIDTH, 2 * SSM_FLAT)),
                  full((2, SSM_FLAT)), full((2 * SSM_FLAT, SSM_WIDTH)), full((1, SSM_WIDTH)),
                  full((SSM_WIDTH, SSM_WIDTH)), full((1, SSM_WIDTH)),
                  full((bsz, SSM_FLAT)), full((bsz, SSM_FLAT))],
        out_specs=[pl.BlockSpec((rows, SSM_WIDTH), lambda i: (i, 0)),
                   full((bsz, SSM_FLAT)), full((bsz, SSM_FLAT))],
        out_shape=[jax.ShapeDtypeStruct((t * bsz, SSM_WIDTH), F32),
                   jax.ShapeDtypeStruct((bsz, SSM_FLAT), F32),
                   jax.ShapeDtypeStruct((bsz, SSM_FLAT), F32)],
        scratch_shapes=[pltpu.VMEM((rows, 2 * SSM_FLAT), F32)],
        compiler_params=pltpu.CompilerParams(dimension_semantics=("arbitrary",),
                                             vmem_limit_bytes=VMEM_LIMIT),
        name="s5",
    )(u, bh, bl, lam, cm, d, gw, gb, x0r, x0i)


def _tail_body(h_ref, oa_ref, ob_ref, oc_ref, wo_ref, nf_ref, wua_ref, wub_ref, cwa_ref, cwb_ref,
               wd_ref, csa_ref, csb_ref, p_ref, npl_ref, wg_ref, wp_ref,
               out_ref, cso_a_ref, cso_b_ref,
               hn, upa, upb, car_a, car_b, *, rows, bsz, n_ffc):
    i = pl.program_id(0)
    j = pl.program_id(1)
    halo = (FF_CONV - 1) * bsz

    @pl.when(j == 0)
    def _():
        h2 = (h_ref[...] + _dot(oa_ref[...], wo_ref[0:DN_WIDTH, :])
              + _dot(ob_ref[...], wo_ref[DN_WIDTH:DN_WIDTH + SSM_WIDTH, :])
              + _dot(oc_ref[...], wo_ref[DN_WIDTH + SSM_WIDTH:D_MODEL, :]))
        out_ref[...] = h2
        hn[...] = _rms(h2, nf_ref[...]).astype(BF16)

    @pl.when(i == 0)
    def _():
        car_a[j] = csa_ref[...]
        car_b[j] = csb_ref[...]

    hnv = hn[...]
    halves = []
    for up, car, wu, cw, cso in ((upa, car_a, wua_ref, cwa_ref, cso_a_ref),
                                 (upb, car_b, wub_ref, cwb_ref, cso_b_ref)):
        up[0:halo, :] = car[j]
        up[halo:halo + rows, :] = jnp.dot(hnv, wu[...], preferred_element_type=F32)
        last = up[rows:rows + halo, :]
        car[j] = last
        cso[...] = last
        w = cw[...]
        halves.append(w[0:1] * up[0:rows, :] + w[1:2] * up[bsz:bsz + rows, :]
                      + w[2:3] * up[2 * bsz:2 * bsz + rows, :])
    out_ref[...] += _dot(_silu(halves[0]) * halves[1], wd_ref[...])

    @pl.when(j == n_ffc - 1)
    def _():
        h3 = out_ref[...]
        gate = _sigmoid(_dot(_rms(h3, npl_ref[...]), wg_ref[...]))
        out_ref[...] = h3 + gate * _dot(p_ref[...], wp_ref[...])


def _tail_call(h, oa, ob, oc, wo, nf, wu, cw, wd, cs, p, npl, wg, wp, bsz, rows, ffc):
    n = h.shape[0]
    n_ffc = FF_DIM // ffc
    halo = (FF_CONV - 1) * bsz
    body = functools.partial(_tail_body, rows=rows, bsz=bsz, n_ffc=n_ffc)
    row = lambda wd_: pl.BlockSpec((rows, wd_), lambda i, j: (i, 0))
    full = lambda shp: pl.BlockSpec(shp, lambda i, j: tuple(0 for _ in shp),
                                    pipeline_mode=pl.Buffered(1))
    return pl.pallas_call(
        body,
        grid=(n // rows, n_ffc),
        in_specs=[row(D_MODEL), row(DN_WIDTH), row(SSM_WIDTH), row(HG_WIDTH),
                  full((D_MODEL, D_MODEL)), full((1, D_MODEL)),
                  pl.BlockSpec((D_MODEL, ffc), lambda i, j: (0, j)),
                  pl.BlockSpec((D_MODEL, ffc), lambda i, j: (0, n_ffc + j)),
                  pl.BlockSpec((FF_CONV, ffc), lambda i, j: (0, j)),
                  pl.BlockSpec((FF_CONV, ffc), lambda i, j: (0, n_ffc + j)),
                  pl.BlockSpec((ffc, D_MODEL), lambda i, j: (j, 0)),
                  pl.BlockSpec((halo, ffc), lambda i, j: (0, j)),
                  pl.BlockSpec((halo, ffc), lambda i, j: (0, n_ffc + j)),
                  row(PLE_DIM), full((1, D_MODEL)), full((D_MODEL, D_MODEL)), full((PLE_DIM, D_MODEL))],
        out_specs=[row(D_MODEL),
                   pl.BlockSpec((None, halo, ffc), lambda i, j: (i, 0, j)),
                   pl.BlockSpec((None, halo, ffc), lambda i, j: (i, 0, j))],
        out_shape=[jax.ShapeDtypeStruct((n, D_MODEL), F32),
                   jax.ShapeDtypeStruct((n // rows, halo, FF_DIM), F32),
                   jax.ShapeDtypeStruct((n // rows, halo, FF_DIM), F32)],
        scratch_shapes=[pltpu.VMEM((rows, D_MODEL), BF16),
                        pltpu.VMEM((rows + halo, ffc), F32),
                        pltpu.VMEM((rows + halo, ffc), F32),
                        pltpu.VMEM((n_ffc, halo, ffc), F32),
                        pltpu.VMEM((n_ffc, halo, ffc), F32)],
        compiler_params=pltpu.CompilerParams(dimension_semantics=("arbitrary", "arbitrary"),
                                             vmem_limit_bytes=VMEM_LIMIT),
        name="tail",
    )(h, oa, ob, oc, wo, nf, wu, wu, cw, cw, wd, cs, cs, p, npl, wg, wp)


def _final_body(h_ref, g_ref, o_ref):
    o_ref[...] = _rms(h_ref[...], g_ref[...])


def _final_seq(h2d, g, bsz, t, tt):
    return pl.pallas_call(
        _final_body,
        grid=(bsz, t // tt),
        in_specs=[pl.BlockSpec((tt, D_MODEL), lambda b, i: (i, b)),
                  pl.BlockSpec((1, D_MODEL), lambda b, i: (0, 0))],
        out_specs=pl.BlockSpec((None, tt, D_MODEL), lambda b, i: (b, i, 0)),
        out_shape=jax.ShapeDtypeStruct((bsz, t, D_MODEL), F32),
        compiler_params=pltpu.CompilerParams(dimension_semantics=("arbitrary", "arbitrary")),
        name="final_seq",
    )(h2d, g)


def _final_rows(h, g, rows):
    n = h.shape[0]
    return pl.pallas_call(
        _final_body,
        grid=(n // rows,),
        in_specs=[pl.BlockSpec((rows, D_MODEL), lambda i: (i, 0)),
                  pl.BlockSpec((1, D_MODEL), lambda i: (0, 0))],
        out_specs=pl.BlockSpec((rows, D_MODEL), lambda i: (i, 0)),
        out_shape=jax.ShapeDtypeStruct((n, D_MODEL), F32),
        compiler_params=pltpu.CompilerParams(dimension_semantics=("arbitrary",)),
        name="final_rows",
    )(h, g)


def _pack_layer(i, prm):
    (norm_mix, w_in, dn_conv_w, dn_a_log, dn_dt_bias, dn_norm, ssm_lam_re, ssm_lam_im, ssm_log_step,
     ssm_b_re, ssm_b_im, ssm_c_re, ssm_c_im, ssm_d, ssm_glu_w, ssm_glu_b, lower_bounds, hg_norm, w_out,
     norm_ffn, ffn_w_up, ffn_conv_w, ffn_w_down, norm_ple, ple_w_gate, ple_w_proj) = prm
    w = w_in[i]
    o_gate = QKV_W
    o_beta = o_gate + DN_WIDTH
    o_a = o_beta + DN_HEADS
    o_u = o_a + DN_HEADS
    o_h = o_u + SSM_WIDTH
    w_packed = jnp.concatenate(
        [w[:, 0:o_gate], w[:, o_gate:o_beta], w[:, o_u:o_h], w[:, o_h:o_h + ZH_W],
         w[:, o_beta:o_u], jnp.zeros((D_MODEL, BA_W - 2 * DN_HEADS), F32)], axis=1).astype(BF16)

    zeros4 = jnp.zeros((DN_HEADS,), F32)
    pad = jnp.zeros((BA_W - 2 * DN_HEADS,), F32)
    coef = jnp.stack([jnp.concatenate([zeros4, -jnp.exp(dn_a_log[i].astype(F32)), pad]),
                      jnp.concatenate([zeros4, dn_dt_bias[i].astype(F32), pad])])

    lre = ssm_lam_re[i].astype(F32)
    lim = ssm_lam_im[i].astype(F32)
    delta = jnp.exp(ssm_log_step[i].astype(F32))[:, None]
    mag = jnp.exp(lre * delta)
    lbr = mag * jnp.cos(lim * delta)
    lbi = mag * jnp.sin(lim * delta)
    den = lre * lre + lim * lim
    fr = ((lbr - 1.0) * lre + lbi * lim) / den
    fi = (lbi * lre - (lbr - 1.0) * lim) / den
    bre = ssm_b_re[i].astype(F32)
    bim = ssm_b_im[i].astype(F32)
    bbr = fr[..., None] * bre - fi[..., None] * bim
    bbi = fr[..., None] * bim + fi[..., None] * bre
    eye_g = jnp.eye(SSM_GROUPS, dtype=F32)

    def bdiag_in(m):
        return jnp.einsum('gph,gk->ghkp', m, eye_g).reshape(SSM_WIDTH, SSM_FLAT)

    def bdiag_out(m):
        return jnp.einsum('ghp,gk->gpkh', m, eye_g).reshape(SSM_FLAT, SSM_WIDTH)

    bmat = jnp.concatenate([bdiag_in(bbr), bdiag_in(bbi)], axis=1)
    bmat_hi = bmat.astype(BF16)
    bmat_lo = (bmat - bmat_hi.astype(F32)).astype(BF16)
    cmat = jnp.concatenate([bdiag_out(ssm_c_re[i].astype(F32)), -bdiag_out(ssm_c_im[i].astype(F32))],
                           axis=0).astype(BF16)
    lam = jnp.stack([lbr.reshape(SSM_FLAT), lbi.reshape(SSM_FLAT)])

    gmat = jnp.kron(jnp.eye(HG_HEADS, dtype=F32), jnp.ones((HG_HD, HG_HD), F32)).astype(BF16)
    return dict(
        norm_mix=norm_mix[i].reshape(1, D_MODEL), w_in=w_packed,
        conv_w=dn_conv_w[i], coef=coef, dn_norm=dn_norm[i].reshape(1, DN_HD),
        bmat_hi=bmat_hi, bmat_lo=bmat_lo, lam=lam, cmat=cmat, ssm_d=ssm_d[i].reshape(1, SSM_WIDTH),
        glu_w=ssm_glu_w[i].astype(BF16), glu_b=ssm_glu_b[i].reshape(1, SSM_WIDTH),
        lb=lower_bounds[i].reshape(1, HG_WIDTH), hg_norm=jnp.tile(hg_norm[i], HG_HEADS).reshape(1, HG_WIDTH),
        gmat=gmat, w_out=w_out[i].astype(BF16), norm_ffn=norm_ffn[i].reshape(1, D_MODEL),
        w_up=ffn_w_up[i].astype(BF16), ffn_conv_w=ffn_conv_w[i], w_down=ffn_w_down[i].astype(BF16),
        norm_ple=norm_ple[i].reshape(1, D_MODEL), ple_gate=ple_w_gate[i].astype(BF16),
        ple_proj=ple_w_proj[i].astype(BF16))


def _trunk(x, p_tm, conv_qkv, delta, ssm_re, ssm_im, hgrn, conv_ffn, layers, norm_final, *,
           bsz, t, seq_major_in, tt, dn_chunk, hg_chunk, s5_steps, tail_rows, ffc):
    n = bsz * t
    outs = {k: [] for k in ('cq', 'dl', 'sr', 'si', 'hg', 'cf')}
    eye_h = jnp.eye(HG_HEADS, dtype=F32)
    h = x
    for i in range(DEPTH):
        lp = layers[i]
        if seq_major_in:
            x_in = h if i == 0 else h.reshape(t, bsz * D_MODEL)
            res = _proj_in_seq(x_in, lp['norm_mix'], lp['w_in'], bsz, t, tt, x_is_btd=(i == 0))
            if i == 0:
                h = res[5].reshape(n, D_MODEL)
            zq, zg, zu, zh, zba = res[:5]
        else:
            zq, zg, zu, zh, zba = [z.reshape(t, bsz * wd) for z, wd in
                                   zip(_proj_in_rows(h, lp['norm_mix'], lp['w_in'], tail_rows), _Z_WIDTHS)]
        cs8 = jnp.pad(conv_qkv[i], ((0, 0), (8 - (DN_CONV - 1), 0), (0, 0)))
        o_a, cq8, dl = _delta_call(zq, zg, zba, lp['conv_w'], lp['coef'], lp['dn_norm'],
                                   cs8, delta[i], bsz, t, tt, dn_chunk)
        o_b, sr, si = _s5_call(zu.reshape(n, SSM_WIDTH), lp['bmat_hi'], lp['bmat_lo'], lp['lam'], lp['cmat'],
                               lp['ssm_d'], lp['glu_w'], lp['glu_b'],
                               ssm_re[i].reshape(bsz, SSM_FLAT), ssm_im[i].reshape(bsz, SSM_FLAT),
                               bsz, t, s5_steps)
        st0 = jnp.einsum('bhkv,hg->bhvgk', hgrn[i], eye_h).reshape(bsz, HG_WIDTH, HG_WIDTH)
        o_c, hg = _hgrn_call(zh, lp['lb'], lp['hg_norm'], lp['gmat'], st0, bsz, t, tt, hg_chunk)
        hg5 = hg.reshape(bsz, HG_HEADS, HG_HD, HG_HEADS, HG_HD)
        hg = jnp.stack([jnp.swapaxes(hg5[:, hh, :, hh, :], -1, -2) for hh in range(HG_HEADS)], axis=1)
        cs = jnp.swapaxes(conv_ffn[i], 0, 1).reshape((FF_CONV - 1) * bsz, 2 * FF_DIM)
        h, cfa, cfb = _tail_call(h, o_a.reshape(n, DN_WIDTH), o_b, o_c.reshape(n, HG_WIDTH),
                                 lp['w_out'], lp['norm_ffn'], lp['w_up'], lp['ffn_conv_w'], lp['w_down'],
                                 cs, p_tm[i], lp['norm_ple'], lp['ple_gate'], lp['ple_proj'], bsz, tail_rows,
                                 ffc)
        cf = jnp.concatenate([cfa[-1], cfb[-1]], axis=1).reshape(FF_CONV - 1, bsz, 2 * FF_DIM)
        outs['cq'].append(cq8[:, 8 - (DN_CONV - 1):, :])
        outs['dl'].append(dl)
        outs['sr'].append(sr.reshape(bsz, SSM_GROUPS, SSM_STATE))
        outs['si'].append(si.reshape(bsz, SSM_GROUPS, SSM_STATE))
        outs['hg'].append(hg)
        outs['cf'].append(jnp.swapaxes(cf, 0, 1))
    if seq_major_in:
        y = _final_seq(h.reshape(t, bsz * D_MODEL), norm_final, bsz, t, tt)
    else:
        y = jnp.swapaxes(_final_rows(h, norm_final, tail_rows).reshape(t, bsz, D_MODEL), 0, 1)
    return (y,) + tuple(jnp.stack(outs[k]) for k in ('cq', 'dl', 'sr', 'si', 'hg', 'cf'))


def kernel(x_prompt, x_sample, p_prompt, p_sample, state_conv_qkv, state_delta, state_ssm_re, state_ssm_im, state_hgrn, state_conv_ffn, norm_mix, w_in, dn_conv_w, dn_a_log, dn_dt_bias, dn_norm, ssm_lam_re, ssm_lam_im, ssm_log_step, ssm_b_re, ssm_b_im, ssm_c_re, ssm_c_im, ssm_d, ssm_glu_w, ssm_glu_b, hg_lower, hg_norm, w_out, norm_ffn, ffn_w_up, ffn_conv_w, ffn_w_down, norm_ple, ple_w_gate, ple_w_proj, norm_final):
    lb_p = jax.nn.softmax(hg_lower.astype(F32), axis=0)
    lower_bounds = jnp.cumsum(lb_p, axis=0) - lb_p[0]
    prm = (norm_mix, w_in, dn_conv_w, dn_a_log, dn_dt_bias, dn_norm, ssm_lam_re, ssm_lam_im, ssm_log_step,
           ssm_b_re, ssm_b_im, ssm_c_re, ssm_c_im, ssm_d, ssm_glu_w, ssm_glu_b, lower_bounds, hg_norm, w_out,
           norm_ffn, ffn_w_up, ffn_conv_w, ffn_w_down, norm_ple, ple_w_gate, ple_w_proj)
    layers = [_pack_layer(i, prm) for i in range(DEPTH)]
    nf = norm_final.reshape(1, D_MODEL)

    bp, tp, _ = x_prompt.shape
    bs, ts, _ = x_sample.shape
    z = lambda *shp: jnp.zeros((DEPTH, bp) + shp, F32)
    pp_tm = jnp.swapaxes(p_prompt, 1, 2).reshape(DEPTH, tp * bp, PLE_DIM)
    ps_tm = jnp.swapaxes(p_sample, 1, 2).reshape(DEPTH, ts * bs, PLE_DIM)
    xs_tm = jnp.swapaxes(x_sample, 0, 1).reshape(ts * bs, D_MODEL)

    prompt = _trunk(x_prompt, pp_tm, z(DN_CONV - 1, QKV_W), z(DN_HEADS, DN_HD, DN_HD),
                    z(SSM_GROUPS, SSM_STATE), z(SSM_GROUPS, SSM_STATE), z(HG_HEADS, HG_HD, HG_HD),
                    z(FF_CONV - 1, 2 * FF_DIM), layers, nf,
                    bsz=bp, t=tp, seq_major_in=True, tt=256, dn_chunk=DN_CHUNK, hg_chunk=HG_CHUNK,
                    s5_steps=64, tail_rows=512, ffc=1408)
    sample = _trunk(xs_tm, ps_tm, state_conv_qkv, state_delta, state_ssm_re, state_ssm_im, state_hgrn,
                    state_conv_ffn, layers, nf,
                    bsz=bs, t=ts, seq_major_in=False, tt=ts, dn_chunk=ts, hg_chunk=ts,
                    s5_steps=ts, tail_rows=512, ffc=256)
    return (prompt[0], sample[0]) + prompt[1:] + sample[1:]
```

```python
import functools
import math

import jax
import jax.numpy as jnp
from jax import lax
from jax.experimental import pallas as pl
from jax.experimental.pallas import tpu as pltpu

F32 = jnp.float32
BF16 = jnp.bfloat16

D_MODEL = 1024
DEPTH = 2
DN_HEADS = 4
DN_WIDTH = 512
DN_HD = 128
DN_CONV = 4
DN_CHUNK = 64
SSM_WIDTH = 256
SSM_GROUP = 16
SSM_GROUPS = 16
SSM_STATE = 64
SSM_FLAT = SSM_GROUPS * SSM_STATE
HG_WIDTH = 256
HG_HEADS = 4
HG_HD = 64
HG_CHUNK = 32
FF_DIM = 2816
FF_CONV = 3
PLE_DIM = 256
EPS = 1e-6

SUBLANES = 8
LANES = 128

QKV_W = 3 * DN_WIDTH
ZH_W = 4 * HG_WIDTH
BA_W = LANES
IN_PACKED = QKV_W + DN_WIDTH + SSM_WIDTH + ZH_W + BA_W
_Z_WIDTHS = (QKV_W, DN_WIDTH, SSM_WIDTH, ZH_W, BA_W)

VMEM_LIMIT = 56 * 1024 * 1024

_NT = (((1,), (1,)), ((), ()))
_TN = (((0,), (0,)), ((), ()))


def _dot(a, b):
    return jnp.dot(a.astype(BF16), b.astype(BF16), preferred_element_type=F32)


def _dot_nt(a, b):
    return lax.dot_general(a.astype(BF16), b.astype(BF16), _NT, preferred_element_type=F32)


def _dot_tn(a, b):
    return lax.dot_general(a.astype(BF16), b.astype(BF16), _TN, preferred_element_type=F32)


def _split3(x):
    hi = x.astype(BF16)
    r1 = x - hi.astype(F32)
    mid = r1.astype(BF16)
    lo = (r1 - mid.astype(F32)).astype(BF16)
    return hi, mid, lo


def _dot_exact(x, m, dims, x_first=True):
    if x_first:
        return sum(lax.dot_general(p, m, dims, preferred_element_type=F32) for p in _split3(x))
    return sum(lax.dot_general(m, p, dims, preferred_element_type=F32) for p in _split3(x))


_MM = (((1,), (0,)), ((), ()))


def _sigmoid(x):
    return 1.0 / (1.0 + jnp.exp(-x))


def _silu(x):
    return x * _sigmoid(x)


def _softplus(x):
    return jnp.maximum(x, 0.0) + jnp.log1p(jnp.exp(-jnp.abs(x)))


def _rms(x, g):
    ms = jnp.mean(x * x, axis=-1, keepdims=True)
    return x * lax.rsqrt(ms + EPS) * g


def _seq_tri(n, nb):
    r = lax.broadcasted_iota(jnp.int32, (n, n), 0)
    c = lax.broadcasted_iota(jnp.int32, (n, n), 1)
    return jnp.where(((r & (nb - 1)) == (c & (nb - 1))) & (c <= r), 1.0, 0.0).astype(BF16)


def _params(n_axes):
    return pltpu.CompilerParams(dimension_semantics=("arbitrary",) * n_axes, vmem_limit_bytes=VMEM_LIMIT)


def _proj_in_body(x_ref, g_ref, w_ref, zq_ref, zg_ref, zu_ref, zh_ref, zba_ref, *h0_ref, tt, nb):
    if h0_ref:
        h0 = h0_ref[0]

        def cp(b, c):
            h0[:, b, :] = x_ref[b]
            return c

        lax.fori_loop(0, nb, cp, 0)
        x = h0[...].reshape(tt * nb, D_MODEL)
    else:
        x = x_ref[...].reshape(tt * nb, D_MODEL)
    xn = _rms(x, g_ref[...]).astype(BF16)
    c0 = 0
    for ref, width in zip((zq_ref, zg_ref, zu_ref, zh_ref, zba_ref), _Z_WIDTHS):
        ref[...] = jnp.dot(xn, w_ref[:, c0:c0 + width], preferred_element_type=F32).reshape(tt, nb, width)
        c0 += width


def _proj_in(x, g, w, bsz, t, tt, nb, from_btd):
    grid = (bsz // nb, t // tt)
    tm = lambda wd: pl.BlockSpec((tt, nb, wd), lambda j, i: (i, j, 0))
    x_spec = pl.BlockSpec((nb, tt, D_MODEL), lambda j, i: (j, i, 0)) if from_btd else tm(D_MODEL)
    out_shape = [jax.ShapeDtypeStruct((t, bsz, wd), F32) for wd in _Z_WIDTHS]
    out_specs = [tm(wd) for wd in _Z_WIDTHS]
    if from_btd:
        out_shape.append(jax.ShapeDtypeStruct((t, bsz, D_MODEL), F32))
        out_specs.append(tm(D_MODEL))
    return pl.pallas_call(
        functools.partial(_proj_in_body, tt=tt, nb=nb),
        grid=grid,
        in_specs=[x_spec,
                  pl.BlockSpec((1, D_MODEL), lambda j, i: (0, 0)),
                  pl.BlockSpec((D_MODEL, IN_PACKED), lambda j, i: (0, 0))],
        out_specs=out_specs,
        out_shape=out_shape,
        compiler_params=_params(2),
        name="proj_in",
    )(x, g, w)


def _delta_body(zq_ref, zg_ref, zba_ref, cw_ref, coef_ref, dnw_ref, cs_ref, s0_ref,
                o_ref, cso_ref, s_ref,
                buf, qb, kb, vb, pb, gcb, ob, *, tt, n_tiles):
    ti = pl.program_id(1)
    nb = SUBLANES
    n = tt * nb
    hist = DN_CONV - 1
    levels = int(math.log2(tt))

    @pl.when(ti == 0)
    def _():
        s_ref[...] = s0_ref[...]
        for b in range(nb):
            buf[0:hist, b, :] = cs_ref[b]

    buf[hist:hist + tt] = zq_ref[...]
    cw = cw_ref[...]
    y = cw[0:1] * buf[0:tt].reshape(n, QKV_W)
    for j in range(1, DN_CONV):
        y = y + cw[j:j + 1] * buf[j:j + tt].reshape(n, QKV_W)

    @pl.when(ti == n_tiles - 1)
    def _():
        for b in range(nb):
            cso_ref[b] = buf[tt:tt + hist, b, :]

    buf[0:hist] = buf[tt:tt + hist]

    y = _silu(y)
    for h in range(DN_HEADS):
        qs = slice(h * DN_HD, (h + 1) * DN_HD)
        ks = slice(DN_WIDTH + h * DN_HD, DN_WIDTH + (h + 1) * DN_HD)
        qh = y[:, qs]
        kh = y[:, ks]
        qn = qh * (lax.rsqrt(jnp.sum(qh * qh, axis=-1, keepdims=True) + EPS) * DN_HD ** -0.5)
        kn = kh * lax.rsqrt(jnp.sum(kh * kh, axis=-1, keepdims=True) + EPS)
        qb[h] = qn
        kb[h] = kn
        vb[h] = y[:, 2 * DN_WIDTH + h * DN_HD:2 * DN_WIDTH + (h + 1) * DN_HD]

    zba = zba_ref[...].reshape(n, BA_W)
    coef = coef_ref[...]
    lane = lax.broadcasted_iota(jnp.int32, zba.shape, 1)
    g = coef[0:1] * _softplus(zba + coef[1:2])
    pb[...] = jnp.where(lane < DN_HEADS, _sigmoid(zba), g)
    gcb[...] = _dot_exact(g, _seq_tri(n, nb), _MM, x_first=False)

    ri = lax.broadcasted_iota(jnp.int32, (tt, tt), 0)
    ci = lax.broadcasted_iota(jnp.int32, (tt, tt), 1)
    causal = ci <= ri
    strict = ci < ri
    eye = jnp.where(ci == ri, 1.0, 0.0).astype(F32)
    sel = jnp.where(lax.broadcasted_iota(jnp.int32, (SUBLANES, BA_W), 0)
                    == lax.broadcasted_iota(jnp.int32, (SUBLANES, BA_W), 1), 1.0, 0.0).astype(BF16)

    def pair(bp, carry_):
        chains = []
        for s in range(2):
            b = bp * 2 + s
            seq_rows = pl.ds(b, tt, stride=nb)
            pbv = pb[seq_rows, :]
            gcv = gcb[seq_rows, :]
            gct = _dot_exact(gcv, sel, _NT, x_first=False)
            for h in range(DN_HEADS):
                chains.append(dict(b=b, h=h, rows=seq_rows, q=qb[h, seq_rows, :], k=kb[h, seq_rows, :],
                                   v=vb[h, seq_rows, :],
                                   beta=pbv[:, h:h + 1], gcol=gcv[:, DN_HEADS + h:DN_HEADS + h + 1],
                                   grow=gct[DN_HEADS + h:DN_HEADS + h + 1, :]))
        for c in chains:
            c['dec'] = jnp.where(causal, jnp.exp(jnp.where(causal, c['gcol'] - c['grow'], 0.0)), 0.0)
            c['qkk'] = _dot_nt(jnp.concatenate([c['q'], c['k']], axis=0), c['k'])
        for c in chains:
            c['p'] = -jnp.where(strict, c['beta'] * c['qkk'][tt:2 * tt] * c['dec'], 0.0)
            c['t'] = eye + c['p']
        for _ in range(levels - 1):
            for c in chains:
                c['p'] = _dot(c['p'], c['p'])
            for c in chains:
                c['t'] = c['t'] + _dot(c['p'], c['t'])
        for c in chains:
            eg = jnp.exp(c['gcol'])
            rhs = jnp.concatenate([c['beta'] * c['v'], (c['beta'] * eg) * c['k']], axis=1)
            c['uw'] = _dot(c['t'], rhs)
            c['qd'] = c['q'] * eg
            c['glast'] = c['grow'][:, tt - 1:tt]
            c['kd'] = c['k'] * jnp.exp(c['glast'] - c['gcol'])
            c['s'] = s_ref[c['b'], c['h']]
        for c in chains:
            c['r'] = _dot(jnp.concatenate([c['uw'][:, DN_HD:2 * DN_HD], c['qd']], axis=0), c['s'])
        for c in chains:
            c['vn'] = c['uw'][:, 0:DN_HD] - c['r'][0:tt]
        for c in chains:
            ob[c['h'], c['rows'], :] = c['r'][tt:2 * tt] + _dot(c['qkk'][0:tt] * c['dec'], c['vn'])
            s_ref[c['b'], c['h']] = c['s'] * jnp.exp(c['glast']) + _dot_tn(c['kd'], c['vn'])
        return carry_

    lax.fori_loop(0, nb // 2, pair, 0)

    gate = zg_ref[...].reshape(n, DN_WIDTH)
    dnw = dnw_ref[...]
    for h in range(DN_HEADS):
        hs = slice(h * DN_HD, (h + 1) * DN_HD)
        o_ref[:, :, hs] = (_rms(ob[h], dnw) * _silu(gate[:, hs])).reshape(tt, nb, DN_HD)


def _delta_call(zq, zg, zba, cw, coef, dnw, cs, s0, bsz, t, tt):
    nb = SUBLANES
    n_tiles = t // tt
    hist = DN_CONV - 1
    tm = lambda wd: pl.BlockSpec((tt, nb, wd), lambda j, i: (i, j, 0))
    full = lambda shp: pl.BlockSpec(shp, lambda j, i: tuple(0 for _ in shp))
    cs_spec = pl.BlockSpec((nb, hist, QKV_W), lambda j, i: (j, 0, 0))
    st_spec = pl.BlockSpec((nb, DN_HEADS, DN_HD, DN_HD), lambda j, i: (j, 0, 0, 0))
    heads = pltpu.VMEM((DN_HEADS, tt * nb, DN_HD), F32)
    rows = pltpu.VMEM((tt * nb, BA_W), F32)
    return pl.pallas_call(
        functools.partial(_delta_body, tt=tt, n_tiles=n_tiles),
        grid=(bsz // nb, n_tiles),
        in_specs=[tm(QKV_W), tm(DN_WIDTH), tm(BA_W),
                  full((DN_CONV, QKV_W)), full((2, BA_W)), full((1, DN_HD)), cs_spec, st_spec],
        out_specs=[tm(DN_WIDTH), cs_spec, st_spec],
        out_shape=[jax.ShapeDtypeStruct((t, bsz, DN_WIDTH), F32),
                   jax.ShapeDtypeStruct((bsz, hist, QKV_W), F32),
                   jax.ShapeDtypeStruct((bsz, DN_HEADS, DN_HD, DN_HD), F32)],
        scratch_shapes=[pltpu.VMEM((tt + hist, nb, QKV_W), F32), heads, heads, heads, rows, rows, heads],
        compiler_params=_params(2),
        name="delta",
    )(zq, zg, zba, cw, coef, dnw, cs, s0)


def _hgrn_body(zh_ref, lb_ref, nw_ref, gm_ref, s0_ref, o_ref, s_ref,
               sbd, qsb, ksb, qeb, kdb, vb, elb, ob, *, tt, n_tiles):
    ti = pl.program_id(1)
    nb = SUBLANES
    n = tt * nb
    mid = tt // 2 - 1
    sh_h = int(math.log2(HG_HD))

    pr = lax.broadcasted_iota(jnp.int32, (HG_HD, HG_WIDTH), 0)
    pc = lax.broadcasted_iota(jnp.int32, (HG_HD, HG_WIDTH), 1)
    place = [jnp.where(pc == pr + h * HG_HD, 1.0, 0.0).astype(BF16) for h in range(HG_HEADS)]

    @pl.when(ti == 0)
    def _():
        def init(b, c):
            for h in range(HG_HEADS):
                sbd[b, h * HG_HD:(h + 1) * HG_HD, :] = _dot_exact(s0_ref[b, h], place[h], _TN)
            return c
        lax.fori_loop(0, nb, init, 0)

    zh = zh_ref[...].reshape(n, ZH_W)
    lb = lb_ref[...]
    f = lb + (1.0 - lb) * _sigmoid(zh[:, HG_WIDTH:2 * HG_WIDTH])
    q = _silu(zh[:, 0:HG_WIDTH]).reshape(tt, nb, HG_WIDTH)
    k = (1.0 - f).reshape(tt, nb, HG_WIDTH)
    bc = _dot_exact(jnp.log(f), _seq_tri(n, nb), _MM, x_first=False).reshape(tt, nb, HG_WIDTH)
    bm = bc[mid]
    bl = bc[tt - 1]
    half = HG_WIDTH // 2
    for dst, val in ((qsb, q * jnp.exp(bc - bm[None])), (ksb, k * jnp.exp(bm[None] - bc)),
                     (qeb, q * jnp.exp(bc)), (kdb, k * jnp.exp(bl[None] - bc))):
        val = val.reshape(n, HG_WIDTH)
        dst[0] = val[:, 0:half]
        dst[1] = val[:, half:HG_WIDTH]
    vb[0] = zh[:, 2 * HG_WIDTH:2 * HG_WIDTH + half]
    vb[1] = zh[:, 2 * HG_WIDTH + half:3 * HG_WIDTH]
    elb[...] = jnp.exp(bl)

    ri = lax.broadcasted_iota(jnp.int32, (HG_HEADS * tt, tt), 0)
    ci = lax.broadcasted_iota(jnp.int32, (HG_HEADS * tt, tt), 1)
    causal = ci <= (ri & (tt - 1))
    lane_head = lax.broadcasted_iota(jnp.int32, (tt, HG_WIDTH), 1) >> sh_h
    hmask = [jnp.where(lane_head == h, 1.0, 0.0).astype(F32) for h in range(HG_HEADS)]
    bd = ((lax.broadcasted_iota(jnp.int32, (HG_WIDTH, HG_WIDTH), 0) >> sh_h)
          == (lax.broadcasted_iota(jnp.int32, (HG_WIDTH, HG_WIDTH), 1) >> sh_h))

    def pair(bp, carry_):
        seqs = []
        for s in range(2):
            b = bp * 2 + s
            seq_rows = pl.ds(b, tt, stride=nb)
            seq = lambda ref: jnp.concatenate([ref[0, seq_rows, :], ref[1, seq_rows, :]], axis=1)
            seqs.append(dict(b=b, rows=seq_rows, qs=seq(qsb), ks=seq(ksb), qe=seq(qeb), kd=seq(kdb),
                             v=seq(vb), st=sbd[b], el=elb[pl.ds(b, 1), :]))
        for c in seqs:
            c['a'] = jnp.where(causal, _dot_nt(jnp.concatenate([c['qs'] * m for m in hmask], axis=0), c['ks']), 0.0)
            c['o'] = _dot_nt(c['qe'], c['st'])
            c['kv'] = _dot_tn(c['v'], c['kd'])
        for c in seqs:
            o = c['o']
            for h in range(HG_HEADS):
                o = o + hmask[h] * _dot(c['a'][h * tt:(h + 1) * tt], c['v'])
            ob[0, c['rows'], :] = o[:, 0:half]
            ob[1, c['rows'], :] = o[:, half:HG_WIDTH]
            sbd[c['b']] = c['st'] * c['el'] + jnp.where(bd, c['kv'], 0.0)
        return carry_

    lax.fori_loop(0, nb // 2, pair, 0)

    o = jnp.concatenate([ob[0], ob[1]], axis=1)
    ms = _dot_exact(o * o, gm_ref[...], _MM) * (1.0 / HG_HD)
    o_ref[...] = (o * lax.rsqrt(ms + EPS) * nw_ref[...]
                  * _silu(zh[:, 3 * HG_WIDTH:4 * HG_WIDTH])).reshape(tt, nb, HG_WIDTH)

    @pl.when(ti == n_tiles - 1)
    def _():
        def fin(b, c):
            for h in range(HG_HEADS):
                s_ref[b, h] = _dot_exact(sbd[b, h * HG_HD:(h + 1) * HG_HD, :], place[h], _NT, x_first=False)
            return c
        lax.fori_loop(0, nb, fin, 0)


def _hgrn_call(zh, lb, nw, gm, s0, bsz, t, tt):
    nb = SUBLANES
    n_tiles = t // tt
    full = lambda shp: pl.BlockSpec(shp, lambda j, i: tuple(0 for _ in shp))
    st_spec = pl.BlockSpec((nb, HG_HEADS, HG_HD, HG_HD), lambda j, i: (j, 0, 0, 0))
    tile = pltpu.VMEM((HG_WIDTH // LANES, tt * nb, LANES), F32)
    return pl.pallas_call(
        functools.partial(_hgrn_body, tt=tt, n_tiles=n_tiles),
        grid=(bsz // nb, n_tiles),
        in_specs=[pl.BlockSpec((tt, nb, ZH_W), lambda j, i: (i, j, 0)),
                  full((1, HG_WIDTH)), full((1, HG_WIDTH)), full((HG_WIDTH, HG_WIDTH)), st_spec],
        out_specs=[pl.BlockSpec((tt, nb, HG_WIDTH), lambda j, i: (i, j, 0)), st_spec],
        out_shape=[jax.ShapeDtypeStruct((t, bsz, HG_WIDTH), F32),
                   jax.ShapeDtypeStruct((bsz, HG_HEADS, HG_HD, HG_HD), F32)],
        scratch_shapes=[pltpu.VMEM((nb, HG_WIDTH, HG_WIDTH), F32),
                        tile, tile, tile, tile, tile, pltpu.VMEM((nb, HG_WIDTH), F32), tile],
        compiler_params=_params(2),
        name="hgrn",
    )(zh, lb, nw, gm, s0)


def _s5_body(u_ref, bh_ref, bl_ref, lam_ref, c_ref, d_ref, gw_ref, gb_ref, x0r_ref, x0i_ref,
             o_ref, xr_ref, xi_ref, xs, *, bsz, steps):
    i = pl.program_id(0)

    @pl.when(i == 0)
    def _():
        xr_ref[...] = x0r_ref[...]
        xi_ref[...] = x0i_ref[...]

    u = u_ref[...].reshape(steps * bsz, SSM_WIDTH)
    u_hi = u.astype(BF16)
    u_lo = (u - u_hi.astype(F32)).astype(BF16)
    bh = bh_ref[...]
    xs[...] = (jnp.dot(u_hi, bh, preferred_element_type=F32) + jnp.dot(u_lo, bh, preferred_element_type=F32)
               + jnp.dot(u_hi, bl_ref[...], preferred_element_type=F32))

    lr = jnp.broadcast_to(lam_ref[0:1, :], (SUBLANES, SSM_FLAT))
    li = jnp.broadcast_to(lam_ref[1:2, :], (SUBLANES, SSM_FLAT))

    def row_block(rb, carry_):
        r0 = pl.multiple_of(rb * SUBLANES, SUBLANES)
        xr0 = xr_ref[pl.ds(r0, SUBLANES), :]
        xi0 = xi_ref[pl.ds(r0, SUBLANES), :]

        def tstep(t, st):
            xr, xi = st
            row = pl.ds(pl.multiple_of(t * bsz + r0, SUBLANES), SUBLANES)
            nr = lr * xr - li * xi + xs[row, 0:SSM_FLAT]
            ni = lr * xi + li * xr + xs[row, SSM_FLAT:2 * SSM_FLAT]
            xs[row, 0:SSM_FLAT] = nr
            xs[row, SSM_FLAT:2 * SSM_FLAT] = ni
            return nr, ni

        xr1, xi1 = lax.fori_loop(0, steps, tstep, (xr0, xi0))
        xr_ref[pl.ds(r0, SUBLANES), :] = xr1
        xi_ref[pl.ds(r0, SUBLANES), :] = xi1
        return carry_

    lax.fori_loop(0, bsz // SUBLANES, row_block, 0)

    y = _dot(xs[...], c_ref[...]) + d_ref[...] * u
    y = 0.5 * y * (1.0 + jnp.tanh(math.sqrt(2.0 / math.pi) * (y + 0.044715 * (y * y * y))))
    o_ref[...] = (y * _sigmoid(_dot(y, gw_ref[...]) + gb_ref[...])).reshape(steps, bsz, SSM_WIDTH)


def _s5_call(u, bh, bl, lam, cm, d, gw, gb, x0r, x0i, bsz, t, steps):
    rows = steps * bsz
    full = lambda shp: pl.BlockSpec(shp, lambda i: tuple(0 for _ in shp))
    return pl.pallas_call(
        functools.partial(_s5_body, bsz=bsz, steps=steps),
        grid=(t // steps,),
        in_specs=[pl.BlockSpec((steps, bsz, SSM_WIDTH), lambda i: (i, 0, 0)),
                  full((SSM_WIDTH, 2 * SSM_FLAT)), full((SSM_WIDTH, 2 * SSM_FLAT)),
                  full((2, SSM_FLAT)), full((2 * SSM_FLAT, SSM_WIDTH)), full((1, SSM_WIDTH)),
                  full((SSM_WIDTH, SSM_WIDTH)), full((1, SSM_WIDTH)),
                  full((bsz, SSM_FLAT)), full((bsz, SSM_FLAT))],
        out_specs=[pl.BlockSpec((steps, bsz, SSM_WIDTH), lambda i: (i, 0, 0)),
                   full((bsz, SSM_FLAT)), full((bsz, SSM_FLAT))],
        out_shape=[jax.ShapeDtypeStruct((t, bsz, SSM_WIDTH), F32),
                   jax.ShapeDtypeStruct((bsz, SSM_FLAT), F32),
                   jax.ShapeDtypeStruct((bsz, SSM_FLAT), F32)],
        scratch_shapes=[pltpu.VMEM((rows, 2 * SSM_FLAT), F32)],
        compiler_params=_params(1),
        name="s5",
    )(u, bh, bl, lam, cm, d, gw, gb, x0r, x0i)


def _tail_body(h_ref, oa_ref, ob_ref, oc_ref, wo_ref, nf_ref, wua_ref, wub_ref, cwa_ref, cwb_ref,
               wd_ref, csa_ref, csb_ref, p_ref, npl_ref, wg_ref, wp_ref,
               out_ref, cso_a_ref, cso_b_ref,
               hn, upa, upb, p3, *cars, tt, nb, n_ffc, ffc):
    i = pl.program_id(1)
    j = pl.program_id(2)
    n = tt * nb
    hist = FF_CONV - 1

    @pl.when(j == 0)
    def _():
        h2 = (h_ref[...].reshape(n, D_MODEL) + _dot(oa_ref[...].reshape(n, DN_WIDTH), wo_ref[0:DN_WIDTH, :])
              + _dot(ob_ref[...].reshape(n, SSM_WIDTH), wo_ref[DN_WIDTH:DN_WIDTH + SSM_WIDTH, :])
              + _dot(oc_ref[...].reshape(n, HG_WIDTH), wo_ref[DN_WIDTH + SSM_WIDTH:D_MODEL, :]))
        out_ref[...] = h2.reshape(tt, nb, D_MODEL)
        hn[...] = _rms(h2, nf_ref[...]).astype(BF16)

    hnv = hn[...]
    halves = []
    for idx, (up, wu, cw, cs, cso) in enumerate(((upa, wua_ref, cwa_ref, csa_ref, cso_a_ref),
                                                 (upb, wub_ref, cwb_ref, csb_ref, cso_b_ref))):
        @pl.when(i == 0)
        def _(up=up, cs=cs):
            def cp(b, c):
                up[0:hist, b, :] = cs[b]
                return c
            lax.fori_loop(0, nb, cp, 0)

        if cars:
            @pl.when(i > 0)
            def _(up=up, car=cars[idx]):
                up[0:hist] = car[j]

        up[hist:hist + tt] = jnp.dot(hnv, wu[...], preferred_element_type=F32).reshape(tt, nb, ffc)
        last = up[tt:tt + hist]
        if cars:
            cars[idx][j] = last
        cso[...] = last
        w = cw[...]
        acc = w[0:1] * up[0:tt].reshape(n, ffc)
        for s in range(1, FF_CONV):
            acc = acc + w[s:s + 1] * up[s:s + tt].reshape(n, ffc)
        halves.append(acc)
    out_ref[...] += _dot(_silu(halves[0]) * halves[1], wd_ref[...]).reshape(tt, nb, D_MODEL)

    @pl.when(j == n_ffc - 1)
    def _():
        def cp(b, c):
            p3[:, b, :] = p_ref[b]
            return c
        lax.fori_loop(0, nb, cp, 0)
        h3 = out_ref[...].reshape(n, D_MODEL)
        gate = _sigmoid(_dot(_rms(h3, npl_ref[...]), wg_ref[...]))
        out_ref[...] = (h3 + gate * _dot(p3[...].reshape(n, PLE_DIM), wp_ref[...])).reshape(tt, nb, D_MODEL)


def _tail_call(h, oa, ob, oc, wo, nf, wu, cw, wd, cs, p, npl, wg, wp, bsz, t, tt, nb, ffc):
    n_ffc = FF_DIM // ffc
    n_t = t // tt
    hist = FF_CONV - 1
    tm = lambda wd_: pl.BlockSpec((tt, nb, wd_), lambda jb, i, j: (i, jb, 0))
    full = lambda shp: pl.BlockSpec(shp, lambda jb, i, j: tuple(0 for _ in shp),
                                    pipeline_mode=pl.Buffered(1))
    cs_a = pl.BlockSpec((nb, hist, ffc), lambda jb, i, j: (jb, 0, j))
    cs_b = pl.BlockSpec((nb, hist, ffc), lambda jb, i, j: (jb, 0, n_ffc + j))
    cso = pl.BlockSpec((None, hist, nb, ffc), lambda jb, i, j: (i, 0, jb, j))
    scratch = [pltpu.VMEM((tt * nb, D_MODEL), BF16),
               pltpu.VMEM((tt + hist, nb, ffc), F32),
               pltpu.VMEM((tt + hist, nb, ffc), F32),
               pltpu.VMEM((tt, nb, PLE_DIM), F32)]
    if n_t > 1:
        scratch += [pltpu.VMEM((n_ffc, hist, nb, ffc), F32)] * 2
    return pl.pallas_call(
        functools.partial(_tail_body, tt=tt, nb=nb, n_ffc=n_ffc, ffc=ffc),
        grid=(bsz // nb, n_t, n_ffc),
        in_specs=[tm(D_MODEL), tm(DN_WIDTH), tm(SSM_WIDTH), tm(HG_WIDTH),
                  full((D_MODEL, D_MODEL)), full((1, D_MODEL)),
                  pl.BlockSpec((D_MODEL, ffc), lambda jb, i, j: (0, j)),
                  pl.BlockSpec((D_MODEL, ffc), lambda jb, i, j: (0, n_ffc + j)),
                  pl.BlockSpec((FF_CONV, ffc), lambda jb, i, j: (0, j)),
                  pl.BlockSpec((FF_CONV, ffc), lambda jb, i, j: (0, n_ffc + j)),
                  pl.BlockSpec((ffc, D_MODEL), lambda jb, i, j: (j, 0)),
                  cs_a, cs_b,
                  pl.BlockSpec((nb, tt, PLE_DIM), lambda jb, i, j: (jb, i, 0)),
                  full((1, D_MODEL)), full((D_MODEL, D_MODEL)), full((PLE_DIM, D_MODEL))],
        out_specs=[tm(D_MODEL), cso, cso],
        out_shape=[jax.ShapeDtypeStruct((t, bsz, D_MODEL), F32),
                   jax.ShapeDtypeStruct((n_t, hist, bsz, FF_DIM), F32),
                   jax.ShapeDtypeStruct((n_t, hist, bsz, FF_DIM), F32)],
        scratch_shapes=scratch,
        compiler_params=_params(3),
        name="tail",
    )(h, oa, ob, oc, wo, nf, wu, wu, cw, cw, wd, cs, cs, p, npl, wg, wp)


def _final_body(h_ref, g_ref, o_ref, *, nb):
    g = g_ref[...]

    def one(b, c):
        o_ref[b] = _rms(h_ref[:, b, :], g)
        return c

    lax.fori_loop(0, nb, one, 0)


def _final_call(h, g, bsz, t, tt, nb):
    return pl.pallas_call(
        functools.partial(_final_body, nb=nb),
        grid=(bsz // nb, t // tt),
        in_specs=[pl.BlockSpec((tt, nb, D_MODEL), lambda j, i: (i, j, 0)),
                  pl.BlockSpec((1, D_MODEL), lambda j, i: (0, 0))],
        out_specs=pl.BlockSpec((nb, tt, D_MODEL), lambda j, i: (j, i, 0)),
        out_shape=jax.ShapeDtypeStruct((bsz, t, D_MODEL), F32),
        compiler_params=_params(2),
        name="final",
    )(h, g)


def _pack_layer(i, prm):
    (norm_mix, w_in, dn_conv_w, dn_a_log, dn_dt_bias, dn_norm, ssm_lam_re, ssm_lam_im, ssm_log_step,
     ssm_b_re, ssm_b_im, ssm_c_re, ssm_c_im, ssm_d, ssm_glu_w, ssm_glu_b, lower_bounds, hg_norm, w_out,
     norm_ffn, ffn_w_up, ffn_conv_w, ffn_w_down, norm_ple, ple_w_gate, ple_w_proj) = prm
    w = w_in[i]
    o_gate = QKV_W
    o_beta = o_gate + DN_WIDTH
    o_a = o_beta + DN_HEADS
    o_u = o_a + DN_HEADS
    o_h = o_u + SSM_WIDTH
    w_packed = jnp.concatenate(
        [w[:, 0:o_gate], w[:, o_gate:o_beta], w[:, o_u:o_h], w[:, o_h:o_h + ZH_W],
         w[:, o_beta:o_u], jnp.zeros((D_MODEL, BA_W - 2 * DN_HEADS), F32)], axis=1).astype(BF16)

    zeros4 = jnp.zeros((DN_HEADS,), F32)
    pad = jnp.zeros((BA_W - 2 * DN_HEADS,), F32)
    coef = jnp.stack([jnp.concatenate([zeros4, -jnp.exp(dn_a_log[i].astype(F32)), pad]),
                      jnp.concatenate([zeros4, dn_dt_bias[i].astype(F32), pad])])

    lre = ssm_lam_re[i].astype(F32)
    lim = ssm_lam_im[i].astype(F32)
    delta = jnp.exp(ssm_log_step[i].astype(F32))[:, None]
    mag = jnp.exp(lre * delta)
    lbr = mag * jnp.cos(lim * delta)
    lbi = mag * jnp.sin(lim * delta)
    den = lre * lre + lim * lim
    fr = ((lbr - 1.0) * lre + lbi * lim) / den
    fi = (lbi * lre - (lbr - 1.0) * lim) / den
    bre = ssm_b_re[i].astype(F32)
    bim = ssm_b_im[i].astype(F32)
    bbr = fr[..., None] * bre - fi[..., None] * bim
    bbi = fr[..., None] * bim + fi[..., None] * bre
    eye_g = jnp.eye(SSM_GROUPS, dtype=F32)

    def bdiag_in(m):
        return jnp.einsum('gph,gk->ghkp', m, eye_g).reshape(SSM_WIDTH, SSM_FLAT)

    def bdiag_out(m):
        return jnp.einsum('ghp,gk->gpkh', m, eye_g).reshape(SSM_FLAT, SSM_WIDTH)

    bmat = jnp.concatenate([bdiag_in(bbr), bdiag_in(bbi)], axis=1)
    bmat_hi = bmat.astype(BF16)
    bmat_lo = (bmat - bmat_hi.astype(F32)).astype(BF16)
    cmat = jnp.concatenate([bdiag_out(ssm_c_re[i].astype(F32)), -bdiag_out(ssm_c_im[i].astype(F32))],
                           axis=0).astype(BF16)
    lam = jnp.stack([lbr.reshape(SSM_FLAT), lbi.reshape(SSM_FLAT)])

    gmat = jnp.kron(jnp.eye(HG_HEADS, dtype=F32), jnp.ones((HG_HD, HG_HD), F32)).astype(BF16)
    return dict(
        norm_mix=norm_mix[i].reshape(1, D_MODEL), w_in=w_packed,
        conv_w=dn_conv_w[i], coef=coef, dn_norm=dn_norm[i].reshape(1, DN_HD),
        bmat_hi=bmat_hi, bmat_lo=bmat_lo, lam=lam, cmat=cmat, ssm_d=ssm_d[i].reshape(1, SSM_WIDTH),
        glu_w=ssm_glu_w[i].astype(BF16), glu_b=ssm_glu_b[i].reshape(1, SSM_WIDTH),
        lb=lower_bounds[i].reshape(1, HG_WIDTH), hg_norm=jnp.tile(hg_norm[i], HG_HEADS).reshape(1, HG_WIDTH),
        gmat=gmat, w_out=w_out[i].astype(BF16), norm_ffn=norm_ffn[i].reshape(1, D_MODEL),
        w_up=ffn_w_up[i].astype(BF16), ffn_conv_w=ffn_conv_w[i], w_down=ffn_w_down[i].astype(BF16),
        norm_ple=norm_ple[i].reshape(1, D_MODEL), ple_gate=ple_w_gate[i].astype(BF16),
        ple_proj=ple_w_proj[i].astype(BF16))


def _tiles(bsz, t):
    if t >= DN_CHUNK:
        return dict(tok_tt=512 // bsz, tok_nb=bsz, dn_tt=DN_CHUNK, hg_tt=HG_CHUNK, s5_steps=512 // bsz,
                    ffc=1408, fin_tt=128)
    return dict(tok_tt=t, tok_nb=512 // t, dn_tt=t, hg_tt=t, s5_steps=t, ffc=256, fin_tt=t)


def _trunk(x, p, conv_qkv, delta, ssm_re, ssm_im, hgrn, conv_ffn, layers, norm_final):
    bsz, t, _ = x.shape
    tl = _tiles(bsz, t)
    outs = {k: [] for k in ('cq', 'dl', 'sr', 'si', 'hg', 'cf')}
    h = x
    for i in range(DEPTH):
        lp = layers[i]
        res = _proj_in(h, lp['norm_mix'], lp['w_in'], bsz, t, tl['tok_tt'], tl['tok_nb'], from_btd=(i == 0))
        if i == 0:
            h = res[5]
        zq, zg, zu, zh, zba = res[:5]
        o_a, cq, dl = _delta_call(zq, zg, zba, lp['conv_w'], lp['coef'], lp['dn_norm'],
                                  conv_qkv[i], delta[i], bsz, t, tl['dn_tt'])
        o_b, sr, si = _s5_call(zu, lp['bmat_hi'], lp['bmat_lo'], lp['lam'], lp['cmat'],
                               lp['ssm_d'], lp['glu_w'], lp['glu_b'],
                               ssm_re[i].reshape(bsz, SSM_FLAT), ssm_im[i].reshape(bsz, SSM_FLAT),
                               bsz, t, tl['s5_steps'])
        o_c, hg = _hgrn_call(zh, lp['lb'], lp['hg_norm'], lp['gmat'], hgrn[i], bsz, t, tl['hg_tt'])
        h, cfa, cfb = _tail_call(h, o_a, o_b, o_c, lp['w_out'], lp['norm_ffn'], lp['w_up'], lp['ffn_conv_w'],
                                 lp['w_down'], conv_ffn[i], p[i], lp['norm_ple'], lp['ple_gate'],
                                 lp['ple_proj'], bsz, t, tl['tok_tt'], tl['tok_nb'], tl['ffc'])
        outs['cq'].append(cq)
        outs['dl'].append(dl)
        outs['sr'].append(sr.reshape(bsz, SSM_GROUPS, SSM_STATE))
        outs['si'].append(si.reshape(bsz, SSM_GROUPS, SSM_STATE))
        outs['hg'].append(hg)
        outs['cf'].append(jnp.swapaxes(jnp.concatenate([cfa[-1], cfb[-1]], axis=-1), 0, 1))
    y = _final_call(h, norm_final, bsz, t, tl['fin_tt'], tl['tok_nb'])
    return (y,) + tuple(jnp.stack(outs[k]) for k in ('cq', 'dl', 'sr', 'si', 'hg', 'cf'))


def kernel(x_prompt, x_sample, p_prompt, p_sample, state_conv_qkv, state_delta, state_ssm_re, state_ssm_im, state_hgrn, state_conv_ffn, norm_mix, w_in, dn_conv_w, dn_a_log, dn_dt_bias, dn_norm, ssm_lam_re, ssm_lam_im, ssm_log_step, ssm_b_re, ssm_b_im, ssm_c_re, ssm_c_im, ssm_d, ssm_glu_w, ssm_glu_b, hg_lower, hg_norm, w_out, norm_ffn, ffn_w_up, ffn_conv_w, ffn_w_down, norm_ple, ple_w_gate, ple_w_proj, norm_final):
    lb_p = jax.nn.softmax(hg_lower.astype(F32), axis=0)
    lower_bounds = jnp.cumsum(lb_p, axis=0) - lb_p[0]
    prm = (norm_mix, w_in, dn_conv_w, dn_a_log, dn_dt_bias, dn_norm, ssm_lam_re, ssm_lam_im, ssm_log_step,
           ssm_b_re, ssm_b_im, ssm_c_re, ssm_c_im, ssm_d, ssm_glu_w, ssm_glu_b, lower_bounds, hg_norm, w_out,
           norm_ffn, ffn_w_up, ffn_conv_w, ffn_w_down, norm_ple, ple_w_gate, ple_w_proj)
    layers = [_pack_layer(i, prm) for i in range(DEPTH)]
    nf = norm_final.reshape(1, D_MODEL)

    bp = x_prompt.shape[0]
    z = lambda *shp: jnp.zeros((DEPTH, bp) + shp, F32)
    prompt = _trunk(x_prompt, p_prompt, z(DN_CONV - 1, QKV_W), z(DN_HEADS, DN_HD, DN_HD),
                    z(SSM_GROUPS, SSM_STATE), z(SSM_GROUPS, SSM_STATE), z(HG_HEADS, HG_HD, HG_HD),
                    z(FF_CONV - 1, 2 * FF_DIM), layers, nf)
    sample = _trunk(x_sample, p_sample, state_conv_qkv, state_delta, state_ssm_re, state_ssm_im, state_hgrn,
                    state_conv_ffn, layers, nf)
    return (prompt[0], sample[0]) + prompt[1:] + sample[1:]
```

```python
import functools
import math

import jax
import jax.numpy as jnp
from jax import lax
from jax.experimental import pallas as pl
from jax.experimental.pallas import tpu as pltpu

F32 = jnp.float32
BF16 = jnp.bfloat16

D_MODEL = 1024
DEPTH = 2
DN_HEADS = 4
DN_WIDTH = 512
DN_HD = 128
DN_CONV = 4
DN_CHUNK = 64
SSM_WIDTH = 256
SSM_GROUP = 16
SSM_GROUPS = 16
SSM_STATE = 64
SSM_FLAT = SSM_GROUPS * SSM_STATE
HG_WIDTH = 256
HG_HEADS = 4
HG_HD = 64
HG_CHUNK = 32
FF_DIM = 2816
FF_CONV = 3
PLE_DIM = 256
EPS = 1e-6

SUBLANES = 8
LANES = 128

QKV_W = 3 * DN_WIDTH
ZH_W = 4 * HG_WIDTH
BA_W = LANES
IN_PACKED = QKV_W + DN_WIDTH + SSM_WIDTH + ZH_W + BA_W
_Z_WIDTHS = (QKV_W, DN_WIDTH, SSM_WIDTH, ZH_W, BA_W)

VMEM_LIMIT = 56 * 1024 * 1024

_NT = (((1,), (1,)), ((), ()))
_TN = (((0,), (0,)), ((), ()))


def _dot(a, b):
    return jnp.dot(a.astype(BF16), b.astype(BF16), preferred_element_type=F32)


def _dot_nt(a, b):
    return lax.dot_general(a.astype(BF16), b.astype(BF16), _NT, preferred_element_type=F32)


def _dot_tn(a, b):
    return lax.dot_general(a.astype(BF16), b.astype(BF16), _TN, preferred_element_type=F32)


def _split3(x):
    hi = x.astype(BF16)
    r1 = x - hi.astype(F32)
    mid = r1.astype(BF16)
    lo = (r1 - mid.astype(F32)).astype(BF16)
    return hi, mid, lo


def _dot_exact(x, m, dims, x_first=True):
    if x_first:
        return sum(lax.dot_general(p, m, dims, preferred_element_type=F32) for p in _split3(x))
    return sum(lax.dot_general(m, p, dims, preferred_element_type=F32) for p in _split3(x))


_MM = (((1,), (0,)), ((), ()))


def _sigmoid_exp(x):
    return 1.0 / (1.0 + jnp.exp(-x))


def _sigmoid(x):
    return 0.5 * jnp.tanh(0.5 * x) + 0.5


def _silu(x):
    return x * _sigmoid(x)


def _layer_spec(layer, block, index, **kw):
    return pl.BlockSpec((None,) + tuple(block), lambda *g: (layer,) + tuple(index(*g)), **kw)


def _layer_full(layer, shape, **kw):
    return _layer_spec(layer, shape, lambda *g: (0,) * len(shape), **kw)


_ANY = pl.BlockSpec(memory_space=pl.ANY)


def _softplus(x):
    return jnp.maximum(x, 0.0) + jnp.log1p(jnp.exp(-jnp.abs(x)))


def _rms(x, g):
    ms = jnp.mean(x * x, axis=-1, keepdims=True)
    return x * lax.rsqrt(ms + EPS) * g


def _seq_tri(n, nb):
    r = lax.broadcasted_iota(jnp.int32, (n, n), 0)
    c = lax.broadcasted_iota(jnp.int32, (n, n), 1)
    return jnp.where(((r & (nb - 1)) == (c & (nb - 1))) & (c <= r), 1.0, 0.0).astype(BF16)


def _params(n_axes):
    return pltpu.CompilerParams(dimension_semantics=("arbitrary",) * n_axes, vmem_limit_bytes=VMEM_LIMIT)


def _proj_in_body(x_ref, g_ref, w_ref, zq_ref, zg_ref, zu_ref, zh_ref, zba_ref, *h0_ref, tt, nb):
    if h0_ref:
        h0 = h0_ref[0]

        def cp(b, c):
            h0[:, b, :] = x_ref[b]
            return c

        lax.fori_loop(0, nb, cp, 0)
        x = h0[...].reshape(tt * nb, D_MODEL)
    else:
        x = x_ref[...].reshape(tt * nb, D_MODEL)
    xn = _rms(x, g_ref[...]).astype(BF16)
    c0 = 0
    for ref, width in zip((zq_ref, zg_ref, zu_ref, zh_ref, zba_ref), _Z_WIDTHS):
        ref[...] = jnp.dot(xn, w_ref[:, c0:c0 + width], preferred_element_type=F32).reshape(tt, nb, width)
        c0 += width


def _proj_in(x, g, w, layer, bsz, t, tt, nb, from_btd):
    grid = (bsz // nb, t // tt)
    tm = lambda wd: pl.BlockSpec((tt, nb, wd), lambda j, i: (i, j, 0))
    x_spec = pl.BlockSpec((nb, tt, D_MODEL), lambda j, i: (j, i, 0)) if from_btd else tm(D_MODEL)
    out_shape = [jax.ShapeDtypeStruct((t, bsz, wd), F32) for wd in _Z_WIDTHS]
    out_specs = [tm(wd) for wd in _Z_WIDTHS]
    if from_btd:
        out_shape.append(jax.ShapeDtypeStruct((t, bsz, D_MODEL), F32))
        out_specs.append(tm(D_MODEL))
    return pl.pallas_call(
        functools.partial(_proj_in_body, tt=tt, nb=nb),
        grid=grid,
        in_specs=[x_spec, _layer_full(layer, (1, D_MODEL)), _layer_full(layer, (D_MODEL, IN_PACKED))],
        out_specs=out_specs,
        out_shape=out_shape,
        compiler_params=_params(2),
        name="proj_in",
    )(x, g, w)


def _delta_body(*refs, tt, n_tiles, group, n_alias):
    zq_ref, zg_ref, zba_ref, cw_ref, coef_ref, dnw_ref, cs_ref, s0_ref = refs[:8]
    o_ref, cso_ref, s_ref, buf, qb, kb, vb, pb, gcb, ob = refs[8 + n_alias:]
    ti = pl.program_id(1)
    nb = SUBLANES
    n = tt * nb
    hist = DN_CONV - 1
    levels = int(math.log2(tt))

    @pl.when(ti == 0)
    def _():
        s_ref[...] = s0_ref[...]
        for b in range(nb):
            buf[0:hist, b, :] = cs_ref[b]

    buf[hist:hist + tt] = zq_ref[...]
    cw = cw_ref[...]
    y = cw[0:1] * buf[0:tt].reshape(n, QKV_W)
    for j in range(1, DN_CONV):
        y = y + cw[j:j + 1] * buf[j:j + tt].reshape(n, QKV_W)

    @pl.when(ti == n_tiles - 1)
    def _():
        for b in range(nb):
            cso_ref[b] = buf[tt:tt + hist, b, :]

    buf[0:hist] = buf[tt:tt + hist]

    y = _silu(y)
    for h in range(DN_HEADS):
        qs = slice(h * DN_HD, (h + 1) * DN_HD)
        ks = slice(DN_WIDTH + h * DN_HD, DN_WIDTH + (h + 1) * DN_HD)
        qh = y[:, qs]
        kh = y[:, ks]
        qn = qh * (lax.rsqrt(jnp.sum(qh * qh, axis=-1, keepdims=True) + EPS) * DN_HD ** -0.5)
        kn = kh * lax.rsqrt(jnp.sum(kh * kh, axis=-1, keepdims=True) + EPS)
        qb[h] = qn
        kb[h] = kn
        vb[h] = y[:, 2 * DN_WIDTH + h * DN_HD:2 * DN_WIDTH + (h + 1) * DN_HD]

    zba = zba_ref[...].reshape(n, BA_W)
    coef = coef_ref[...]
    lane = lax.broadcasted_iota(jnp.int32, zba.shape, 1)
    g = coef[0:1] * _softplus(zba + coef[1:2])
    pb[...] = jnp.where(lane < DN_HEADS, _sigmoid(zba), g)
    gcb[...] = _dot_exact(g, _seq_tri(n, nb), _MM, x_first=False)

    ri = lax.broadcasted_iota(jnp.int32, (tt, tt), 0)
    ci = lax.broadcasted_iota(jnp.int32, (tt, tt), 1)
    causal = ci <= ri
    strict = ci < ri
    eye = jnp.where(ci == ri, 1.0, 0.0).astype(F32)
    sel = jnp.where(lax.broadcasted_iota(jnp.int32, (SUBLANES, BA_W), 0)
                    == lax.broadcasted_iota(jnp.int32, (SUBLANES, BA_W), 1), 1.0, 0.0).astype(BF16)

    def seq_group(bp, carry_):
        chains = []
        for s in range(group):
            b = bp * group + s
            seq_rows = pl.ds(b, tt, stride=nb)
            pbv = pb[seq_rows, :]
            gcv = gcb[seq_rows, :]
            gct = _dot_exact(gcv, sel, _NT, x_first=False)
            for h in range(DN_HEADS):
                chains.append(dict(b=b, h=h, rows=seq_rows, q=qb[h, seq_rows, :], k=kb[h, seq_rows, :],
                                   v=vb[h, seq_rows, :],
                                   beta=pbv[:, h:h + 1], gcol=gcv[:, DN_HEADS + h:DN_HEADS + h + 1],
                                   grow=gct[DN_HEADS + h:DN_HEADS + h + 1, :]))
        for c in chains:
            c['dec'] = jnp.where(causal, jnp.exp(jnp.where(causal, c['gcol'] - c['grow'], 0.0)), 0.0)
            c['qkk'] = _dot_nt(jnp.concatenate([c['q'], c['k']], axis=0), c['k'])
        for c in chains:
            c['p'] = -jnp.where(strict, c['beta'] * c['qkk'][tt:2 * tt] * c['dec'], 0.0)
            c['t'] = eye + c['p']
        for _ in range(levels - 1):
            for c in chains:
                c['p'] = _dot(c['p'], c['p'])
            for c in chains:
                c['t'] = c['t'] + _dot(c['p'], c['t'])
        for c in chains:
            eg = jnp.exp(c['gcol'])
            rhs = jnp.concatenate([c['beta'] * c['v'], (c['beta'] * eg) * c['k']], axis=1)
            c['uw'] = _dot(c['t'], rhs)
            c['qd'] = c['q'] * eg
            c['glast'] = c['grow'][:, tt - 1:tt]
            c['kd'] = c['k'] * jnp.exp(c['glast'] - c['gcol'])
            c['s'] = s_ref[c['b'], c['h']]
        for c in chains:
            c['r'] = _dot(jnp.concatenate([c['uw'][:, DN_HD:2 * DN_HD], c['qd']], axis=0), c['s'])
        for c in chains:
            c['vn'] = c['uw'][:, 0:DN_HD] - c['r'][0:tt]
        for c in chains:
            ob[c['h'], c['rows'], :] = c['r'][tt:2 * tt] + _dot(c['qkk'][0:tt] * c['dec'], c['vn'])
            s_ref[c['b'], c['h']] = c['s'] * jnp.exp(c['glast']) + _dot_tn(c['kd'], c['vn'])
        return carry_

    lax.fori_loop(0, nb // group, seq_group, 0)

    gate = zg_ref[...].reshape(n, DN_WIDTH)
    dnw = dnw_ref[...]
    for h in range(DN_HEADS):
        hs = slice(h * DN_HD, (h + 1) * DN_HD)
        o_ref[:, :, hs] = (_rms(ob[h], dnw) * _silu(gate[:, hs])).reshape(tt, nb, DN_HD)


def _delta_call(zq, zg, zba, cw, coef, dnw, cs, s0, prev, layer, bsz, t, tt, group):
    nb = SUBLANES
    n_tiles = t // tt
    hist = DN_CONV - 1
    tm = lambda wd: pl.BlockSpec((tt, nb, wd), lambda j, i: (i, j, 0))
    cs_spec = _layer_spec(layer, (nb, hist, QKV_W), lambda j, i: (j, 0, 0))
    st_spec = _layer_spec(layer, (nb, DN_HEADS, DN_HD, DN_HD), lambda j, i: (j, 0, 0, 0))
    heads = pltpu.VMEM((DN_HEADS, tt * nb, DN_HD), F32)
    rows = pltpu.VMEM((tt * nb, BA_W), F32)
    n_in = 8
    return pl.pallas_call(
        functools.partial(_delta_body, tt=tt, n_tiles=n_tiles, group=group, n_alias=len(prev)),
        grid=(bsz // nb, n_tiles),
        in_specs=[tm(QKV_W), tm(DN_WIDTH), tm(BA_W),
                  _layer_full(layer, (DN_CONV, QKV_W)), _layer_full(layer, (2, BA_W)),
                  _layer_full(layer, (1, DN_HD)), cs_spec, st_spec] + [_ANY] * len(prev),
        out_specs=[tm(DN_WIDTH), cs_spec, st_spec],
        out_shape=[jax.ShapeDtypeStruct((t, bsz, DN_WIDTH), F32),
                   jax.ShapeDtypeStruct((DEPTH, bsz, hist, QKV_W), F32),
                   jax.ShapeDtypeStruct((DEPTH, bsz, DN_HEADS, DN_HD, DN_HD), F32)],
        input_output_aliases={n_in + k: 1 + k for k in range(len(prev))},
        scratch_shapes=[pltpu.VMEM((tt + hist, nb, QKV_W), F32), heads, heads, heads, rows, rows, heads],
        compiler_params=_params(2),
        name="delta",
    )(zq, zg, zba, cw, coef, dnw, cs, s0, *prev)


def _hgrn_body(*refs, tt, n_tiles, group, n_alias):
    zh_ref, lb_ref, nw_ref, gm_ref, s0_ref = refs[:5]
    o_ref, s_ref, sbd, qsb, ksb, qeb, kdb, vb, elb, ob = refs[5 + n_alias:]
    ti = pl.program_id(1)
    nb = SUBLANES
    n = tt * nb
    mid = tt // 2 - 1
    sh_h = int(math.log2(HG_HD))

    @pl.when(ti == 0)
    def _():
        zero = jnp.zeros((HG_HD, HG_HD), F32)

        def init(b, c):
            for h in range(HG_HEADS):
                parts = [zero] * HG_HEADS
                parts[h] = s0_ref[b, h].T
                sbd[b, h * HG_HD:(h + 1) * HG_HD, :] = jnp.concatenate(parts, axis=1)
            return c
        lax.fori_loop(0, nb, init, 0)

    zh = zh_ref[...].reshape(n, ZH_W)
    lb = lb_ref[...]
    f = lb + (1.0 - lb) * _sigmoid_exp(zh[:, HG_WIDTH:2 * HG_WIDTH])
    q = _silu(zh[:, 0:HG_WIDTH]).reshape(tt, nb, HG_WIDTH)
    k = (1.0 - f).reshape(tt, nb, HG_WIDTH)
    bc = _dot_exact(jnp.log(f), _seq_tri(n, nb), _MM, x_first=False).reshape(tt, nb, HG_WIDTH)
    bm = bc[mid]
    bl = bc[tt - 1]
    half = HG_WIDTH // 2
    for dst, val in ((qsb, q * jnp.exp(bc - bm[None])), (ksb, k * jnp.exp(bm[None] - bc)),
                     (qeb, q * jnp.exp(bc)), (kdb, k * jnp.exp(bl[None] - bc))):
        val = val.reshape(n, HG_WIDTH)
        dst[0] = val[:, 0:half]
        dst[1] = val[:, half:HG_WIDTH]
    vb[0] = zh[:, 2 * HG_WIDTH:2 * HG_WIDTH + half]
    vb[1] = zh[:, 2 * HG_WIDTH + half:3 * HG_WIDTH]
    elb[...] = jnp.exp(bl)

    ri = lax.broadcasted_iota(jnp.int32, (HG_HEADS * tt, tt), 0)
    ci = lax.broadcasted_iota(jnp.int32, (HG_HEADS * tt, tt), 1)
    causal = ci <= (ri & (tt - 1))
    lane_head = lax.broadcasted_iota(jnp.int32, (tt, HG_WIDTH), 1) >> sh_h
    hmask = [jnp.where(lane_head == h, 1.0, 0.0).astype(F32) for h in range(HG_HEADS)]
    bd = ((lax.broadcasted_iota(jnp.int32, (HG_WIDTH, HG_WIDTH), 0) >> sh_h)
          == (lax.broadcasted_iota(jnp.int32, (HG_WIDTH, HG_WIDTH), 1) >> sh_h))

    def seq_group(bp, carry_):
        seqs = []
        for s in range(group):
            b = bp * group + s
            seq_rows = pl.ds(b, tt, stride=nb)
            seq = lambda ref: jnp.concatenate([ref[0, seq_rows, :], ref[1, seq_rows, :]], axis=1)
            seqs.append(dict(b=b, rows=seq_rows, qs=seq(qsb), ks=seq(ksb), qe=seq(qeb), kd=seq(kdb),
                             v=seq(vb), st=sbd[b], el=elb[pl.ds(b, 1), :]))
        for c in seqs:
            c['a'] = jnp.where(causal, _dot_nt(jnp.concatenate([c['qs'] * m for m in hmask], axis=0), c['ks']), 0.0)
            c['o'] = _dot_nt(c['qe'], c['st'])
            c['kv'] = _dot_tn(c['v'], c['kd'])
        for c in seqs:
            o = c['o']
            for h in range(HG_HEADS):
                o = o + hmask[h] * _dot(c['a'][h * tt:(h + 1) * tt], c['v'])
            ob[0, c['rows'], :] = o[:, 0:half]
            ob[1, c['rows'], :] = o[:, half:HG_WIDTH]
            sbd[c['b']] = c['st'] * c['el'] + jnp.where(bd, c['kv'], 0.0)
        return carry_

    lax.fori_loop(0, nb // group, seq_group, 0)

    o = jnp.concatenate([ob[0], ob[1]], axis=1)
    ms = _dot_exact(o * o, gm_ref[...], _MM) * (1.0 / HG_HD)
    o_ref[...] = (o * lax.rsqrt(ms + EPS) * nw_ref[...]
                  * _silu(zh[:, 3 * HG_WIDTH:4 * HG_WIDTH])).reshape(tt, nb, HG_WIDTH)

    @pl.when(ti == n_tiles - 1)
    def _():
        def fin(b, c):
            for h in range(HG_HEADS):
                blk = sbd[b, h * HG_HD:(h + 1) * HG_HD, :]
                s_ref[b, h] = blk[:, h * HG_HD:(h + 1) * HG_HD].T
            return c
        lax.fori_loop(0, nb, fin, 0)


def _hgrn_call(zh, lb, nw, gm, s0, prev, layer, bsz, t, tt, group):
    nb = SUBLANES
    n_tiles = t // tt
    st_spec = _layer_spec(layer, (nb, HG_HEADS, HG_HD, HG_HD), lambda j, i: (j, 0, 0, 0))
    tile = pltpu.VMEM((HG_WIDTH // LANES, tt * nb, LANES), F32)
    n_in = 5
    return pl.pallas_call(
        functools.partial(_hgrn_body, tt=tt, n_tiles=n_tiles, group=group, n_alias=len(prev)),
        grid=(bsz // nb, n_tiles),
        in_specs=[pl.BlockSpec((tt, nb, ZH_W), lambda j, i: (i, j, 0)),
                  _layer_full(layer, (1, HG_WIDTH)), _layer_full(layer, (1, HG_WIDTH)),
                  pl.BlockSpec((HG_WIDTH, HG_WIDTH), lambda j, i: (0, 0)), st_spec] + [_ANY] * len(prev),
        out_specs=[pl.BlockSpec((tt, nb, HG_WIDTH), lambda j, i: (i, j, 0)), st_spec],
        out_shape=[jax.ShapeDtypeStruct((t, bsz, HG_WIDTH), F32),
                   jax.ShapeDtypeStruct((DEPTH, bsz, HG_HEADS, HG_HD, HG_HD), F32)],
        input_output_aliases={n_in + k: 1 + k for k in range(len(prev))},
        scratch_shapes=[pltpu.VMEM((nb, HG_WIDTH, HG_WIDTH), F32),
                        tile, tile, tile, tile, tile, pltpu.VMEM((nb, HG_WIDTH), F32), tile],
        compiler_params=_params(2),
        name="hgrn",
    )(zh, lb, nw, gm, s0, *prev)


def _s5_body(u_ref, bh_ref, lam_ref, c_ref, d_ref, gw_ref, gb_ref, x0r_ref, x0i_ref,
             o_ref, xr_ref, xi_ref, xs, *, bsz, steps):
    i = pl.program_id(0)

    @pl.when(i == 0)
    def _():
        xr_ref[...] = x0r_ref[...]
        xi_ref[...] = x0i_ref[...]

    u = u_ref[...].reshape(steps * bsz, SSM_WIDTH)
    u_hi = u.astype(BF16)
    u_lo = (u - u_hi.astype(F32)).astype(BF16)
    bh = bh_ref[...]
    xs[...] = jnp.dot(u_hi, bh, preferred_element_type=F32) + jnp.dot(u_lo, bh, preferred_element_type=F32)

    lr = jnp.broadcast_to(lam_ref[0:1, :], (SUBLANES, SSM_FLAT))
    li = jnp.broadcast_to(lam_ref[1:2, :], (SUBLANES, SSM_FLAT))

    def row_block(rb, carry_):
        r0 = pl.multiple_of(rb * SUBLANES, SUBLANES)
        xr0 = xr_ref[pl.ds(r0, SUBLANES), :]
        xi0 = xi_ref[pl.ds(r0, SUBLANES), :]

        def tstep(t, st):
            xr, xi = st
            row = pl.ds(pl.multiple_of(t * bsz + r0, SUBLANES), SUBLANES)
            nr = lr * xr - li * xi + xs[row, 0:SSM_FLAT]
            ni = lr * xi + li * xr + xs[row, SSM_FLAT:2 * SSM_FLAT]
            xs[row, 0:SSM_FLAT] = nr
            xs[row, SSM_FLAT:2 * SSM_FLAT] = ni
            return nr, ni

        xr1, xi1 = lax.fori_loop(0, steps, tstep, (xr0, xi0))
        xr_ref[pl.ds(r0, SUBLANES), :] = xr1
        xi_ref[pl.ds(r0, SUBLANES), :] = xi1
        return carry_

    lax.fori_loop(0, bsz // SUBLANES, row_block, 0)

    y = _dot(xs[...], c_ref[...]) + d_ref[...] * u
    y = 0.5 * y * (1.0 + jnp.tanh(math.sqrt(2.0 / math.pi) * (y + 0.044715 * (y * y * y))))
    o_ref[...] = (y * _sigmoid(_dot(y, gw_ref[...]) + gb_ref[...])).reshape(steps, bsz, SSM_WIDTH)


def _s5_call(u, bh, lam, cm, d, gw, gb, x0r, x0i, layer, bsz, t, steps):
    rows = steps * bsz
    full = lambda shp: pl.BlockSpec(shp, lambda i: tuple(0 for _ in shp))
    lfull = lambda shp: _layer_full(layer, shp)
    return pl.pallas_call(
        functools.partial(_s5_body, bsz=bsz, steps=steps),
        grid=(t // steps,),
        in_specs=[pl.BlockSpec((steps, bsz, SSM_WIDTH), lambda i: (i, 0, 0)),
                  lfull((SSM_WIDTH, 2 * SSM_FLAT)),
                  lfull((2, SSM_FLAT)), lfull((2 * SSM_FLAT, SSM_WIDTH)), lfull((1, SSM_WIDTH)),
                  lfull((SSM_WIDTH, SSM_WIDTH)), lfull((1, SSM_WIDTH)),
                  full((bsz, SSM_FLAT)), full((bsz, SSM_FLAT))],
        out_specs=[pl.BlockSpec((steps, bsz, SSM_WIDTH), lambda i: (i, 0, 0)),
                   full((bsz, SSM_FLAT)), full((bsz, SSM_FLAT))],
        out_shape=[jax.ShapeDtypeStruct((t, bsz, SSM_WIDTH), F32),
                   jax.ShapeDtypeStruct((bsz, SSM_FLAT), F32),
                   jax.ShapeDtypeStruct((bsz, SSM_FLAT), F32)],
        scratch_shapes=[pltpu.VMEM((rows, 2 * SSM_FLAT), F32)],
        compiler_params=_params(1),
        name="s5",
    )(u, bh, lam, cm, d, gw, gb, x0r, x0i)


def _tail_body(h_ref, oa_ref, ob_ref, oc_ref, wo_ref, nf_ref, wua_ref, wub_ref, cwa_ref, cwb_ref,
               wd_ref, csa_ref, csb_ref, p_ref, npl_ref, wg_ref, wp_ref,
               out_ref, cso_a_ref, cso_b_ref,
               hn, p3, car_a, car_b, *, tt, nb, n_ffc, ffc, n_sub):
    i = pl.program_id(1)
    j = pl.program_id(2)
    n = tt * nb
    hist = FF_CONV - 1
    st = tt // n_sub
    sr = st * nb
    hr = hist * nb

    @pl.when(j == 0)
    def _():
        h2 = (h_ref[...].reshape(n, D_MODEL) + _dot(oa_ref[...].reshape(n, DN_WIDTH), wo_ref[0:DN_WIDTH, :])
              + _dot(ob_ref[...].reshape(n, SSM_WIDTH), wo_ref[DN_WIDTH:DN_WIDTH + SSM_WIDTH, :])
              + _dot(oc_ref[...].reshape(n, HG_WIDTH), wo_ref[DN_WIDTH + SSM_WIDTH:D_MODEL, :]))
        out_ref[...] = h2.reshape(tt, nb, D_MODEL)
        hn[...] = _rms(h2, nf_ref[...]).astype(BF16)

    @pl.when(i == 0)
    def _():
        def cp(b, c):
            car_a[j, :, b, :] = csa_ref[b]
            car_b[j, :, b, :] = csb_ref[b]
            return c
        lax.fori_loop(0, nb, cp, 0)

    prev = [car_a[j].reshape(hr, ffc), car_b[j].reshape(hr, ffc)]
    wd = wd_ref[...]
    for sb in range(n_sub):
        hs = hn[sb * sr:(sb + 1) * sr, :]
        halves = []
        for idx, (wu, cw) in enumerate(((wua_ref, cwa_ref), (wub_ref, cwb_ref))):
            x = jnp.concatenate([prev[idx], jnp.dot(hs, wu[...], preferred_element_type=F32)], axis=0)
            w = cw[...]
            acc = w[0:1] * x[0:sr]
            for s in range(1, FF_CONV):
                acc = acc + w[s:s + 1] * x[s * nb:s * nb + sr]
            halves.append(acc)
            prev[idx] = x[sr:sr + hr]
        out_ref[sb * st:(sb + 1) * st] += _dot(_silu(halves[0]) * halves[1], wd).reshape(st, nb, D_MODEL)
    for car, cso, last in ((car_a, cso_a_ref, prev[0]), (car_b, cso_b_ref, prev[1])):
        last = last.reshape(hist, nb, ffc)
        car[j] = last
        cso[...] = last

    @pl.when(j == n_ffc - 1)
    def _():
        def cp(b, c):
            p3[:, b, :] = p_ref[b]
            return c
        lax.fori_loop(0, nb, cp, 0)
        h3 = out_ref[...].reshape(n, D_MODEL)
        gate = _sigmoid(_dot(_rms(h3, npl_ref[...]), wg_ref[...]))
        out_ref[...] = (h3 + gate * _dot(p3[...].reshape(n, PLE_DIM), wp_ref[...])).reshape(tt, nb, D_MODEL)


def _tail_call(h, oa, ob, oc, wo, nf, wu, cw, wd, cs, p, npl, wg, wp, layer, bsz, t, tt, nb, ffc, n_sub):
    n_ffc = FF_DIM // ffc
    n_t = t // tt
    hist = FF_CONV - 1
    tm = lambda wd_: pl.BlockSpec((tt, nb, wd_), lambda jb, i, j: (i, jb, 0))
    full = lambda shp: _layer_full(layer, shp, pipeline_mode=pl.Buffered(1))
    lyr = lambda blk, idx: _layer_spec(layer, blk, idx)
    cs_a = lyr((nb, hist, ffc), lambda jb, i, j: (jb, 0, j))
    cs_b = lyr((nb, hist, ffc), lambda jb, i, j: (jb, 0, n_ffc + j))
    cso = pl.BlockSpec((None, hist, nb, ffc), lambda jb, i, j: (i, 0, jb, j))
    scratch = [pltpu.VMEM((tt * nb, D_MODEL), BF16),
               pltpu.VMEM((tt, nb, PLE_DIM), F32),
               pltpu.VMEM((n_ffc, hist, nb, ffc), F32),
               pltpu.VMEM((n_ffc, hist, nb, ffc), F32)]
    return pl.pallas_call(
        functools.partial(_tail_body, tt=tt, nb=nb, n_ffc=n_ffc, ffc=ffc, n_sub=n_sub),
        grid=(bsz // nb, n_t, n_ffc),
        in_specs=[tm(D_MODEL), tm(DN_WIDTH), tm(SSM_WIDTH), tm(HG_WIDTH),
                  full((D_MODEL, D_MODEL)), full((1, D_MODEL)),
                  lyr((D_MODEL, ffc), lambda jb, i, j: (0, j)),
                  lyr((D_MODEL, ffc), lambda jb, i, j: (0, n_ffc + j)),
                  lyr((FF_CONV, ffc), lambda jb, i, j: (0, j)),
                  lyr((FF_CONV, ffc), lambda jb, i, j: (0, n_ffc + j)),
                  lyr((ffc, D_MODEL), lambda jb, i, j: (j, 0)),
                  cs_a, cs_b,
                  lyr((nb, tt, PLE_DIM), lambda jb, i, j: (jb, i, 0)),
                  full((1, D_MODEL)), full((D_MODEL, D_MODEL)), full((PLE_DIM, D_MODEL))],
        out_specs=[tm(D_MODEL), cso, cso],
        out_shape=[jax.ShapeDtypeStruct((t, bsz, D_MODEL), F32),
                   jax.ShapeDtypeStruct((n_t, hist, bsz, FF_DIM), F32),
                   jax.ShapeDtypeStruct((n_t, hist, bsz, FF_DIM), F32)],
        scratch_shapes=scratch,
        compiler_params=_params(3),
        name="tail",
    )(h, oa, ob, oc, wo, nf, wu, wu, cw, cw, wd, cs, cs, p, npl, wg, wp)


def _final_body(h_ref, g_ref, o_ref, *, nb):
    g = g_ref[...]

    def one(b, c):
        o_ref[b] = _rms(h_ref[:, b, :], g)
        return c

    lax.fori_loop(0, nb, one, 0)


def _final_call(h, g, bsz, t, tt, nb):
    return pl.pallas_call(
        functools.partial(_final_body, nb=nb),
        grid=(bsz // nb, t // tt),
        in_specs=[pl.BlockSpec((tt, nb, D_MODEL), lambda j, i: (i, j, 0)),
                  pl.BlockSpec((1, D_MODEL), lambda j, i: (0, 0))],
        out_specs=pl.BlockSpec((nb, tt, D_MODEL), lambda j, i: (j, i, 0)),
        out_shape=jax.ShapeDtypeStruct((bsz, t, D_MODEL), F32),
        compiler_params=_params(2),
        name="final",
    )(h, g)


def _pack(prm):
    (norm_mix, w_in, dn_conv_w, dn_a_log, dn_dt_bias, dn_norm, ssm_lam_re, ssm_lam_im, ssm_log_step,
     ssm_b_re, ssm_b_im, ssm_c_re, ssm_c_im, ssm_d, ssm_glu_w, ssm_glu_b, lower_bounds, hg_norm, w_out,
     norm_ffn, ffn_w_up, ffn_conv_w, ffn_w_down, norm_ple, ple_w_gate, ple_w_proj) = prm
    w = w_in.astype(BF16)
    o_gate = QKV_W
    o_beta = o_gate + DN_WIDTH
    o_a = o_beta + DN_HEADS
    o_u = o_a + DN_HEADS
    o_h = o_u + SSM_WIDTH
    w_packed = jnp.concatenate(
        [w[..., 0:o_beta], w[..., o_u:o_h + ZH_W], w[..., o_beta:o_u],
         jnp.zeros((DEPTH, D_MODEL, BA_W - 2 * DN_HEADS), BF16)], axis=-1)

    zeros4 = jnp.zeros((DEPTH, DN_HEADS), F32)
    pad = jnp.zeros((DEPTH, BA_W - 2 * DN_HEADS), F32)
    coef = jnp.stack([jnp.concatenate([zeros4, -jnp.exp(dn_a_log.astype(F32)), pad], axis=-1),
                      jnp.concatenate([zeros4, dn_dt_bias.astype(F32), pad], axis=-1)], axis=1)

    lre = ssm_lam_re.astype(F32)
    lim = ssm_lam_im.astype(F32)
    delta = jnp.exp(ssm_log_step.astype(F32))[..., None]
    mag = jnp.exp(lre * delta)
    lbr = mag * jnp.cos(lim * delta)
    lbi = mag * jnp.sin(lim * delta)
    den = lre * lre + lim * lim
    fr = ((lbr - 1.0) * lre + lbi * lim) / den
    fi = (lbi * lre - (lbr - 1.0) * lim) / den
    bre = ssm_b_re.astype(F32)
    bim = ssm_b_im.astype(F32)
    bbr = fr[..., None] * bre - fi[..., None] * bim
    bbi = fr[..., None] * bim + fi[..., None] * bre
    eye_g = jnp.eye(SSM_GROUPS, dtype=F32)

    def bdiag_in(m):
        return jnp.einsum('dgph,gk->dghkp', m, eye_g).reshape(DEPTH, SSM_WIDTH, SSM_FLAT)

    def bdiag_out(m):
        return jnp.einsum('dghp,gk->dgpkh', m, eye_g).reshape(DEPTH, SSM_FLAT, SSM_WIDTH)

    bmat = jnp.concatenate([bdiag_in(bbr), bdiag_in(bbi)], axis=-1).astype(BF16)
    cmat = jnp.concatenate([bdiag_out(ssm_c_re.astype(F32)), -bdiag_out(ssm_c_im.astype(F32))],
                           axis=1).astype(BF16)
    lam = jnp.stack([lbr.reshape(DEPTH, SSM_FLAT), lbi.reshape(DEPTH, SSM_FLAT)], axis=1)

    gmat = jnp.kron(jnp.eye(HG_HEADS, dtype=F32), jnp.ones((HG_HD, HG_HD), F32)).astype(BF16)
    row = lambda a: a.reshape(DEPTH, 1, a.shape[-1])
    return dict(
        norm_mix=row(norm_mix), w_in=w_packed, conv_w=dn_conv_w, coef=coef, dn_norm=row(dn_norm),
        bmat=bmat, lam=lam, cmat=cmat, ssm_d=row(ssm_d), glu_w=ssm_glu_w.astype(BF16), glu_b=row(ssm_glu_b),
        lb=row(lower_bounds), hg_norm=row(jnp.tile(hg_norm, (1, HG_HEADS))), gmat=gmat,
        w_out=w_out.astype(BF16), norm_ffn=row(norm_ffn), w_up=ffn_w_up.astype(BF16), ffn_conv_w=ffn_conv_w,
        w_down=ffn_w_down.astype(BF16), norm_ple=row(norm_ple), ple_gate=ple_w_gate.astype(BF16),
        ple_proj=ple_w_proj.astype(BF16))


def _tiles(bsz, t):
    if t >= DN_CHUNK:
        return dict(tok_tt=512 // bsz, tok_nb=bsz, dn_tt=DN_CHUNK, hg_tt=HG_CHUNK, s5_steps=512 // bsz,
                    ffc=1408, n_sub=4, fin_tt=128, group=4)
    return dict(tok_tt=t, tok_nb=512 // t, dn_tt=t, hg_tt=t, s5_steps=t, ffc=1408, n_sub=4, fin_tt=t, group=4)


def _trunk(x, p, conv_qkv, delta, ssm_re, ssm_im, hgrn, conv_ffn, lp, norm_final):
    bsz, t, _ = x.shape
    tl = _tiles(bsz, t)
    sr_l, si_l, cf_l = [], [], []
    dn_prev, hg_prev = (), ()
    h = x
    for i in range(DEPTH):
        res = _proj_in(h, lp['norm_mix'], lp['w_in'], i, bsz, t, tl['tok_tt'], tl['tok_nb'], from_btd=(i == 0))
        if i == 0:
            h = res[5]
        zq, zg, zu, zh, zba = res[:5]
        o_a, cq, dl = _delta_call(zq, zg, zba, lp['conv_w'], lp['coef'], lp['dn_norm'], conv_qkv, delta,
                                  dn_prev, i, bsz, t, tl['dn_tt'], tl['group'])
        dn_prev = (cq, dl)
        o_b, sr, si = _s5_call(zu, lp['bmat'], lp['lam'], lp['cmat'], lp['ssm_d'], lp['glu_w'], lp['glu_b'],
                               ssm_re[i].reshape(bsz, SSM_FLAT), ssm_im[i].reshape(bsz, SSM_FLAT),
                               i, bsz, t, tl['s5_steps'])
        o_c, hg = _hgrn_call(zh, lp['lb'], lp['hg_norm'], lp['gmat'], hgrn, hg_prev, i, bsz, t, tl['hg_tt'],
                             tl['group'])
        hg_prev = (hg,)
        h, cfa, cfb = _tail_call(h, o_a, o_b, o_c, lp['w_out'], lp['norm_ffn'], lp['w_up'], lp['ffn_conv_w'],
                                 lp['w_down'], conv_ffn, p, lp['norm_ple'], lp['ple_gate'],
                                 lp['ple_proj'], i, bsz, t, tl['tok_tt'], tl['tok_nb'], tl['ffc'], tl['n_sub'])
        sr_l.append(sr.reshape(bsz, SSM_GROUPS, SSM_STATE))
        si_l.append(si.reshape(bsz, SSM_GROUPS, SSM_STATE))
        cf_l.append(jnp.swapaxes(jnp.concatenate([cfa[-1], cfb[-1]], axis=-1), 0, 1))
    y = _final_call(h, norm_final, bsz, t, tl['fin_tt'], tl['tok_nb'])
    return (y, cq, dl, jnp.stack(sr_l), jnp.stack(si_l), hg, jnp.stack(cf_l))


def kernel(x_prompt, x_sample, p_prompt, p_sample, state_conv_qkv, state_delta, state_ssm_re, state_ssm_im, state_hgrn, state_conv_ffn, norm_mix, w_in, dn_conv_w, dn_a_log, dn_dt_bias, dn_norm, ssm_lam_re, ssm_lam_im, ssm_log_step, ssm_b_re, ssm_b_im, ssm_c_re, ssm_c_im, ssm_d, ssm_glu_w, ssm_glu_b, hg_lower, hg_norm, w_out, norm_ffn, ffn_w_up, ffn_conv_w, ffn_w_down, norm_ple, ple_w_gate, ple_w_proj, norm_final):
    lb_p = jax.nn.softmax(hg_lower.astype(F32), axis=0)
    lower_bounds = jnp.cumsum(lb_p, axis=0) - lb_p[0]
    prm = (norm_mix, w_in, dn_conv_w, dn_a_log, dn_dt_bias, dn_norm, ssm_lam_re, ssm_lam_im, ssm_log_step,
           ssm_b_re, ssm_b_im, ssm_c_re, ssm_c_im, ssm_d, ssm_glu_w, ssm_glu_b, lower_bounds, hg_norm, w_out,
           norm_ffn, ffn_w_up, ffn_conv_w, ffn_w_down, norm_ple, ple_w_gate, ple_w_proj)
    layers = _pack(prm)
    nf = norm_final.reshape(1, D_MODEL)

    bp = x_prompt.shape[0]
    z = lambda *shp: jnp.zeros((DEPTH, bp) + shp, F32)
    prompt = _trunk(x_prompt, p_prompt, z(DN_CONV - 1, QKV_W), z(DN_HEADS, DN_HD, DN_HD),
                    z(SSM_GROUPS, SSM_STATE), z(SSM_GROUPS, SSM_STATE), z(HG_HEADS, HG_HD, HG_HD),
                    z(FF_CONV - 1, 2 * FF_DIM), layers, nf)
    sample = _trunk(x_sample, p_sample, state_conv_qkv, state_delta, state_ssm_re, state_ssm_im, state_hgrn,
                    state_conv_ffn, layers, nf)
    return (prompt[0], sample[0]) + prompt[1:] + sample[1:]
```

```python
import functools
import math

import jax
import jax.numpy as jnp
from jax import lax
from jax.experimental import pallas as pl
from jax.experimental.pallas import tpu as pltpu

F32 = jnp.float32
BF16 = jnp.bfloat16

D_MODEL = 1024
DEPTH = 2
DN_HEADS = 4
DN_WIDTH = 512
DN_HD = 128
DN_CONV = 4
DN_CHUNK = 64
SSM_WIDTH = 256
SSM_GROUP = 16
SSM_GROUPS = 16
SSM_STATE = 64
SSM_FLAT = SSM_GROUPS * SSM_STATE
HG_WIDTH = 256
HG_HEADS = 4
HG_HD = 64
HG_CHUNK = 32
FF_DIM = 2816
FF_CONV = 3
PLE_DIM = 256
EPS = 1e-6

SUBLANES = 8
LANES = 128

QKV_W = 3 * DN_WIDTH
ZH_W = 4 * HG_WIDTH
BA_W = LANES
IN_PACKED = QKV_W + DN_WIDTH + SSM_WIDTH + ZH_W + BA_W
_Z_WIDTHS = (QKV_W, DN_WIDTH, SSM_WIDTH, ZH_W, BA_W)

VMEM_LIMIT = 56 * 1024 * 1024

_NT = (((1,), (1,)), ((), ()))
_TN = (((0,), (0,)), ((), ()))


def _dot(a, b):
    return jnp.dot(a.astype(BF16), b.astype(BF16), preferred_element_type=F32)


def _dot_nt(a, b):
    return lax.dot_general(a.astype(BF16), b.astype(BF16), _NT, preferred_element_type=F32)


def _dot_tn(a, b):
    return lax.dot_general(a.astype(BF16), b.astype(BF16), _TN, preferred_element_type=F32)


def _split3(x):
    hi = x.astype(BF16)
    r1 = x - hi.astype(F32)
    mid = r1.astype(BF16)
    lo = (r1 - mid.astype(F32)).astype(BF16)
    return hi, mid, lo


def _dot_exact(x, m, dims, x_first=True):
    if x_first:
        return sum(lax.dot_general(p, m, dims, preferred_element_type=F32) for p in _split3(x))
    return sum(lax.dot_general(m, p, dims, preferred_element_type=F32) for p in _split3(x))


_MM = (((1,), (0,)), ((), ()))


def _sigmoid_exp(x):
    return 1.0 / (1.0 + jnp.exp(-x))


def _sigmoid(x):
    return 0.5 * jnp.tanh(0.5 * x) + 0.5


def _silu(x):
    h = 0.5 * x
    return h + h * jnp.tanh(h)


def _layer_spec(layer, block, index, **kw):
    return pl.BlockSpec((None,) + tuple(block), lambda *g: (layer,) + tuple(index(*g)), **kw)


def _layer_full(layer, shape, **kw):
    return _layer_spec(layer, shape, lambda *g: (0,) * len(shape), **kw)


_ANY = pl.BlockSpec(memory_space=pl.ANY)


def _softplus(x):
    return jnp.maximum(x, 0.0) + jnp.log1p(jnp.exp(-jnp.abs(x)))


def _rms(x, g):
    ms = jnp.mean(x * x, axis=-1, keepdims=True)
    return x * lax.rsqrt(ms + EPS) * g


def _seq_tri(n, nb):
    r = lax.broadcasted_iota(jnp.int32, (n, n), 0)
    c = lax.broadcasted_iota(jnp.int32, (n, n), 1)
    return jnp.where(((r & (nb - 1)) == (c & (nb - 1))) & (c <= r), 1.0, 0.0).astype(BF16)


def _params(n_axes):
    return pltpu.CompilerParams(dimension_semantics=("arbitrary",) * n_axes, vmem_limit_bytes=VMEM_LIMIT)


def _proj_in_body(*refs, tt, nb, n_tiles, from_btd, n_alias):
    x_ref, g_ref, w_ref, cw_ref, cs_ref = refs[:5]
    outs = refs[5 + n_alias:]
    qkv_ref, zg_ref, zu_ref, zh_ref, zba_ref, cso_ref = outs[:6]
    buf = outs[-1]
    ti = pl.program_id(1)
    n = tt * nb
    hist = DN_CONV - 1
    if from_btd:
        h0 = outs[6]

        def cp(b, c):
            h0[:, b, :] = x_ref[b]
            return c

        lax.fori_loop(0, nb, cp, 0)
        x = h0[...].reshape(n, D_MODEL)
    else:
        x = x_ref[...].reshape(n, D_MODEL)
    xn = _rms(x, g_ref[...]).astype(BF16)

    @pl.when(ti == 0)
    def _():
        def cp(b, c):
            buf[:, b, :] = cs_ref[b]
            return c
        lax.fori_loop(0, nb, cp, 0)

    n_sub = 4
    st = tt // n_sub
    sr = st * nb
    hr = hist * nb
    cw = cw_ref[...]
    prev = buf[...].reshape(hr, QKV_W)
    for sb in range(n_sub):
        xs = xn[sb * sr:(sb + 1) * sr]
        ts = slice(sb * st, (sb + 1) * st)
        xcat = jnp.concatenate([prev, jnp.dot(xs, w_ref[:, 0:QKV_W], preferred_element_type=F32)], axis=0)
        c0 = QKV_W
        for ref, width in zip((zg_ref, zu_ref, zh_ref, zba_ref), _Z_WIDTHS[1:]):
            ref[ts] = jnp.dot(xs, w_ref[:, c0:c0 + width], preferred_element_type=F32).reshape(st, nb, width)
            c0 += width
        y = cw[0:1] * xcat[0:sr]
        for j in range(1, DN_CONV):
            y = y + cw[j:j + 1] * xcat[j * nb:j * nb + sr]
        prev = xcat[sr:sr + hr]
        y = _silu(y)
        for h in range(DN_HEADS):
            qs = slice(h * DN_HD, (h + 1) * DN_HD)
            ks = slice(DN_WIDTH + h * DN_HD, DN_WIDTH + (h + 1) * DN_HD)
            qh = y[:, qs]
            kh = y[:, ks]
            qn = qh * (lax.rsqrt(jnp.sum(qh * qh, axis=-1, keepdims=True) + EPS) * DN_HD ** -0.5)
            kn = kh * lax.rsqrt(jnp.sum(kh * kh, axis=-1, keepdims=True) + EPS)
            qkv_ref[ts, :, qs] = qn.reshape(st, nb, DN_HD)
            qkv_ref[ts, :, ks] = kn.reshape(st, nb, DN_HD)
        qkv_ref[ts, :, 2 * DN_WIDTH:QKV_W] = y[:, 2 * DN_WIDTH:QKV_W].reshape(st, nb, DN_WIDTH)
    buf[...] = prev.reshape(hist, nb, QKV_W)

    @pl.when(ti == n_tiles - 1)
    def _():
        def cp(b, c):
            cso_ref[b] = buf[:, b, :]
            return c
        lax.fori_loop(0, nb, cp, 0)


def _proj_in(x, g, w, cw, cs, prev, layer, bsz, t, tt, nb, from_btd):
    n_tiles = t // tt
    hist = DN_CONV - 1
    tm = lambda wd: pl.BlockSpec((tt, nb, wd), lambda j, i: (i, j, 0))
    x_spec = pl.BlockSpec((nb, tt, D_MODEL), lambda j, i: (j, i, 0)) if from_btd else tm(D_MODEL)
    cs_spec = _layer_spec(layer, (nb, hist, QKV_W), lambda j, i: (j, 0, 0))
    out_shape = [jax.ShapeDtypeStruct((t, bsz, wd), F32) for wd in _Z_WIDTHS]
    out_specs = [tm(wd) for wd in _Z_WIDTHS]
    out_shape.append(jax.ShapeDtypeStruct((DEPTH, bsz, hist, QKV_W), F32))
    out_specs.append(cs_spec)
    if from_btd:
        out_shape.append(jax.ShapeDtypeStruct((t, bsz, D_MODEL), F32))
        out_specs.append(tm(D_MODEL))
    n_in = 5
    return pl.pallas_call(
        functools.partial(_proj_in_body, tt=tt, nb=nb, n_tiles=n_tiles, from_btd=from_btd, n_alias=len(prev)),
        grid=(bsz // nb, n_tiles),
        in_specs=[x_spec, _layer_full(layer, (1, D_MODEL)), _layer_full(layer, (D_MODEL, IN_PACKED)),
                  _layer_full(layer, (DN_CONV, QKV_W)), cs_spec] + [_ANY] * len(prev),
        out_specs=out_specs,
        out_shape=out_shape,
        input_output_aliases={n_in + k: 5 + k for k in range(len(prev))},
        scratch_shapes=[pltpu.VMEM((hist, nb, QKV_W), F32)],
        compiler_params=_params(2),
        name="proj_in",
    )(x, g, w, cw, cs, *prev)


def _delta_body(*refs, tt, group, n_alias):
    qkv_ref, zg_ref, zba_ref, coef_ref, dnw_ref, s0_ref = refs[:6]
    o_ref, s_ref, qb, kb, vb, pb, gcb, ob = refs[6 + n_alias:]
    ti = pl.program_id(1)
    nb = SUBLANES
    n = tt * nb
    levels = int(math.log2(tt))

    @pl.when(ti == 0)
    def _():
        s_ref[...] = s0_ref[...]

    for h in range(DN_HEADS):
        for dst, off in ((qb, 0), (kb, DN_WIDTH), (vb, 2 * DN_WIDTH)):
            dst[h] = qkv_ref[:, :, off + h * DN_HD:off + (h + 1) * DN_HD].reshape(n, DN_HD)

    zba = zba_ref[...].reshape(n, BA_W)
    coef = coef_ref[...]
    lane = lax.broadcasted_iota(jnp.int32, zba.shape, 1)
    g = coef[0:1] * _softplus(zba + coef[1:2])
    pb[...] = jnp.where(lane < DN_HEADS, _sigmoid(zba), g)
    gcb[...] = _dot_exact(g, _seq_tri(n, nb), _MM, x_first=False)

    ri = lax.broadcasted_iota(jnp.int32, (tt, tt), 0)
    ci = lax.broadcasted_iota(jnp.int32, (tt, tt), 1)
    causal = ci <= ri
    strict = ci < ri
    eye = jnp.where(ci == ri, 1.0, 0.0).astype(F32)
    sel = jnp.where(lax.broadcasted_iota(jnp.int32, (SUBLANES, BA_W), 0)
                    == lax.broadcasted_iota(jnp.int32, (SUBLANES, BA_W), 1), 1.0, 0.0).astype(BF16)

    def seq_group(bp, carry_):
        chains = []
        for s in range(group):
            b = bp * group + s
            seq_rows = pl.ds(b, tt, stride=nb)
            pbv = pb[seq_rows, :]
            gcv = gcb[seq_rows, :]
            gct = _dot_exact(gcv, sel, _NT, x_first=False)
            for h in range(DN_HEADS):
                chains.append(dict(b=b, h=h, rows=seq_rows, q=qb[h, seq_rows, :], k=kb[h, seq_rows, :],
                                   v=vb[h, seq_rows, :],
                                   beta=pbv[:, h:h + 1], gcol=gcv[:, DN_HEADS + h:DN_HEADS + h + 1],
                                   grow=gct[DN_HEADS + h:DN_HEADS + h + 1, :]))
        for c in chains:
            c['dec'] = jnp.where(causal, jnp.exp(jnp.where(causal, c['gcol'] - c['grow'], 0.0)), 0.0)
            c['qkk'] = _dot_nt(jnp.concatenate([c['q'], c['k']], axis=0), c['k'])
        for c in chains:
            c['p'] = -jnp.where(strict, c['beta'] * c['qkk'][tt:2 * tt] * c['dec'], 0.0)
            c['t'] = eye + c['p']
        for _ in range(levels - 1):
            for c in chains:
                c['p'] = _dot(c['p'], c['p'])
            for c in chains:
                c['t'] = c['t'] + _dot(c['p'], c['t'])
        for c in chains:
            eg = jnp.exp(c['gcol'])
            rhs = jnp.concatenate([c['beta'] * c['v'], (c['beta'] * eg) * c['k']], axis=1)
            c['uw'] = _dot(c['t'], rhs)
            c['qd'] = c['q'] * eg
            c['glast'] = c['grow'][:, tt - 1:tt]
            c['kd'] = c['k'] * jnp.exp(c['glast'] - c['gcol'])
            c['s'] = s_ref[c['b'], c['h']]
        for c in chains:
            c['r'] = _dot(jnp.concatenate([c['uw'][:, DN_HD:2 * DN_HD], c['qd']], axis=0), c['s'])
        for c in chains:
            c['vn'] = c['uw'][:, 0:DN_HD] - c['r'][0:tt]
        for c in chains:
            ob[c['h'], c['rows'], :] = c['r'][tt:2 * tt] + _dot(c['qkk'][0:tt] * c['dec'], c['vn'])
            s_ref[c['b'], c['h']] = c['s'] * jnp.exp(c['glast']) + _dot_tn(c['kd'], c['vn'])
        return carry_

    lax.fori_loop(0, nb // group, seq_group, 0)

    gate = zg_ref[...].reshape(n, DN_WIDTH)
    dnw = dnw_ref[...]
    for h in range(DN_HEADS):
        hs = slice(h * DN_HD, (h + 1) * DN_HD)
        o_ref[:, :, hs] = (_rms(ob[h], dnw) * _silu(gate[:, hs])).reshape(tt, nb, DN_HD)


def _delta_call(qkv, zg, zba, coef, dnw, s0, prev, layer, bsz, t, tt, group):
    nb = SUBLANES
    tm = lambda wd: pl.BlockSpec((tt, nb, wd), lambda j, i: (i, j, 0))
    st_spec = _layer_spec(layer, (nb, DN_HEADS, DN_HD, DN_HD), lambda j, i: (j, 0, 0, 0))
    heads = pltpu.VMEM((DN_HEADS, tt * nb, DN_HD), F32)
    rows = pltpu.VMEM((tt * nb, BA_W), F32)
    n_in = 6
    return pl.pallas_call(
        functools.partial(_delta_body, tt=tt, group=group, n_alias=len(prev)),
        grid=(bsz // nb, t // tt),
        in_specs=[tm(QKV_W), tm(DN_WIDTH), tm(BA_W), _layer_full(layer, (2, BA_W)),
                  _layer_full(layer, (1, DN_HD)), st_spec] + [_ANY] * len(prev),
        out_specs=[tm(DN_WIDTH), st_spec],
        out_shape=[jax.ShapeDtypeStruct((t, bsz, DN_WIDTH), F32),
                   jax.ShapeDtypeStruct((DEPTH, bsz, DN_HEADS, DN_HD, DN_HD), F32)],
        input_output_aliases={n_in + k: 1 + k for k in range(len(prev))},
        scratch_shapes=[heads, heads, heads, rows, rows, heads],
        compiler_params=_params(2),
        name="delta",
    )(qkv, zg, zba, coef, dnw, s0, *prev)


def _hgrn_body(*refs, tt, n_tiles, group, n_alias):
    zh_ref, lb_ref, nw_ref, gm_ref, s0_ref = refs[:5]
    o_ref, s_ref, sbd, qsb, ksb, qeb, kdb, vb, elb, ob = refs[5 + n_alias:]
    ti = pl.program_id(1)
    nb = SUBLANES
    n = tt * nb
    mid = tt // 2 - 1
    sh_h = int(math.log2(HG_HD))

    @pl.when(ti == 0)
    def _():
        zero = jnp.zeros((HG_HD, HG_HD), F32)

        def init(b, c):
            for h in range(HG_HEADS):
                parts = [zero] * HG_HEADS
                parts[h] = s0_ref[b, h].T
                sbd[b, h * HG_HD:(h + 1) * HG_HD, :] = jnp.concatenate(parts, axis=1)
            return c
        lax.fori_loop(0, nb, init, 0)

    zh = zh_ref[...].reshape(n, ZH_W)
    lb = lb_ref[...]
    f = lb + (1.0 - lb) * _sigmoid_exp(zh[:, HG_WIDTH:2 * HG_WIDTH])
    q = _silu(zh[:, 0:HG_WIDTH]).reshape(tt, nb, HG_WIDTH)
    k = (1.0 - f).reshape(tt, nb, HG_WIDTH)
    bc = _dot_exact(jnp.log(f), _seq_tri(n, nb), _MM, x_first=False).reshape(tt, nb, HG_WIDTH)
    bm = bc[mid]
    bl = bc[tt - 1]
    half = HG_WIDTH // 2
    qs = q * jnp.exp(bc - bm[None])
    ks = k * jnp.exp(bm[None] - bc)
    for dst, val in ((qsb, qs), (ksb, ks), (qeb, qs * jnp.exp(bm)[None]), (kdb, ks * jnp.exp(bl - bm)[None])):
        val = val.reshape(n, HG_WIDTH)
        dst[0] = val[:, 0:half]
        dst[1] = val[:, half:HG_WIDTH]
    vb[0] = zh[:, 2 * HG_WIDTH:2 * HG_WIDTH + half]
    vb[1] = zh[:, 2 * HG_WIDTH + half:3 * HG_WIDTH]
    elb[...] = jnp.exp(bl)

    ri = lax.broadcasted_iota(jnp.int32, (HG_HEADS * tt, tt), 0)
    ci = lax.broadcasted_iota(jnp.int32, (HG_HEADS * tt, tt), 1)
    causal = ci <= (ri & (tt - 1))
    lane_head = lax.broadcasted_iota(jnp.int32, (tt, HG_WIDTH), 1) >> sh_h
    hmask = [jnp.where(lane_head == h, 1.0, 0.0).astype(F32) for h in range(HG_HEADS)]
    bd = ((lax.broadcasted_iota(jnp.int32, (HG_WIDTH, HG_WIDTH), 0) >> sh_h)
          == (lax.broadcasted_iota(jnp.int32, (HG_WIDTH, HG_WIDTH), 1) >> sh_h))

    def seq_group(bp, carry_):
        seqs = []
        for s in range(group):
            b = bp * group + s
            seq_rows = pl.ds(b, tt, stride=nb)
            seq = lambda ref: jnp.concatenate([ref[0, seq_rows, :], ref[1, seq_rows, :]], axis=1)
            seqs.append(dict(b=b, rows=seq_rows, qs=seq(qsb), ks=seq(ksb), qe=seq(qeb), kd=seq(kdb),
                             v=seq(vb), st=sbd[b], el=elb[pl.ds(b, 1), :]))
        for c in seqs:
            c['a'] = jnp.where(causal, _dot_nt(jnp.concatenate([c['qs'] * m for m in hmask], axis=0), c['ks']), 0.0)
            c['o'] = _dot_nt(c['qe'], c['st'])
            c['kv'] = _dot_tn(c['v'], c['kd'])
        for c in seqs:
            o = c['o']
            for h in range(HG_HEADS):
                o = o + hmask[h] * _dot(c['a'][h * tt:(h + 1) * tt], c['v'])
            ob[0, c['rows'], :] = o[:, 0:half]
            ob[1, c['rows'], :] = o[:, half:HG_WIDTH]
            sbd[c['b']] = c['st'] * c['el'] + jnp.where(bd, c['kv'], 0.0)
        return carry_

    lax.fori_loop(0, nb // group, seq_group, 0)

    o = jnp.concatenate([ob[0], ob[1]], axis=1)
    ms = _dot_exact(o * o, gm_ref[...], _MM) * (1.0 / HG_HD)
    o_ref[...] = (o * lax.rsqrt(ms + EPS) * nw_ref[...]
                  * _silu(zh[:, 3 * HG_WIDTH:4 * HG_WIDTH])).reshape(tt, nb, HG_WIDTH)

    @pl.when(ti == n_tiles - 1)
    def _():
        def fin(b, c):
            for h in range(HG_HEADS):
                blk = sbd[b, h * HG_HD:(h + 1) * HG_HD, :]
                s_ref[b, h] = blk[:, h * HG_HD:(h + 1) * HG_HD].T
            return c
        lax.fori_loop(0, nb, fin, 0)


def _hgrn_call(zh, lb, nw, gm, s0, prev, layer, bsz, t, tt, group):
    nb = SUBLANES
    n_tiles = t // tt
    st_spec = _layer_spec(layer, (nb, HG_HEADS, HG_HD, HG_HD), lambda j, i: (j, 0, 0, 0))
    tile = pltpu.VMEM((HG_WIDTH // LANES, tt * nb, LANES), F32)
    n_in = 5
    return pl.pallas_call(
        functools.partial(_hgrn_body, tt=tt, n_tiles=n_tiles, group=group, n_alias=len(prev)),
        grid=(bsz // nb, n_tiles),
        in_specs=[pl.BlockSpec((tt, nb, ZH_W), lambda j, i: (i, j, 0)),
                  _layer_full(layer, (1, HG_WIDTH)), _layer_full(layer, (1, HG_WIDTH)),
                  pl.BlockSpec((HG_WIDTH, HG_WIDTH), lambda j, i: (0, 0)), st_spec] + [_ANY] * len(prev),
        out_specs=[pl.BlockSpec((tt, nb, HG_WIDTH), lambda j, i: (i, j, 0)), st_spec],
        out_shape=[jax.ShapeDtypeStruct((t, bsz, HG_WIDTH), F32),
                   jax.ShapeDtypeStruct((DEPTH, bsz, HG_HEADS, HG_HD, HG_HD), F32)],
        input_output_aliases={n_in + k: 1 + k for k in range(len(prev))},
        scratch_shapes=[pltpu.VMEM((nb, HG_WIDTH, HG_WIDTH), F32),
                        tile, tile, tile, tile, tile, pltpu.VMEM((nb, HG_WIDTH), F32), tile],
        compiler_params=_params(2),
        name="hgrn",
    )(zh, lb, nw, gm, s0, *prev)


def _s5_body(u_ref, bh_ref, lam_ref, c_ref, d_ref, gw_ref, gb_ref, x0r_ref, x0i_ref,
             o_ref, xr_ref, xi_ref, xs, *, bsz, steps):
    i = pl.program_id(0)

    @pl.when(i == 0)
    def _():
        xr_ref[...] = x0r_ref[...]
        xi_ref[...] = x0i_ref[...]

    u = u_ref[...].reshape(steps * bsz, SSM_WIDTH)
    u_hi = u.astype(BF16)
    u_lo = (u - u_hi.astype(F32)).astype(BF16)
    bh = bh_ref[...]
    xs[...] = jnp.dot(u_hi, bh, preferred_element_type=F32) + jnp.dot(u_lo, bh, preferred_element_type=F32)

    lr = jnp.broadcast_to(lam_ref[0:1, :], (SUBLANES, SSM_FLAT))
    li = jnp.broadcast_to(lam_ref[1:2, :], (SUBLANES, SSM_FLAT))

    def row_block(rb, carry_):
        r0 = pl.multiple_of(rb * SUBLANES, SUBLANES)
        xr0 = xr_ref[pl.ds(r0, SUBLANES), :]
        xi0 = xi_ref[pl.ds(r0, SUBLANES), :]

        def tstep(t, st):
            xr, xi = st
            row = pl.ds(pl.multiple_of(t * bsz + r0, SUBLANES), SUBLANES)
            nr = lr * xr - li * xi + xs[row, 0:SSM_FLAT]
            ni = lr * xi + li * xr + xs[row, SSM_FLAT:2 * SSM_FLAT]
            xs[row, 0:SSM_FLAT] = nr
            xs[row, SSM_FLAT:2 * SSM_FLAT] = ni
            return nr, ni

        xr1, xi1 = lax.fori_loop(0, steps, tstep, (xr0, xi0))
        xr_ref[pl.ds(r0, SUBLANES), :] = xr1
        xi_ref[pl.ds(r0, SUBLANES), :] = xi1
        return carry_

    lax.fori_loop(0, bsz // SUBLANES, row_block, 0)

    y = _dot(xs[...], c_ref[...]) + d_ref[...] * u
    y = 0.5 * y * (1.0 + jnp.tanh(math.sqrt(2.0 / math.pi) * (y + 0.044715 * (y * y * y))))
    o_ref[...] = (y * _sigmoid(_dot(y, gw_ref[...]) + gb_ref[...])).reshape(steps, bsz, SSM_WIDTH)


def _s5_call(u, bh, lam, cm, d, gw, gb, x0r, x0i, layer, bsz, t, steps):
    rows = steps * bsz
    full = lambda shp: pl.BlockSpec(shp, lambda i: tuple(0 for _ in shp))
    lfull = lambda shp: _layer_full(layer, shp)
    return pl.pallas_call(
        functools.partial(_s5_body, bsz=bsz, steps=steps),
        grid=(t // steps,),
        in_specs=[pl.BlockSpec((steps, bsz, SSM_WIDTH), lambda i: (i, 0, 0)),
                  lfull((SSM_WIDTH, 2 * SSM_FLAT)),
                  lfull((2, SSM_FLAT)), lfull((2 * SSM_FLAT, SSM_WIDTH)), lfull((1, SSM_WIDTH)),
                  lfull((SSM_WIDTH, SSM_WIDTH)), lfull((1, SSM_WIDTH)),
                  full((bsz, SSM_FLAT)), full((bsz, SSM_FLAT))],
        out_specs=[pl.BlockSpec((steps, bsz, SSM_WIDTH), lambda i: (i, 0, 0)),
                   full((bsz, SSM_FLAT)), full((bsz, SSM_FLAT))],
        out_shape=[jax.ShapeDtypeStruct((t, bsz, SSM_WIDTH), F32),
                   jax.ShapeDtypeStruct((bsz, SSM_FLAT), F32),
                   jax.ShapeDtypeStruct((bsz, SSM_FLAT), F32)],
        scratch_shapes=[pltpu.VMEM((rows, 2 * SSM_FLAT), F32)],
        compiler_params=_params(1),
        name="s5",
    )(u, bh, lam, cm, d, gw, gb, x0r, x0i)


def _tail_body(h_ref, oa_ref, ob_ref, oc_ref, wo_ref, nf_ref, wua_ref, wub_ref, cwa_ref, cwb_ref,
               wd_ref, csa_ref, csb_ref, p_ref, npl_ref, wg_ref, wp_ref,
               out_ref, cso_a_ref, cso_b_ref,
               hn, p3, car_a, car_b, *, tt, nb, n_ffc, ffc, n_sub):
    i = pl.program_id(1)
    j = pl.program_id(2)
    n = tt * nb
    hist = FF_CONV - 1
    st = tt // n_sub
    sr = st * nb
    hr = hist * nb

    @pl.when(j == 0)
    def _():
        h2 = (h_ref[...].reshape(n, D_MODEL) + _dot(oa_ref[...].reshape(n, DN_WIDTH), wo_ref[0:DN_WIDTH, :])
              + _dot(ob_ref[...].reshape(n, SSM_WIDTH), wo_ref[DN_WIDTH:DN_WIDTH + SSM_WIDTH, :])
              + _dot(oc_ref[...].reshape(n, HG_WIDTH), wo_ref[DN_WIDTH + SSM_WIDTH:D_MODEL, :]))
        out_ref[...] = h2.reshape(tt, nb, D_MODEL)
        hn[...] = _rms(h2, nf_ref[...]).astype(BF16)

    @pl.when(i == 0)
    def _():
        def cp(b, c):
            car_a[j, :, b, :] = csa_ref[b]
            car_b[j, :, b, :] = csb_ref[b]
            return c
        lax.fori_loop(0, nb, cp, 0)

    prev = [car_a[j].reshape(hr, ffc), car_b[j].reshape(hr, ffc)]
    wd = wd_ref[...]
    for sb in range(n_sub):
        hs = hn[sb * sr:(sb + 1) * sr, :]
        halves = []
        for idx, (wu, cw) in enumerate(((wua_ref, cwa_ref), (wub_ref, cwb_ref))):
            x = jnp.concatenate([prev[idx], jnp.dot(hs, wu[...], preferred_element_type=F32)], axis=0)
            w = cw[...]
            acc = w[0:1] * x[0:sr]
            for s in range(1, FF_CONV):
                acc = acc + w[s:s + 1] * x[s * nb:s * nb + sr]
            halves.append(acc)
            prev[idx] = x[sr:sr + hr]
        out_ref[sb * st:(sb + 1) * st] += _dot(_silu(halves[0]) * halves[1], wd).reshape(st, nb, D_MODEL)
    for car, cso, last in ((car_a, cso_a_ref, prev[0]), (car_b, cso_b_ref, prev[1])):
        last = last.reshape(hist, nb, ffc)
        car[j] = last
        cso[...] = last

    @pl.when(j == n_ffc - 1)
    def _():
        def cp(b, c):
            p3[:, b, :] = p_ref[b]
            return c
        lax.fori_loop(0, nb, cp, 0)
        h3 = out_ref[...].reshape(n, D_MODEL)
        gate = _sigmoid(_dot(_rms(h3, npl_ref[...]), wg_ref[...]))
        out_ref[...] = (h3 + gate * _dot(p3[...].reshape(n, PLE_DIM), wp_ref[...])).reshape(tt, nb, D_MODEL)


def _tail_call(h, oa, ob, oc, wo, nf, wu, cw, wd, cs, p, npl, wg, wp, layer, bsz, t, tt, nb, ffc, n_sub):
    n_ffc = FF_DIM // ffc
    n_t = t // tt
    hist = FF_CONV - 1
    tm = lambda wd_: pl.BlockSpec((tt, nb, wd_), lambda jb, i, j: (i, jb, 0))
    full = lambda shp: _layer_full(layer, shp, pipeline_mode=pl.Buffered(1))
    lyr = lambda blk, idx: _layer_spec(layer, blk, idx)
    wchunk = (lambda blk, idx: _layer_spec(layer, blk, idx, pipeline_mode=pl.Buffered(1))) if n_ffc == 1 else lyr
    cs_a = wchunk((nb, hist, ffc), lambda jb, i, j: (jb, 0, j))
    cs_b = wchunk((nb, hist, ffc), lambda jb, i, j: (jb, 0, n_ffc + j))
    cso = pl.BlockSpec((None, hist, nb, ffc), lambda jb, i, j: (i, 0, jb, j))
    scratch = [pltpu.VMEM((tt * nb, D_MODEL), BF16),
               pltpu.VMEM((tt, nb, PLE_DIM), F32),
               pltpu.VMEM((n_ffc, hist, nb, ffc), F32),
               pltpu.VMEM((n_ffc, hist, nb, ffc), F32)]
    return pl.pallas_call(
        functools.partial(_tail_body, tt=tt, nb=nb, n_ffc=n_ffc, ffc=ffc, n_sub=n_sub),
        grid=(bsz // nb, n_t, n_ffc),
        in_specs=[tm(D_MODEL), tm(DN_WIDTH), tm(SSM_WIDTH), tm(HG_WIDTH),
                  full((D_MODEL, D_MODEL)), full((1, D_MODEL)),
                  wchunk((D_MODEL, ffc), lambda jb, i, j: (0, j)),
                  wchunk((D_MODEL, ffc), lambda jb, i, j: (0, n_ffc + j)),
                  lyr((FF_CONV, ffc), lambda jb, i, j: (0, j)),
                  lyr((FF_CONV, ffc), lambda jb, i, j: (0, n_ffc + j)),
                  wchunk((ffc, D_MODEL), lambda jb, i, j: (j, 0)),
                  cs_a, cs_b,
                  lyr((nb, tt, PLE_DIM), lambda jb, i, j: (jb, i, 0)),
                  full((1, D_MODEL)), full((D_MODEL, D_MODEL)), full((PLE_DIM, D_MODEL))],
        out_specs=[tm(D_MODEL), cso, cso],
        out_shape=[jax.ShapeDtypeStruct((t, bsz, D_MODEL), F32),
                   jax.ShapeDtypeStruct((n_t, hist, bsz, FF_DIM), F32),
                   jax.ShapeDtypeStruct((n_t, hist, bsz, FF_DIM), F32)],
        scratch_shapes=scratch,
        compiler_params=_params(3),
        name="tail",
    )(h, oa, ob, oc, wo, nf, wu, wu, cw, cw, wd, cs, cs, p, npl, wg, wp)


def _final_body(h_ref, g_ref, o_ref, *, nb):
    g = g_ref[...]

    def one(b, c):
        o_ref[b] = _rms(h_ref[:, b, :], g)
        return c

    lax.fori_loop(0, nb, one, 0)


def _final_call(h, g, bsz, t, tt, nb):
    return pl.pallas_call(
        functools.partial(_final_body, nb=nb),
        grid=(bsz // nb, t // tt),
        in_specs=[pl.BlockSpec((tt, nb, D_MODEL), lambda j, i: (i, j, 0)),
                  pl.BlockSpec((1, D_MODEL), lambda j, i: (0, 0))],
        out_specs=pl.BlockSpec((nb, tt, D_MODEL), lambda j, i: (j, i, 0)),
        out_shape=jax.ShapeDtypeStruct((bsz, t, D_MODEL), F32),
        compiler_params=_params(2),
        name="final",
    )(h, g)


def _pack(prm):
    (norm_mix, w_in, dn_conv_w, dn_a_log, dn_dt_bias, dn_norm, ssm_lam_re, ssm_lam_im, ssm_log_step,
     ssm_b_re, ssm_b_im, ssm_c_re, ssm_c_im, ssm_d, ssm_glu_w, ssm_glu_b, lower_bounds, hg_norm, w_out,
     norm_ffn, ffn_w_up, ffn_conv_w, ffn_w_down, norm_ple, ple_w_gate, ple_w_proj) = prm
    w = w_in.astype(BF16)
    o_gate = QKV_W
    o_beta = o_gate + DN_WIDTH
    o_a = o_beta + DN_HEADS
    o_u = o_a + DN_HEADS
    o_h = o_u + SSM_WIDTH
    w_packed = jnp.concatenate(
        [w[..., 0:o_beta], w[..., o_u:o_h + ZH_W], w[..., o_beta:o_u],
         jnp.zeros((DEPTH, D_MODEL, BA_W - 2 * DN_HEADS), BF16)], axis=-1)

    zeros4 = jnp.zeros((DEPTH, DN_HEADS), F32)
    pad = jnp.zeros((DEPTH, BA_W - 2 * DN_HEADS), F32)
    coef = jnp.stack([jnp.concatenate([zeros4, -jnp.exp(dn_a_log.astype(F32)), pad], axis=-1),
                      jnp.concatenate([zeros4, dn_dt_bias.astype(F32), pad], axis=-1)], axis=1)

    lre = ssm_lam_re.astype(F32)
    lim = ssm_lam_im.astype(F32)
    delta = jnp.exp(ssm_log_step.astype(F32))[..., None]
    mag = jnp.exp(lre * delta)
    lbr = mag * jnp.cos(lim * delta)
    lbi = mag * jnp.sin(lim * delta)
    den = lre * lre + lim * lim
    fr = ((lbr - 1.0) * lre + lbi * lim) / den
    fi = (lbi * lre - (lbr - 1.0) * lim) / den
    bre = ssm_b_re.astype(F32)
    bim = ssm_b_im.astype(F32)
    bbr = fr[..., None] * bre - fi[..., None] * bim
    bbi = fr[..., None] * bim + fi[..., None] * bre
    eye_g = jnp.eye(SSM_GROUPS, dtype=F32)

    def bdiag_in(m):
        return jnp.einsum('dgph,gk->dghkp', m, eye_g).reshape(DEPTH, SSM_WIDTH, SSM_FLAT)

    def bdiag_out(m):
        return jnp.einsum('dghp,gk->dgpkh', m, eye_g).reshape(DEPTH, SSM_FLAT, SSM_WIDTH)

    bmat = jnp.concatenate([bdiag_in(bbr), bdiag_in(bbi)], axis=-1).astype(BF16)
    cmat = jnp.concatenate([bdiag_out(ssm_c_re.astype(F32)), -bdiag_out(ssm_c_im.astype(F32))],
                           axis=1).astype(BF16)
    lam = jnp.stack([lbr.reshape(DEPTH, SSM_FLAT), lbi.reshape(DEPTH, SSM_FLAT)], axis=1)

    gmat = jnp.kron(jnp.eye(HG_HEADS, dtype=F32), jnp.ones((HG_HD, HG_HD), F32)).astype(BF16)
    row = lambda a: a.reshape(DEPTH, 1, a.shape[-1])
    return dict(
        norm_mix=row(norm_mix), w_in=w_packed, conv_w=dn_conv_w, coef=coef, dn_norm=row(dn_norm),
        bmat=bmat, lam=lam, cmat=cmat, ssm_d=row(ssm_d), glu_w=ssm_glu_w.astype(BF16), glu_b=row(ssm_glu_b),
        lb=row(lower_bounds), hg_norm=row(jnp.tile(hg_norm, (1, HG_HEADS))), gmat=gmat,
        w_out=w_out.astype(BF16), norm_ffn=row(norm_ffn), w_up=ffn_w_up.astype(BF16), ffn_conv_w=ffn_conv_w,
        w_down=ffn_w_down.astype(BF16), norm_ple=row(norm_ple), ple_gate=ple_w_gate.astype(BF16),
        ple_proj=ple_w_proj.astype(BF16))


def _tiles(bsz, t):
    if t >= DN_CHUNK:
        return dict(tok_tt=512 // bsz, tok_nb=bsz, dn_tt=DN_CHUNK, hg_tt=HG_CHUNK, s5_steps=512 // bsz,
                    ffc=FF_DIM, n_sub=4, fin_tt=128, group=4)
    return dict(tok_tt=t, tok_nb=512 // t, dn_tt=t, hg_tt=t, s5_steps=t, ffc=FF_DIM // 2, n_sub=4, fin_tt=t,
                group=4)


def _trunk(x, p, conv_qkv, delta, ssm_re, ssm_im, hgrn, conv_ffn, lp, norm_final):
    bsz, t, _ = x.shape
    tl = _tiles(bsz, t)
    sr_l, si_l, cf_l = [], [], []
    cq_prev, dn_prev, hg_prev = (), (), ()
    h = x
    for i in range(DEPTH):
        res = _proj_in(h, lp['norm_mix'], lp['w_in'], lp['conv_w'], conv_qkv, cq_prev, i, bsz, t,
                       tl['tok_tt'], tl['tok_nb'], from_btd=(i == 0))
        if i == 0:
            h = res[6]
        qkv, zg, zu, zh, zba, cq = res[:6]
        cq_prev = (cq,)
        o_a, dl = _delta_call(qkv, zg, zba, lp['coef'], lp['dn_norm'], delta, dn_prev, i, bsz, t,
                              tl['dn_tt'], tl['group'])
        dn_prev = (dl,)
        o_b, sr, si = _s5_call(zu, lp['bmat'], lp['lam'], lp['cmat'], lp['ssm_d'], lp['glu_w'], lp['glu_b'],
                               ssm_re[i].reshape(bsz, SSM_FLAT), ssm_im[i].reshape(bsz, SSM_FLAT),
                               i, bsz, t, tl['s5_steps'])
        o_c, hg = _hgrn_call(zh, lp['lb'], lp['hg_norm'], lp['gmat'], hgrn, hg_prev, i, bsz, t, tl['hg_tt'],
                             tl['group'])
        hg_prev = (hg,)
        h, cfa, cfb = _tail_call(h, o_a, o_b, o_c, lp['w_out'], lp['norm_ffn'], lp['w_up'], lp['ffn_conv_w'],
                                 lp['w_down'], conv_ffn, p, lp['norm_ple'], lp['ple_gate'],
                                 lp['ple_proj'], i, bsz, t, tl['tok_tt'], tl['tok_nb'], tl['ffc'], tl['n_sub'])
        sr_l.append(sr.reshape(bsz, SSM_GROUPS, SSM_STATE))
        si_l.append(si.reshape(bsz, SSM_GROUPS, SSM_STATE))
        cf_l.append(jnp.swapaxes(jnp.concatenate([cfa[-1], cfb[-1]], axis=-1), 0, 1))
    y = _final_call(h, norm_final, bsz, t, tl['fin_tt'], tl['tok_nb'])
    return (y, cq, dl, jnp.stack(sr_l), jnp.stack(si_l), hg, jnp.stack(cf_l))


def kernel(x_prompt, x_sample, p_prompt, p_sample, state_conv_qkv, state_delta, state_ssm_re, state_ssm_im, state_hgrn, state_conv_ffn, norm_mix, w_in, dn_conv_w, dn_a_log, dn_dt_bias, dn_norm, ssm_lam_re, ssm_lam_im, ssm_log_step, ssm_b_re, ssm_b_im, ssm_c_re, ssm_c_im, ssm_d, ssm_glu_w, ssm_glu_b, hg_lower, hg_norm, w_out, norm_ffn, ffn_w_up, ffn_conv_w, ffn_w_down, norm_ple, ple_w_gate, ple_w_proj, norm_final):
    lb_p = jax.nn.softmax(hg_lower.astype(F32), axis=0)
    lower_bounds = jnp.cumsum(lb_p, axis=0) - lb_p[0]
    prm = (norm_mix, w_in, dn_conv_w, dn_a_log, dn_dt_bias, dn_norm, ssm_lam_re, ssm_lam_im, ssm_log_step,
           ssm_b_re, ssm_b_im, ssm_c_re, ssm_c_im, ssm_d, ssm_glu_w, ssm_glu_b, lower_bounds, hg_norm, w_out,
           norm_ffn, ffn_w_up, ffn_conv_w, ffn_w_down, norm_ple, ple_w_gate, ple_w_proj)
    layers = _pack(prm)
    nf = norm_final.reshape(1, D_MODEL)

    bp = x_prompt.shape[0]
    z = lambda *shp: jnp.zeros((DEPTH, bp) + shp, F32)
    prompt = _trunk(x_prompt, p_prompt, z(DN_CONV - 1, QKV_W), z(DN_HEADS, DN_HD, DN_HD),
                    z(SSM_GROUPS, SSM_STATE), z(SSM_GROUPS, SSM_STATE), z(HG_HEADS, HG_HD, HG_HD),
                    z(FF_CONV - 1, 2 * FF_DIM), layers, nf)
    sample = _trunk(x_sample, p_sample, state_conv_qkv, state_delta, state_ssm_re, state_ssm_im, state_hgrn,
                    state_conv_ffn, layers, nf)
    return (prompt[0], sample[0]) + prompt[1:] + sample[1:]
```

```python
import functools
import math

import jax
import jax.numpy as jnp
from jax import lax
from jax.experimental import pallas as pl
from jax.experimental.pallas import tpu as pltpu

F32 = jnp.float32
BF16 = jnp.bfloat16

D_MODEL = 1024
DEPTH = 2
DN_HEADS = 4
DN_WIDTH = 512
DN_HD = 128
DN_CONV = 4
DN_CHUNK = 64
SSM_WIDTH = 256
SSM_GROUP = 16
SSM_GROUPS = 16
SSM_STATE = 64
SSM_FLAT = SSM_GROUPS * SSM_STATE
HG_WIDTH = 256
HG_HEADS = 4
HG_HD = 64
HG_CHUNK = 32
FF_DIM = 2816
FF_CONV = 3
PLE_DIM = 256
EPS = 1e-6

SUBLANES = 8
LANES = 128

QKV_W = 3 * DN_WIDTH
ZH_W = 4 * HG_WIDTH
BA_W = LANES
IN_PACKED = QKV_W + DN_WIDTH + SSM_WIDTH + ZH_W + BA_W
_Z_WIDTHS = (QKV_W, DN_WIDTH, SSM_WIDTH, ZH_W, BA_W)

VMEM_LIMIT = 56 * 1024 * 1024

_NT = (((1,), (1,)), ((), ()))
_TN = (((0,), (0,)), ((), ()))


def _dot(a, b):
    return jnp.dot(a.astype(BF16), b.astype(BF16), preferred_element_type=F32)


def _dot_nt(a, b):
    return lax.dot_general(a.astype(BF16), b.astype(BF16), _NT, preferred_element_type=F32)


def _dot_tn(a, b):
    return lax.dot_general(a.astype(BF16), b.astype(BF16), _TN, preferred_element_type=F32)


def _split3(x):
    hi = x.astype(BF16)
    r1 = x - hi.astype(F32)
    mid = r1.astype(BF16)
    lo = (r1 - mid.astype(F32)).astype(BF16)
    return hi, mid, lo


def _dot_exact(x, m, dims, x_first=True):
    if x_first:
        return sum(lax.dot_general(p, m, dims, preferred_element_type=F32) for p in _split3(x))
    return sum(lax.dot_general(m, p, dims, preferred_element_type=F32) for p in _split3(x))


_MM = (((1,), (0,)), ((), ()))


def _sigmoid_exp(x):
    return 1.0 / (1.0 + jnp.exp(-x))


def _sigmoid(x):
    return 0.5 * jnp.tanh(0.5 * x) + 0.5


def _silu(x):
    h = 0.5 * x
    return h + h * jnp.tanh(h)


def _layer_spec(layer, block, index, **kw):
    return pl.BlockSpec((None,) + tuple(block), lambda *g: (layer,) + tuple(index(*g)), **kw)


def _layer_full(layer, shape, **kw):
    return _layer_spec(layer, shape, lambda *g: (0,) * len(shape), **kw)


_ANY = pl.BlockSpec(memory_space=pl.ANY)


def _softplus(x):
    return jnp.maximum(x, 0.0) + jnp.log1p(jnp.exp(-jnp.abs(x)))


def _rms(x, g):
    ms = jnp.mean(x * x, axis=-1, keepdims=True)
    return x * lax.rsqrt(ms + EPS) * g


def _seq_tri(n, nb):
    r = lax.broadcasted_iota(jnp.int32, (n, n), 0)
    c = lax.broadcasted_iota(jnp.int32, (n, n), 1)
    return jnp.where(((r & (nb - 1)) == (c & (nb - 1))) & (c <= r), 1.0, 0.0).astype(BF16)


def _params(n_axes):
    return pltpu.CompilerParams(dimension_semantics=("arbitrary",) * n_axes, vmem_limit_bytes=VMEM_LIMIT)


def _proj_in_body(*refs, tt, nb, n_tiles, from_btd, n_alias):
    x_ref, g_ref, w_ref, cw_ref, cs_ref = refs[:5]
    outs = refs[5 + n_alias:]
    qkv_ref, zg_ref, zu_ref, zh_ref, zba_ref, cso_ref = outs[:6]
    buf = outs[-1]
    ti = pl.program_id(1)
    n = tt * nb
    hist = DN_CONV - 1
    if from_btd:
        h0 = outs[6]

        def cp(b, c):
            h0[:, b, :] = x_ref[b]
            return c

        lax.fori_loop(0, nb, cp, 0)
        x = h0[...].reshape(n, D_MODEL)
    else:
        x = x_ref[...].reshape(n, D_MODEL)
    xn = _rms(x, g_ref[...]).astype(BF16)

    @pl.when(ti == 0)
    def _():
        def cp(b, c):
            buf[:, b, :] = cs_ref[b]
            return c
        lax.fori_loop(0, nb, cp, 0)

    n_sub = 4
    st = tt // n_sub
    sr = st * nb
    hr = hist * nb
    cw = cw_ref[...]
    prev = buf[...].reshape(hr, QKV_W)
    for sb in range(n_sub):
        xs = xn[sb * sr:(sb + 1) * sr]
        ts = slice(sb * st, (sb + 1) * st)
        xcat = jnp.concatenate([prev, jnp.dot(xs, w_ref[:, 0:QKV_W], preferred_element_type=F32)], axis=0)
        c0 = QKV_W
        for ref, width in zip((zg_ref, zu_ref, zh_ref, zba_ref), _Z_WIDTHS[1:]):
            ref[ts] = jnp.dot(xs, w_ref[:, c0:c0 + width], preferred_element_type=F32).reshape(st, nb, width)
            c0 += width
        y = cw[0:1] * xcat[0:sr]
        for j in range(1, DN_CONV):
            y = y + cw[j:j + 1] * xcat[j * nb:j * nb + sr]
        prev = xcat[sr:sr + hr]
        y = _silu(y)
        for h in range(DN_HEADS):
            qs = slice(h * DN_HD, (h + 1) * DN_HD)
            ks = slice(DN_WIDTH + h * DN_HD, DN_WIDTH + (h + 1) * DN_HD)
            qh = y[:, qs]
            kh = y[:, ks]
            qn = qh * (lax.rsqrt(jnp.sum(qh * qh, axis=-1, keepdims=True) + EPS) * DN_HD ** -0.5)
            kn = kh * lax.rsqrt(jnp.sum(kh * kh, axis=-1, keepdims=True) + EPS)
            qkv_ref[ts, :, qs] = qn.reshape(st, nb, DN_HD)
            qkv_ref[ts, :, ks] = kn.reshape(st, nb, DN_HD)
        qkv_ref[ts, :, 2 * DN_WIDTH:QKV_W] = y[:, 2 * DN_WIDTH:QKV_W].reshape(st, nb, DN_WIDTH)
    buf[...] = prev.reshape(hist, nb, QKV_W)

    @pl.when(ti == n_tiles - 1)
    def _():
        def cp(b, c):
            cso_ref[b] = buf[:, b, :]
            return c
        lax.fori_loop(0, nb, cp, 0)


def _proj_in(x, g, w, cw, cs, prev, layer, bsz, t, tt, nb, from_btd):
    n_tiles = t // tt
    hist = DN_CONV - 1
    tm = lambda wd: pl.BlockSpec((tt, nb, wd), lambda j, i: (i, j, 0))
    x_spec = pl.BlockSpec((nb, tt, D_MODEL), lambda j, i: (j, i, 0)) if from_btd else tm(D_MODEL)
    cs_spec = _layer_spec(layer, (nb, hist, QKV_W), lambda j, i: (j, 0, 0))
    out_shape = [jax.ShapeDtypeStruct((t, bsz, wd), F32) for wd in _Z_WIDTHS]
    out_specs = [tm(wd) for wd in _Z_WIDTHS]
    out_shape.append(jax.ShapeDtypeStruct((DEPTH, bsz, hist, QKV_W), F32))
    out_specs.append(cs_spec)
    if from_btd:
        out_shape.append(jax.ShapeDtypeStruct((t, bsz, D_MODEL), F32))
        out_specs.append(tm(D_MODEL))
    n_in = 5
    return pl.pallas_call(
        functools.partial(_proj_in_body, tt=tt, nb=nb, n_tiles=n_tiles, from_btd=from_btd, n_alias=len(prev)),
        grid=(bsz // nb, n_tiles),
        in_specs=[x_spec, _layer_full(layer, (1, D_MODEL)), _layer_full(layer, (D_MODEL, IN_PACKED)),
                  _layer_full(layer, (DN_CONV, QKV_W)), cs_spec] + [_ANY] * len(prev),
        out_specs=out_specs,
        out_shape=out_shape,
        input_output_aliases={n_in + k: 5 + k for k in range(len(prev))},
        scratch_shapes=[pltpu.VMEM((hist, nb, QKV_W), F32)],
        compiler_params=_params(2),
        name="proj_in",
    )(x, g, w, cw, cs, *prev)


def _delta_body(*refs, tt, group, n_alias):
    qkv_ref, zg_ref, zba_ref, coef_ref, dnw_ref, s0_ref = refs[:6]
    o_ref, s_ref, qb, kb, vb, pb, gcb, ob = refs[6 + n_alias:]
    ti = pl.program_id(1)
    nb = SUBLANES
    n = tt * nb
    levels = int(math.log2(tt))

    @pl.when(ti == 0)
    def _():
        s_ref[...] = s0_ref[...]

    for h in range(DN_HEADS):
        for dst, off in ((qb, 0), (kb, DN_WIDTH), (vb, 2 * DN_WIDTH)):
            dst[h] = qkv_ref[:, :, off + h * DN_HD:off + (h + 1) * DN_HD].reshape(n, DN_HD)

    zba = zba_ref[...].reshape(n, BA_W)
    coef = coef_ref[...]
    lane = lax.broadcasted_iota(jnp.int32, zba.shape, 1)
    g = coef[0:1] * _softplus(zba + coef[1:2])
    pb[...] = jnp.where(lane < DN_HEADS, _sigmoid(zba), g)
    gcb[...] = _dot_exact(g, _seq_tri(n, nb), _MM, x_first=False)

    ri = lax.broadcasted_iota(jnp.int32, (tt, tt), 0)
    ci = lax.broadcasted_iota(jnp.int32, (tt, tt), 1)
    causal = ci <= ri
    strict = ci < ri
    eye = jnp.where(ci == ri, 1.0, 0.0).astype(F32)
    sel = jnp.where(lax.broadcasted_iota(jnp.int32, (SUBLANES, BA_W), 0)
                    == lax.broadcasted_iota(jnp.int32, (SUBLANES, BA_W), 1), 1.0, 0.0).astype(BF16)

    def seq_group(bp, carry_):
        chains = []
        for s in range(group):
            b = bp * group + s
            seq_rows = pl.ds(b, tt, stride=nb)
            pbv = pb[seq_rows, :]
            gcv = gcb[seq_rows, :]
            gct = _dot_exact(gcv, sel, _NT, x_first=False)
            for h in range(DN_HEADS):
                chains.append(dict(b=b, h=h, rows=seq_rows, q=qb[h, seq_rows, :], k=kb[h, seq_rows, :],
                                   v=vb[h, seq_rows, :],
                                   beta=pbv[:, h:h + 1], gcol=gcv[:, DN_HEADS + h:DN_HEADS + h + 1],
                                   grow=gct[DN_HEADS + h:DN_HEADS + h + 1, :]))
        for c in chains:
            c['dec'] = jnp.where(causal, jnp.exp(jnp.where(causal, c['gcol'] - c['grow'], 0.0)), 0.0)
            c['qkk'] = _dot_nt(jnp.concatenate([c['q'], c['k']], axis=0), c['k'])
        for c in chains:
            c['p'] = -jnp.where(strict, c['beta'] * c['qkk'][tt:2 * tt] * c['dec'], 0.0)
            c['t'] = eye + c['p']
        for _ in range(levels - 1):
            for c in chains:
                c['p'] = _dot(c['p'], c['p'])
            for c in chains:
                c['t'] = c['t'] + _dot(c['p'], c['t'])
        for c in chains:
            eg = jnp.exp(c['gcol'])
            rhs = jnp.concatenate([c['beta'] * c['v'], (c['beta'] * eg) * c['k']], axis=1)
            c['uw'] = _dot(c['t'], rhs)
            c['qd'] = c['q'] * eg
            c['glast'] = c['grow'][:, tt - 1:tt]
            c['kd'] = c['k'] * jnp.exp(c['glast'] - c['gcol'])
            c['s'] = s_ref[c['b'], c['h']]
        for c in chains:
            c['r'] = _dot(jnp.concatenate([c['uw'][:, DN_HD:2 * DN_HD], c['qd']], axis=0), c['s'])
        for c in chains:
            c['vn'] = c['uw'][:, 0:DN_HD] - c['r'][0:tt]
        for c in chains:
            ob[c['h'], c['rows'], :] = c['r'][tt:2 * tt] + _dot(c['qkk'][0:tt] * c['dec'], c['vn'])
            s_ref[c['b'], c['h']] = c['s'] * jnp.exp(c['glast']) + _dot_tn(c['kd'], c['vn'])
        return carry_

    lax.fori_loop(0, nb // group, seq_group, 0)

    gate = zg_ref[...].reshape(n, DN_WIDTH)
    dnw = dnw_ref[...]
    for h in range(DN_HEADS):
        hs = slice(h * DN_HD, (h + 1) * DN_HD)
        o_ref[:, :, hs] = (_rms(ob[h], dnw) * _silu(gate[:, hs])).reshape(tt, nb, DN_HD)


def _delta_call(qkv, zg, zba, coef, dnw, s0, prev, layer, bsz, t, tt, group):
    nb = SUBLANES
    tm = lambda wd: pl.BlockSpec((tt, nb, wd), lambda j, i: (i, j, 0))
    st_spec = _layer_spec(layer, (nb, DN_HEADS, DN_HD, DN_HD), lambda j, i: (j, 0, 0, 0))
    heads = pltpu.VMEM((DN_HEADS, tt * nb, DN_HD), F32)
    rows = pltpu.VMEM((tt * nb, BA_W), F32)
    n_in = 6
    return pl.pallas_call(
        functools.partial(_delta_body, tt=tt, group=group, n_alias=len(prev)),
        grid=(bsz // nb, t // tt),
        in_specs=[tm(QKV_W), tm(DN_WIDTH), tm(BA_W), _layer_full(layer, (2, BA_W)),
                  _layer_full(layer, (1, DN_HD)), st_spec] + [_ANY] * len(prev),
        out_specs=[tm(DN_WIDTH), st_spec],
        out_shape=[jax.ShapeDtypeStruct((t, bsz, DN_WIDTH), F32),
                   jax.ShapeDtypeStruct((DEPTH, bsz, DN_HEADS, DN_HD, DN_HD), F32)],
        input_output_aliases={n_in + k: 1 + k for k in range(len(prev))},
        scratch_shapes=[heads, heads, heads, rows, rows, heads],
        compiler_params=_params(2),
        name="delta",
    )(qkv, zg, zba, coef, dnw, s0, *prev)


def _hgrn_body(*refs, tt, n_tiles, group, n_alias):
    zh_ref, lb_ref, nw_ref, gm_ref, s0_ref = refs[:5]
    o_ref, s_ref, sbd, qsb, ksb, qeb, kdb, vb, elb, ob = refs[5 + n_alias:]
    ti = pl.program_id(1)
    nb = SUBLANES
    n = tt * nb
    mid = tt // 2 - 1
    sh_h = int(math.log2(HG_HD))

    @pl.when(ti == 0)
    def _():
        zero = jnp.zeros((HG_HD, HG_HD), F32)

        def init(b, c):
            for h in range(HG_HEADS):
                parts = [zero] * HG_HEADS
                parts[h] = s0_ref[b, h].T
                sbd[b, h * HG_HD:(h + 1) * HG_HD, :] = jnp.concatenate(parts, axis=1)
            return c
        for b in range(nb):
            init(b, 0)

    zh = zh_ref[...].reshape(n, ZH_W)
    lb = lb_ref[...]
    f = lb + (1.0 - lb) * _sigmoid_exp(zh[:, HG_WIDTH:2 * HG_WIDTH])
    q = _silu(zh[:, 0:HG_WIDTH]).reshape(tt, nb, HG_WIDTH)
    k = (1.0 - f).reshape(tt, nb, HG_WIDTH)
    bc = _dot_exact(jnp.log(f), _seq_tri(n, nb), _MM, x_first=False).reshape(tt, nb, HG_WIDTH)
    bm = bc[mid]
    bl = bc[tt - 1]
    half = HG_WIDTH // 2
    qs = q * jnp.exp(bc - bm[None])
    ks = k * jnp.exp(bm[None] - bc)
    for dst, val in ((qsb, qs), (ksb, ks), (qeb, qs * jnp.exp(bm)[None]), (kdb, ks * jnp.exp(bl - bm)[None])):
        val = val.reshape(n, HG_WIDTH)
        dst[0] = val[:, 0:half]
        dst[1] = val[:, half:HG_WIDTH]
    vb[0] = zh[:, 2 * HG_WIDTH:2 * HG_WIDTH + half]
    vb[1] = zh[:, 2 * HG_WIDTH + half:3 * HG_WIDTH]
    elb[...] = jnp.exp(bl)

    ri = lax.broadcasted_iota(jnp.int32, (HG_HEADS * tt, tt), 0)
    ci = lax.broadcasted_iota(jnp.int32, (HG_HEADS * tt, tt), 1)
    causal = ci <= (ri & (tt - 1))
    lane_head = lax.broadcasted_iota(jnp.int32, (tt, HG_WIDTH), 1) >> sh_h
    hmask = [jnp.where(lane_head == h, 1.0, 0.0).astype(F32) for h in range(HG_HEADS)]
    bd = ((lax.broadcasted_iota(jnp.int32, (HG_WIDTH, HG_WIDTH), 0) >> sh_h)
          == (lax.broadcasted_iota(jnp.int32, (HG_WIDTH, HG_WIDTH), 1) >> sh_h))

    def seq_group(bp, carry_):
        seqs = []
        for s in range(group):
            b = bp * group + s
            seq_rows = pl.ds(b, tt, stride=nb)
            seq = lambda ref: jnp.concatenate([ref[0, seq_rows, :], ref[1, seq_rows, :]], axis=1)
            seqs.append(dict(b=b, rows=seq_rows, qs=seq(qsb), ks=seq(ksb), qe=seq(qeb), kd=seq(kdb),
                             v=seq(vb), st=sbd[b], el=elb[pl.ds(b, 1), :]))
        for c in seqs:
            c['a'] = jnp.where(causal, _dot_nt(jnp.concatenate([c['qs'] * m for m in hmask], axis=0), c['ks']), 0.0)
            c['o'] = _dot_nt(c['qe'], c['st'])
            c['kv'] = _dot_tn(c['v'], c['kd'])
        for c in seqs:
            o = c['o']
            for h in range(HG_HEADS):
                o = o + hmask[h] * _dot(c['a'][h * tt:(h + 1) * tt], c['v'])
            ob[0, c['rows'], :] = o[:, 0:half]
            ob[1, c['rows'], :] = o[:, half:HG_WIDTH]
            sbd[c['b']] = c['st'] * c['el'] + jnp.where(bd, c['kv'], 0.0)
        return carry_

    lax.fori_loop(0, nb // group, seq_group, 0)

    o = jnp.concatenate([ob[0], ob[1]], axis=1)
    ms = _dot_exact(o * o, gm_ref[...], _MM) * (1.0 / HG_HD)
    o_ref[...] = (o * lax.rsqrt(ms + EPS) * nw_ref[...]
                  * _silu(zh[:, 3 * HG_WIDTH:4 * HG_WIDTH])).reshape(tt, nb, HG_WIDTH)

    @pl.when(ti == n_tiles - 1)
    def _():
        def fin(b, c):
            for h in range(HG_HEADS):
                blk = sbd[b, h * HG_HD:(h + 1) * HG_HD, :]
                s_ref[b, h] = blk[:, h * HG_HD:(h + 1) * HG_HD].T
            return c
        for b in range(nb):
            fin(b, 0)


def _hgrn_call(zh, lb, nw, gm, s0, prev, layer, bsz, t, tt, group):
    nb = SUBLANES
    n_tiles = t // tt
    st_spec = _layer_spec(layer, (nb, HG_HEADS, HG_HD, HG_HD), lambda j, i: (j, 0, 0, 0))
    tile = pltpu.VMEM((HG_WIDTH // LANES, tt * nb, LANES), F32)
    n_in = 5
    return pl.pallas_call(
        functools.partial(_hgrn_body, tt=tt, n_tiles=n_tiles, group=group, n_alias=len(prev)),
        grid=(bsz // nb, n_tiles),
        in_specs=[pl.BlockSpec((tt, nb, ZH_W), lambda j, i: (i, j, 0)),
                  _layer_full(layer, (1, HG_WIDTH)), _layer_full(layer, (1, HG_WIDTH)),
                  pl.BlockSpec((HG_WIDTH, HG_WIDTH), lambda j, i: (0, 0)), st_spec] + [_ANY] * len(prev),
        out_specs=[pl.BlockSpec((tt, nb, HG_WIDTH), lambda j, i: (i, j, 0)), st_spec],
        out_shape=[jax.ShapeDtypeStruct((t, bsz, HG_WIDTH), F32),
                   jax.ShapeDtypeStruct((DEPTH, bsz, HG_HEADS, HG_HD, HG_HD), F32)],
        input_output_aliases={n_in + k: 1 + k for k in range(len(prev))},
        scratch_shapes=[pltpu.VMEM((nb, HG_WIDTH, HG_WIDTH), F32),
                        tile, tile, tile, tile, tile, pltpu.VMEM((nb, HG_WIDTH), F32), tile],
        compiler_params=_params(2),
        name="hgrn",
    )(zh, lb, nw, gm, s0, *prev)


def _s5_body(u_ref, bh_ref, lam_ref, c_ref, d_ref, gw_ref, gb_ref, x0r_ref, x0i_ref,
             o_ref, xr_ref, xi_ref, *, bsz, steps):
    i = pl.program_id(0)

    @pl.when(i == 0)
    def _():
        xr_ref[...] = x0r_ref[...]
        xi_ref[...] = x0i_ref[...]

    u = u_ref[...].reshape(steps * bsz, SSM_WIDTH)
    bu = _dot(u, bh_ref[...])

    lr = jnp.broadcast_to(lam_ref[0:1, :], (SUBLANES, SSM_FLAT))
    li = jnp.broadcast_to(lam_ref[1:2, :], (SUBLANES, SSM_FLAT))

    n_rb = bsz // SUBLANES
    xr_all = xr_ref[...]
    xi_all = xi_ref[...]
    scanned = [[None] * n_rb for _ in range(steps)]
    last_r, last_i = [], []
    for rb in range(n_rb):
        xr = xr_all[rb * SUBLANES:(rb + 1) * SUBLANES]
        xi = xi_all[rb * SUBLANES:(rb + 1) * SUBLANES]
        for t in range(steps):
            r0 = t * bsz + rb * SUBLANES
            xr, xi = (lr * xr - li * xi + bu[r0:r0 + SUBLANES, 0:SSM_FLAT],
                      lr * xi + li * xr + bu[r0:r0 + SUBLANES, SSM_FLAT:2 * SSM_FLAT])
            scanned[t][rb] = jnp.concatenate([xr, xi], axis=1)
        last_r.append(xr)
        last_i.append(xi)
    xr_ref[...] = jnp.concatenate(last_r, axis=0)
    xi_ref[...] = jnp.concatenate(last_i, axis=0)
    xs = jnp.concatenate([scanned[t][rb] for t in range(steps) for rb in range(n_rb)], axis=0)

    y = _dot(xs, c_ref[...]) + d_ref[...] * u
    y = 0.5 * y * (1.0 + jnp.tanh(math.sqrt(2.0 / math.pi) * (y + 0.044715 * (y * y * y))))
    o_ref[...] = (y * _sigmoid(_dot(y, gw_ref[...]) + gb_ref[...])).reshape(steps, bsz, SSM_WIDTH)


def _s5_call(u, bh, lam, cm, d, gw, gb, x0r, x0i, layer, bsz, t, steps):
    rows = steps * bsz
    full = lambda shp: pl.BlockSpec(shp, lambda i: tuple(0 for _ in shp))
    lfull = lambda shp: _layer_full(layer, shp)
    return pl.pallas_call(
        functools.partial(_s5_body, bsz=bsz, steps=steps),
        grid=(t // steps,),
        in_specs=[pl.BlockSpec((steps, bsz, SSM_WIDTH), lambda i: (i, 0, 0)),
                  lfull((SSM_WIDTH, 2 * SSM_FLAT)),
                  lfull((2, SSM_FLAT)), lfull((2 * SSM_FLAT, SSM_WIDTH)), lfull((1, SSM_WIDTH)),
                  lfull((SSM_WIDTH, SSM_WIDTH)), lfull((1, SSM_WIDTH)),
                  full((bsz, SSM_FLAT)), full((bsz, SSM_FLAT))],
        out_specs=[pl.BlockSpec((steps, bsz, SSM_WIDTH), lambda i: (i, 0, 0)),
                   full((bsz, SSM_FLAT)), full((bsz, SSM_FLAT))],
        out_shape=[jax.ShapeDtypeStruct((t, bsz, SSM_WIDTH), F32),
                   jax.ShapeDtypeStruct((bsz, SSM_FLAT), F32),
                   jax.ShapeDtypeStruct((bsz, SSM_FLAT), F32)],
        compiler_params=_params(1),
        name="s5",
    )(u, bh, lam, cm, d, gw, gb, x0r, x0i)


def _tail_body(h_ref, oa_ref, ob_ref, oc_ref, wo_ref, nf_ref, wua_ref, wub_ref, cwa_ref, cwb_ref,
               wd_ref, csa_ref, csb_ref, p_ref, npl_ref, wg_ref, wp_ref,
               out_ref, cso_a_ref, cso_b_ref,
               hn, p3, car_a, car_b, *, tt, nb, n_ffc, ffc, n_sub):
    i = pl.program_id(1)
    j = pl.program_id(2)
    n = tt * nb
    hist = FF_CONV - 1
    st = tt // n_sub
    sr = st * nb
    hr = hist * nb

    @pl.when(j == 0)
    def _():
        h2 = (h_ref[...].reshape(n, D_MODEL) + _dot(oa_ref[...].reshape(n, DN_WIDTH), wo_ref[0:DN_WIDTH, :])
              + _dot(ob_ref[...].reshape(n, SSM_WIDTH), wo_ref[DN_WIDTH:DN_WIDTH + SSM_WIDTH, :])
              + _dot(oc_ref[...].reshape(n, HG_WIDTH), wo_ref[DN_WIDTH + SSM_WIDTH:D_MODEL, :]))
        out_ref[...] = h2.reshape(tt, nb, D_MODEL)
        hn[...] = _rms(h2, nf_ref[...]).astype(BF16)

    @pl.when(i == 0)
    def _():
        def cp(b, c):
            car_a[j, :, b, :] = csa_ref[b]
            car_b[j, :, b, :] = csb_ref[b]
            return c
        lax.fori_loop(0, nb, cp, 0)

    prev = [car_a[j].reshape(hr, ffc), car_b[j].reshape(hr, ffc)]
    wd = wd_ref[...]
    for sb in range(n_sub):
        hs = hn[sb * sr:(sb + 1) * sr, :]
        halves = []
        for idx, (wu, cw) in enumerate(((wua_ref, cwa_ref), (wub_ref, cwb_ref))):
            x = jnp.concatenate([prev[idx], jnp.dot(hs, wu[...], preferred_element_type=F32)], axis=0)
            w = cw[...]
            acc = w[0:1] * x[0:sr]
            for s in range(1, FF_CONV):
                acc = acc + w[s:s + 1] * x[s * nb:s * nb + sr]
            halves.append(acc)
            prev[idx] = x[sr:sr + hr]
        out_ref[sb * st:(sb + 1) * st] += _dot(_silu(halves[0]) * halves[1], wd).reshape(st, nb, D_MODEL)
    for car, cso, last in ((car_a, cso_a_ref, prev[0]), (car_b, cso_b_ref, prev[1])):
        last = last.reshape(hist, nb, ffc)
        car[j] = last
        cso[...] = last

    @pl.when(j == n_ffc - 1)
    def _():
        def cp(b, c):
            p3[:, b, :] = p_ref[b]
            return c
        lax.fori_loop(0, nb, cp, 0)
        h3 = out_ref[...].reshape(n, D_MODEL)
        gate = _sigmoid(_dot(_rms(h3, npl_ref[...]), wg_ref[...]))
        out_ref[...] = (h3 + gate * _dot(p3[...].reshape(n, PLE_DIM), wp_ref[...])).reshape(tt, nb, D_MODEL)


def _tail_call(h, oa, ob, oc, wo, nf, wu, cw, wd, cs, p, npl, wg, wp, layer, bsz, t, tt, nb, ffc, n_sub):
    n_ffc = FF_DIM // ffc
    n_t = t // tt
    hist = FF_CONV - 1
    tm = lambda wd_: pl.BlockSpec((tt, nb, wd_), lambda jb, i, j: (i, jb, 0))
    full = lambda shp: _layer_full(layer, shp, pipeline_mode=pl.Buffered(1))
    lyr = lambda blk, idx: _layer_spec(layer, blk, idx)
    wchunk = (lambda blk, idx: _layer_spec(layer, blk, idx, pipeline_mode=pl.Buffered(1))) if n_ffc == 1 else lyr
    cs_a = wchunk((nb, hist, ffc), lambda jb, i, j: (jb, 0, j))
    cs_b = wchunk((nb, hist, ffc), lambda jb, i, j: (jb, 0, n_ffc + j))
    cso = pl.BlockSpec((None, hist, nb, ffc), lambda jb, i, j: (i, 0, jb, j))
    scratch = [pltpu.VMEM((tt * nb, D_MODEL), BF16),
               pltpu.VMEM((tt, nb, PLE_DIM), F32),
               pltpu.VMEM((n_ffc, hist, nb, ffc), F32),
               pltpu.VMEM((n_ffc, hist, nb, ffc), F32)]
    return pl.pallas_call(
        functools.partial(_tail_body, tt=tt, nb=nb, n_ffc=n_ffc, ffc=ffc, n_sub=n_sub),
        grid=(bsz // nb, n_t, n_ffc),
        in_specs=[tm(D_MODEL), tm(DN_WIDTH), tm(SSM_WIDTH), tm(HG_WIDTH),
                  full((D_MODEL, D_MODEL)), full((1, D_MODEL)),
                  wchunk((D_MODEL, ffc), lambda jb, i, j: (0, j)),
                  wchunk((D_MODEL, ffc), lambda jb, i, j: (0, n_ffc + j)),
                  lyr((FF_CONV, ffc), lambda jb, i, j: (0, j)),
                  lyr((FF_CONV, ffc), lambda jb, i, j: (0, n_ffc + j)),
                  wchunk((ffc, D_MODEL), lambda jb, i, j: (j, 0)),
                  cs_a, cs_b,
                  lyr((nb, tt, PLE_DIM), lambda jb, i, j: (jb, i, 0)),
                  full((1, D_MODEL)), full((D_MODEL, D_MODEL)), full((PLE_DIM, D_MODEL))],
        out_specs=[tm(D_MODEL), cso, cso],
        out_shape=[jax.ShapeDtypeStruct((t, bsz, D_MODEL), F32),
                   jax.ShapeDtypeStruct((n_t, hist, bsz, FF_DIM), F32),
                   jax.ShapeDtypeStruct((n_t, hist, bsz, FF_DIM), F32)],
        scratch_shapes=scratch,
        compiler_params=_params(3),
        name="tail",
    )(h, oa, ob, oc, wo, nf, wu, wu, cw, cw, wd, cs, cs, p, npl, wg, wp)


def _final_body(h_ref, g_ref, o_ref, *, nb):
    g = g_ref[...]

    def one(b, c):
        o_ref[b] = _rms(h_ref[:, b, :], g)
        return c

    lax.fori_loop(0, nb, one, 0)


def _final_call(h, g, bsz, t, tt, nb):
    return pl.pallas_call(
        functools.partial(_final_body, nb=nb),
        grid=(bsz // nb, t // tt),
        in_specs=[pl.BlockSpec((tt, nb, D_MODEL), lambda j, i: (i, j, 0)),
                  pl.BlockSpec((1, D_MODEL), lambda j, i: (0, 0))],
        out_specs=pl.BlockSpec((nb, tt, D_MODEL), lambda j, i: (j, i, 0)),
        out_shape=jax.ShapeDtypeStruct((bsz, t, D_MODEL), F32),
        compiler_params=_params(2),
        name="final",
    )(h, g)


def _pack(prm):
    (norm_mix, w_in, dn_conv_w, dn_a_log, dn_dt_bias, dn_norm, ssm_lam_re, ssm_lam_im, ssm_log_step,
     ssm_b_re, ssm_b_im, ssm_c_re, ssm_c_im, ssm_d, ssm_glu_w, ssm_glu_b, lower_bounds, hg_norm, w_out,
     norm_ffn, ffn_w_up, ffn_conv_w, ffn_w_down, norm_ple, ple_w_gate, ple_w_proj) = prm
    w = w_in.astype(BF16)
    o_gate = QKV_W
    o_beta = o_gate + DN_WIDTH
    o_a = o_beta + DN_HEADS
    o_u = o_a + DN_HEADS
    o_h = o_u + SSM_WIDTH
    w_packed = jnp.concatenate(
        [w[..., 0:o_beta], w[..., o_u:o_h + ZH_W], w[..., o_beta:o_u],
         jnp.zeros((DEPTH, D_MODEL, BA_W - 2 * DN_HEADS), BF16)], axis=-1)

    zeros4 = jnp.zeros((DEPTH, DN_HEADS), F32)
    pad = jnp.zeros((DEPTH, BA_W - 2 * DN_HEADS), F32)
    coef = jnp.stack([jnp.concatenate([zeros4, -jnp.exp(dn_a_log.astype(F32)), pad], axis=-1),
                      jnp.concatenate([zeros4, dn_dt_bias.astype(F32), pad], axis=-1)], axis=1)

    lre = ssm_lam_re.astype(F32)
    lim = ssm_lam_im.astype(F32)
    delta = jnp.exp(ssm_log_step.astype(F32))[..., None]
    mag = jnp.exp(lre * delta)
    lbr = mag * jnp.cos(lim * delta)
    lbi = mag * jnp.sin(lim * delta)
    den = lre * lre + lim * lim
    fr = ((lbr - 1.0) * lre + lbi * lim) / den
    fi = (lbi * lre - (lbr - 1.0) * lim) / den
    bre = ssm_b_re.astype(F32)
    bim = ssm_b_im.astype(F32)
    bbr = fr[..., None] * bre - fi[..., None] * bim
    bbi = fr[..., None] * bim + fi[..., None] * bre
    eye_g = jnp.eye(SSM_GROUPS, dtype=F32)

    def bdiag_in(m):
        return jnp.einsum('dgph,gk->dghkp', m, eye_g).reshape(DEPTH, SSM_WIDTH, SSM_FLAT)

    def bdiag_out(m):
        return jnp.einsum('dghp,gk->dgpkh', m, eye_g).reshape(DEPTH, SSM_FLAT, SSM_WIDTH)

    bmat = jnp.concatenate([bdiag_in(bbr), bdiag_in(bbi)], axis=-1).astype(BF16)
    cmat = jnp.concatenate([bdiag_out(ssm_c_re.astype(F32)), -bdiag_out(ssm_c_im.astype(F32))],
                           axis=1).astype(BF16)
    lam = jnp.stack([lbr.reshape(DEPTH, SSM_FLAT), lbi.reshape(DEPTH, SSM_FLAT)], axis=1)

    gmat = jnp.kron(jnp.eye(HG_HEADS, dtype=F32), jnp.ones((HG_HD, HG_HD), F32)).astype(BF16)
    row = lambda a: a.reshape(DEPTH, 1, a.shape[-1])
    return dict(
        norm_mix=row(norm_mix), w_in=w_packed, conv_w=dn_conv_w, coef=coef, dn_norm=row(dn_norm),
        bmat=bmat, lam=lam, cmat=cmat, ssm_d=row(ssm_d), glu_w=ssm_glu_w.astype(BF16), glu_b=row(ssm_glu_b),
        lb=row(lower_bounds), hg_norm=row(jnp.tile(hg_norm, (1, HG_HEADS))), gmat=gmat,
        w_out=w_out.astype(BF16), norm_ffn=row(norm_ffn), w_up=ffn_w_up.astype(BF16), ffn_conv_w=ffn_conv_w,
        w_down=ffn_w_down.astype(BF16), norm_ple=row(norm_ple), ple_gate=ple_w_gate.astype(BF16),
        ple_proj=ple_w_proj.astype(BF16))


def _tiles(bsz, t):
    if t >= DN_CHUNK:
        return dict(tok_tt=512 // bsz, tok_nb=bsz, dn_tt=DN_CHUNK, hg_tt=HG_CHUNK, s5_steps=512 // bsz,
                    ffc=FF_DIM, n_sub=4, fin_tt=128, group=4)
    return dict(tok_tt=t, tok_nb=512 // t, dn_tt=t, hg_tt=t, s5_steps=t, ffc=FF_DIM // 2, n_sub=4, fin_tt=t,
                group=4)


def _trunk(x, p, conv_qkv, delta, ssm_re, ssm_im, hgrn, conv_ffn, lp, norm_final):
    bsz, t, _ = x.shape
    tl = _tiles(bsz, t)
    sr_l, si_l, cf_l = [], [], []
    cq_prev, dn_prev, hg_prev = (), (), ()
    h = x
    for i in range(DEPTH):
        res = _proj_in(h, lp['norm_mix'], lp['w_in'], lp['conv_w'], conv_qkv, cq_prev, i, bsz, t,
                       tl['tok_tt'], tl['tok_nb'], from_btd=(i == 0))
        if i == 0:
            h = res[6]
        qkv, zg, zu, zh, zba, cq = res[:6]
        cq_prev = (cq,)
        o_a, dl = _delta_call(qkv, zg, zba, lp['coef'], lp['dn_norm'], delta, dn_prev, i, bsz, t,
                              tl['dn_tt'], tl['group'])
        dn_prev = (dl,)
        o_b, sr, si = _s5_call(zu, lp['bmat'], lp['lam'], lp['cmat'], lp['ssm_d'], lp['glu_w'], lp['glu_b'],
                               ssm_re[i].reshape(bsz, SSM_FLAT), ssm_im[i].reshape(bsz, SSM_FLAT),
                               i, bsz, t, tl['s5_steps'])
        o_c, hg = _hgrn_call(zh, lp['lb'], lp['hg_norm'], lp['gmat'], hgrn, hg_prev, i, bsz, t, tl['hg_tt'],
                             tl['group'])
        hg_prev = (hg,)
        h, cfa, cfb = _tail_call(h, o_a, o_b, o_c, lp['w_out'], lp['norm_ffn'], lp['w_up'], lp['ffn_conv_w'],
                                 lp['w_down'], conv_ffn, p, lp['norm_ple'], lp['ple_gate'],
                                 lp['ple_proj'], i, bsz, t, tl['tok_tt'], tl['tok_nb'], tl['ffc'], tl['n_sub'])
        sr_l.append(sr.reshape(bsz, SSM_GROUPS, SSM_STATE))
        si_l.append(si.reshape(bsz, SSM_GROUPS, SSM_STATE))
        cf_l.append(jnp.swapaxes(jnp.concatenate([cfa[-1], cfb[-1]], axis=-1), 0, 1))
    y = _final_call(h, norm_final, bsz, t, tl['fin_tt'], tl['tok_nb'])
    return (y, cq, dl, jnp.stack(sr_l), jnp.stack(si_l), hg, jnp.stack(cf_l))


def kernel(x_prompt, x_sample, p_prompt, p_sample, state_conv_qkv, state_delta, state_ssm_re, state_ssm_im, state_hgrn, state_conv_ffn, norm_mix, w_in, dn_conv_w, dn_a_log, dn_dt_bias, dn_norm, ssm_lam_re, ssm_lam_im, ssm_log_step, ssm_b_re, ssm_b_im, ssm_c_re, ssm_c_im, ssm_d, ssm_glu_w, ssm_glu_b, hg_lower, hg_norm, w_out, norm_ffn, ffn_w_up, ffn_conv_w, ffn_w_down, norm_ple, ple_w_gate, ple_w_proj, norm_final):
    lb_p = jax.nn.softmax(hg_lower.astype(F32), axis=0)
    lower_bounds = jnp.cumsum(lb_p, axis=0) - lb_p[0]
    prm = (norm_mix, w_in, dn_conv_w, dn_a_log, dn_dt_bias, dn_norm, ssm_lam_re, ssm_lam_im, ssm_log_step,
           ssm_b_re, ssm_b_im, ssm_c_re, ssm_c_im, ssm_d, ssm_glu_w, ssm_glu_b, lower_bounds, hg_norm, w_out,
           norm_ffn, ffn_w_up, ffn_conv_w, ffn_w_down, norm_ple, ple_w_gate, ple_w_proj)
    layers = _pack(prm)
    nf = norm_final.reshape(1, D_MODEL)

    bp = x_prompt.shape[0]
    z = lambda *shp: jnp.zeros((DEPTH, bp) + shp, F32)
    prompt = _trunk(x_prompt, p_prompt, z(DN_CONV - 1, QKV_W), z(DN_HEADS, DN_HD, DN_HD),
                    z(SSM_GROUPS, SSM_STATE), z(SSM_GROUPS, SSM_STATE), z(HG_HEADS, HG_HD, HG_HD),
                    z(FF_CONV - 1, 2 * FF_DIM), layers, nf)
    sample = _trunk(x_sample, p_sample, state_conv_qkv, state_delta, state_ssm_re, state_ssm_im, state_hgrn,
                    state_conv_ffn, layers, nf)
    return (prompt[0], sample[0]) + prompt[1:] + sample[1:]
```

```python
import functools
import math

import jax
import jax.numpy as jnp
from jax import lax
from jax.experimental import pallas as pl
from jax.experimental.pallas import tpu as pltpu

F32 = jnp.float32
BF16 = jnp.bfloat16

D_MODEL = 1024
DEPTH = 2
DN_HEADS = 4
DN_WIDTH = 512
DN_HD = 128
DN_CONV = 4
DN_CHUNK = 64
SSM_WIDTH = 256
SSM_GROUP = 16
SSM_GROUPS = 16
SSM_STATE = 64
SSM_FLAT = SSM_GROUPS * SSM_STATE
HG_WIDTH = 256
HG_HEADS = 4
HG_HD = 64
HG_CHUNK = 32
FF_DIM = 2816
FF_CONV = 3
PLE_DIM = 256
EPS = 1e-6

SUBLANES = 8
LANES = 128

QKV_W = 3 * DN_WIDTH
ZH_W = 4 * HG_WIDTH
BA_W = LANES
IN_PACKED = QKV_W + DN_WIDTH + SSM_WIDTH + ZH_W + BA_W
IN_RAW = QKV_W + DN_WIDTH + 2 * DN_HEADS + SSM_WIDTH + ZH_W
_Z_WIDTHS = (QKV_W, DN_WIDTH, SSM_WIDTH, ZH_W, BA_W)

VMEM_LIMIT = 56 * 1024 * 1024

_NT = (((1,), (1,)), ((), ()))
_TN = (((0,), (0,)), ((), ()))


def _dot(a, b):
    return jnp.dot(a.astype(BF16), b.astype(BF16), preferred_element_type=F32)


def _dot_nt(a, b):
    return lax.dot_general(a.astype(BF16), b.astype(BF16), _NT, preferred_element_type=F32)


def _dot_tn(a, b):
    return lax.dot_general(a.astype(BF16), b.astype(BF16), _TN, preferred_element_type=F32)


def _split3(x):
    hi = x.astype(BF16)
    r1 = x - hi.astype(F32)
    mid = r1.astype(BF16)
    lo = (r1 - mid.astype(F32)).astype(BF16)
    return hi, mid, lo


def _dot_exact(x, m, dims, x_first=True):
    if x_first:
        return sum(lax.dot_general(p, m, dims, preferred_element_type=F32) for p in _split3(x))
    return sum(lax.dot_general(m, p, dims, preferred_element_type=F32) for p in _split3(x))


_MM = (((1,), (0,)), ((), ()))


def _sigmoid_exp(x):
    return 1.0 / (1.0 + jnp.exp(-x))


def _sigmoid(x):
    return 0.5 * jnp.tanh(0.5 * x) + 0.5


def _silu(x):
    h = 0.5 * x
    return h + h * jnp.tanh(h)


def _layer_spec(layer, block, index, **kw):
    return pl.BlockSpec((None,) + tuple(block), lambda *g: (layer,) + tuple(index(*g)), **kw)


def _layer_full(layer, shape, **kw):
    return _layer_spec(layer, shape, lambda *g: (0,) * len(shape), **kw)


_ANY = pl.BlockSpec(memory_space=pl.ANY)


def _softplus(x):
    return jnp.maximum(x, 0.0) + jnp.log1p(jnp.exp(-jnp.abs(x)))


def _rms(x, g):
    ms = jnp.mean(x * x, axis=-1, keepdims=True)
    return x * lax.rsqrt(ms + EPS) * g


def _seq_tri(n, nb):
    r = lax.broadcasted_iota(jnp.int32, (n, n), 0)
    c = lax.broadcasted_iota(jnp.int32, (n, n), 1)
    return jnp.where(((r & (nb - 1)) == (c & (nb - 1))) & (c <= r), 1.0, 0.0).astype(BF16)


def _params(n_axes):
    return pltpu.CompilerParams(dimension_semantics=("arbitrary",) * n_axes, vmem_limit_bytes=VMEM_LIMIT)


def _proj_in_body(*refs, tt, nb, n_tiles, from_btd, n_alias):
    x_ref, g_ref, wraw_ref, cw_ref, cs_ref = refs[:5]
    outs = refs[5 + n_alias:]
    qkv_ref, zg_ref, zu_ref, zh_ref, zba_ref, cso_ref = outs[:6]
    buf, w_ref = outs[-2:]
    ti = pl.program_id(1)
    n = tt * nb
    hist = DN_CONV - 1

    @pl.when((pl.program_id(0) == 0) & (ti == 0))
    def _():
        o_ba = QKV_W + DN_WIDTH
        n_ba = 2 * DN_HEADS
        n_rest = SSM_WIDTH + ZH_W
        w_ref[:, 0:o_ba] = wraw_ref[:, 0:o_ba].astype(BF16)
        w_ref[:, o_ba:o_ba + n_rest] = wraw_ref[:, o_ba + n_ba:o_ba + n_ba + n_rest].astype(BF16)
        w_ref[:, o_ba + n_rest:IN_PACKED] = jnp.concatenate(
            [wraw_ref[:, o_ba:o_ba + n_ba], jnp.zeros((D_MODEL, BA_W - n_ba), F32)], axis=1).astype(BF16)
    if from_btd:
        h0 = outs[6]

        def cp(b, c):
            h0[:, b, :] = x_ref[b]
            return c

        lax.fori_loop(0, nb, cp, 0)
        x = h0[...].reshape(n, D_MODEL)
    else:
        x = x_ref[...].reshape(n, D_MODEL)
    xn = _rms(x, g_ref[...]).astype(BF16)

    @pl.when(ti == 0)
    def _():
        def cp(b, c):
            buf[:, b, :] = cs_ref[b]
            return c
        lax.fori_loop(0, nb, cp, 0)

    n_sub = 4
    st = tt // n_sub
    sr = st * nb
    hr = hist * nb
    cw = cw_ref[...]
    prev = buf[...].reshape(hr, QKV_W)
    for sb in range(n_sub):
        xs = xn[sb * sr:(sb + 1) * sr]
        ts = slice(sb * st, (sb + 1) * st)
        xcat = jnp.concatenate([prev, jnp.dot(xs, w_ref[:, 0:QKV_W], preferred_element_type=F32)], axis=0)
        c0 = QKV_W
        for ref, width in zip((zg_ref, zu_ref, zh_ref, zba_ref), _Z_WIDTHS[1:]):
            ref[ts] = jnp.dot(xs, w_ref[:, c0:c0 + width], preferred_element_type=F32).reshape(st, nb, width)
            c0 += width
        y = cw[0:1] * xcat[0:sr]
        for j in range(1, DN_CONV):
            y = y + cw[j:j + 1] * xcat[j * nb:j * nb + sr]
        prev = xcat[sr:sr + hr]
        y = _silu(y)
        for h in range(DN_HEADS):
            qs = slice(h * DN_HD, (h + 1) * DN_HD)
            ks = slice(DN_WIDTH + h * DN_HD, DN_WIDTH + (h + 1) * DN_HD)
            qh = y[:, qs]
            kh = y[:, ks]
            qn = qh * (lax.rsqrt(jnp.sum(qh * qh, axis=-1, keepdims=True) + EPS) * DN_HD ** -0.5)
            kn = kh * lax.rsqrt(jnp.sum(kh * kh, axis=-1, keepdims=True) + EPS)
            qkv_ref[ts, :, qs] = qn.reshape(st, nb, DN_HD)
            qkv_ref[ts, :, ks] = kn.reshape(st, nb, DN_HD)
        qkv_ref[ts, :, 2 * DN_WIDTH:QKV_W] = y[:, 2 * DN_WIDTH:QKV_W].reshape(st, nb, DN_WIDTH)
    buf[...] = prev.reshape(hist, nb, QKV_W)

    @pl.when(ti == n_tiles - 1)
    def _():
        def cp(b, c):
            cso_ref[b] = buf[:, b, :]
            return c
        lax.fori_loop(0, nb, cp, 0)


def _proj_in(x, g, w, cw, cs, prev, layer, bsz, t, tt, nb, from_btd):
    n_tiles = t // tt
    hist = DN_CONV - 1
    tm = lambda wd: pl.BlockSpec((tt, nb, wd), lambda j, i: (i, j, 0))
    x_spec = pl.BlockSpec((nb, tt, D_MODEL), lambda j, i: (j, i, 0)) if from_btd else tm(D_MODEL)
    cs_spec = _layer_spec(layer, (nb, hist, QKV_W), lambda j, i: (j, 0, 0))
    out_shape = [jax.ShapeDtypeStruct((t, bsz, wd), F32) for wd in _Z_WIDTHS]
    out_specs = [tm(wd) for wd in _Z_WIDTHS]
    out_shape.append(jax.ShapeDtypeStruct((DEPTH, bsz, hist, QKV_W), F32))
    out_specs.append(cs_spec)
    if from_btd:
        out_shape.append(jax.ShapeDtypeStruct((t, bsz, D_MODEL), F32))
        out_specs.append(tm(D_MODEL))
    n_in = 5
    return pl.pallas_call(
        functools.partial(_proj_in_body, tt=tt, nb=nb, n_tiles=n_tiles, from_btd=from_btd, n_alias=len(prev)),
        grid=(bsz // nb, n_tiles),
        in_specs=[x_spec, _layer_full(layer, (1, D_MODEL)),
                  _layer_full(layer, (D_MODEL, IN_RAW), pipeline_mode=pl.Buffered(1)),
                  _layer_full(layer, (DN_CONV, QKV_W)), cs_spec] + [_ANY] * len(prev),
        out_specs=out_specs,
        out_shape=out_shape,
        input_output_aliases={n_in + k: 5 + k for k in range(len(prev))},
        scratch_shapes=[pltpu.VMEM((hist, nb, QKV_W), F32), pltpu.VMEM((D_MODEL, IN_PACKED), BF16)],
        compiler_params=_params(2),
        name="proj_in",
    )(x, g, w, cw, cs, *prev)


def _delta_body(*refs, tt, group, n_alias):
    qkv_ref, zg_ref, zba_ref, coef_ref, dnw_ref, s0_ref = refs[:6]
    o_ref, s_ref, qb, kb, vb, pb, gcb, ob = refs[6 + n_alias:]
    ti = pl.program_id(1)
    nb = SUBLANES
    n = tt * nb
    levels = int(math.log2(tt))

    @pl.when(ti == 0)
    def _():
        s_ref[...] = s0_ref[...]

    for h in range(DN_HEADS):
        for dst, off in ((qb, 0), (kb, DN_WIDTH), (vb, 2 * DN_WIDTH)):
            dst[h] = qkv_ref[:, :, off + h * DN_HD:off + (h + 1) * DN_HD].reshape(n, DN_HD)

    zba = zba_ref[...].reshape(n, BA_W)
    coef = coef_ref[...]
    lane = lax.broadcasted_iota(jnp.int32, zba.shape, 1)
    g = coef[0:1] * _softplus(zba + coef[1:2])
    pb[...] = jnp.where(lane < DN_HEADS, _sigmoid(zba), g)
    gcb[...] = _dot_exact(g, _seq_tri(n, nb), _MM, x_first=False)

    ri = lax.broadcasted_iota(jnp.int32, (tt, tt), 0)
    ci = lax.broadcasted_iota(jnp.int32, (tt, tt), 1)
    causal = ci <= ri
    strict = ci < ri
    eye = jnp.where(ci == ri, 1.0, 0.0).astype(F32)
    sel = jnp.where(lax.broadcasted_iota(jnp.int32, (SUBLANES, BA_W), 0)
                    == lax.broadcasted_iota(jnp.int32, (SUBLANES, BA_W), 1), 1.0, 0.0).astype(BF16)

    def seq_group(bp, carry_):
        chains = []
        for s in range(group):
            b = bp * group + s
            seq_rows = pl.ds(b, tt, stride=nb)
            pbv = pb[seq_rows, :]
            gcv = gcb[seq_rows, :]
            gct = _dot_exact(gcv, sel, _NT, x_first=False)
            for h in range(DN_HEADS):
                chains.append(dict(b=b, h=h, rows=seq_rows, q=qb[h, seq_rows, :], k=kb[h, seq_rows, :],
                                   v=vb[h, seq_rows, :],
                                   beta=pbv[:, h:h + 1], gcol=gcv[:, DN_HEADS + h:DN_HEADS + h + 1],
                                   grow=gct[DN_HEADS + h:DN_HEADS + h + 1, :]))
        for c in chains:
            c['dec'] = jnp.where(causal, jnp.exp(jnp.where(causal, c['gcol'] - c['grow'], 0.0)), 0.0)
            c['qkk'] = _dot_nt(jnp.concatenate([c['q'], c['k']], axis=0), c['k'])
        for c in chains:
            c['p'] = -jnp.where(strict, c['beta'] * c['qkk'][tt:2 * tt] * c['dec'], 0.0)
            c['t'] = eye + c['p']
        for _ in range(levels - 1):
            for c in chains:
                c['p'] = _dot(c['p'], c['p'])
            for c in chains:
                c['t'] = c['t'] + _dot(c['p'], c['t'])
        for c in chains:
            eg = jnp.exp(c['gcol'])
            rhs = jnp.concatenate([c['beta'] * c['v'], (c['beta'] * eg) * c['k']], axis=1)
            c['uw'] = _dot(c['t'], rhs)
            c['qd'] = c['q'] * eg
            c['glast'] = c['grow'][:, tt - 1:tt]
            c['kd'] = c['k'] * jnp.exp(c['glast'] - c['gcol'])
            c['s'] = s_ref[c['b'], c['h']]
        for c in chains:
            c['r'] = _dot(jnp.concatenate([c['uw'][:, DN_HD:2 * DN_HD], c['qd']], axis=0), c['s'])
        for c in chains:
            c['vn'] = c['uw'][:, 0:DN_HD] - c['r'][0:tt]
        for c in chains:
            ob[c['h'], c['rows'], :] = c['r'][tt:2 * tt] + _dot(c['qkk'][0:tt] * c['dec'], c['vn'])
            s_ref[c['b'], c['h']] = c['s'] * jnp.exp(c['glast']) + _dot_tn(c['kd'], c['vn'])
        return carry_

    lax.fori_loop(0, nb // group, seq_group, 0)

    gate = zg_ref[...].reshape(n, DN_WIDTH)
    dnw = dnw_ref[...]
    for h in range(DN_HEADS):
        hs = slice(h * DN_HD, (h + 1) * DN_HD)
        o_ref[:, :, hs] = (_rms(ob[h], dnw) * _silu(gate[:, hs])).reshape(tt, nb, DN_HD)


def _delta_call(qkv, zg, zba, coef, dnw, s0, prev, layer, bsz, t, tt, group):
    nb = SUBLANES
    tm = lambda wd: pl.BlockSpec((tt, nb, wd), lambda j, i: (i, j, 0))
    st_spec = _layer_spec(layer, (nb, DN_HEADS, DN_HD, DN_HD), lambda j, i: (j, 0, 0, 0))
    heads = pltpu.VMEM((DN_HEADS, tt * nb, DN_HD), F32)
    rows = pltpu.VMEM((tt * nb, BA_W), F32)
    n_in = 6
    return pl.pallas_call(
        functools.partial(_delta_body, tt=tt, group=group, n_alias=len(prev)),
        grid=(bsz // nb, t // tt),
        in_specs=[tm(QKV_W), tm(DN_WIDTH), tm(BA_W), _layer_full(layer, (2, BA_W)),
                  _layer_full(layer, (1, DN_HD)), st_spec] + [_ANY] * len(prev),
        out_specs=[tm(DN_WIDTH), st_spec],
        out_shape=[jax.ShapeDtypeStruct((t, bsz, DN_WIDTH), F32),
                   jax.ShapeDtypeStruct((DEPTH, bsz, DN_HEADS, DN_HD, DN_HD), F32)],
        input_output_aliases={n_in + k: 1 + k for k in range(len(prev))},
        scratch_shapes=[heads, heads, heads, rows, rows, heads],
        compiler_params=_params(2),
        name="delta",
    )(qkv, zg, zba, coef, dnw, s0, *prev)


def _hgrn_body(*refs, tt, n_tiles, group, n_alias):
    zh_ref, lb_ref, nw_ref, gm_ref, s0_ref = refs[:5]
    o_ref, s_ref, sbd, qsb, ksb, qeb, kdb, vb, elb, ob = refs[5 + n_alias:]
    ti = pl.program_id(1)
    nb = SUBLANES
    n = tt * nb
    mid = tt // 2 - 1
    sh_h = int(math.log2(HG_HD))

    @pl.when(ti == 0)
    def _():
        zero = jnp.zeros((HG_HD, HG_HD), F32)

        def init(b, c):
            for h in range(HG_HEADS):
                parts = [zero] * HG_HEADS
                parts[h] = s0_ref[b, h].T
                sbd[b, h * HG_HD:(h + 1) * HG_HD, :] = jnp.concatenate(parts, axis=1)
            return c
        for b in range(nb):
            init(b, 0)

    zh = zh_ref[...].reshape(n, ZH_W)
    lb = lb_ref[...]
    f = lb + (1.0 - lb) * _sigmoid_exp(zh[:, HG_WIDTH:2 * HG_WIDTH])
    q = _silu(zh[:, 0:HG_WIDTH]).reshape(tt, nb, HG_WIDTH)
    k = (1.0 - f).reshape(tt, nb, HG_WIDTH)
    bc = _dot_exact(jnp.log(f), _seq_tri(n, nb), _MM, x_first=False).reshape(tt, nb, HG_WIDTH)
    bm = bc[mid]
    bl = bc[tt - 1]
    half = HG_WIDTH // 2
    qs = q * jnp.exp(bc - bm[None])
    ks = k * jnp.exp(bm[None] - bc)
    for dst, val in ((qsb, qs), (ksb, ks), (qeb, qs * jnp.exp(bm)[None]), (kdb, ks * jnp.exp(bl - bm)[None])):
        val = val.reshape(n, HG_WIDTH)
        dst[0] = val[:, 0:half]
        dst[1] = val[:, half:HG_WIDTH]
    vb[0] = zh[:, 2 * HG_WIDTH:2 * HG_WIDTH + half]
    vb[1] = zh[:, 2 * HG_WIDTH + half:3 * HG_WIDTH]
    elb[...] = jnp.exp(bl)

    ri = lax.broadcasted_iota(jnp.int32, (HG_HEADS * tt, tt), 0)
    ci = lax.broadcasted_iota(jnp.int32, (HG_HEADS * tt, tt), 1)
    causal = ci <= (ri & (tt - 1))
    lane_head = lax.broadcasted_iota(jnp.int32, (tt, HG_WIDTH), 1) >> sh_h
    hmask = [jnp.where(lane_head == h, 1.0, 0.0).astype(F32) for h in range(HG_HEADS)]
    bd = ((lax.broadcasted_iota(jnp.int32, (HG_WIDTH, HG_WIDTH), 0) >> sh_h)
          == (lax.broadcasted_iota(jnp.int32, (HG_WIDTH, HG_WIDTH), 1) >> sh_h))

    def seq_group(bp, carry_):
        seqs = []
        for s in range(group):
            b = bp * group + s
            seq_rows = pl.ds(b, tt, stride=nb)
            seq = lambda ref: jnp.concatenate([ref[0, seq_rows, :], ref[1, seq_rows, :]], axis=1)
            seqs.append(dict(b=b, rows=seq_rows, qs=seq(qsb), ks=seq(ksb), qe=seq(qeb), kd=seq(kdb),
                             v=seq(vb), st=sbd[b], el=elb[pl.ds(b, 1), :]))
        for c in seqs:
            c['a'] = jnp.where(causal, _dot_nt(jnp.concatenate([c['qs'] * m for m in hmask], axis=0), c['ks']), 0.0)
            c['o'] = _dot_nt(c['qe'], c['st'])
            c['kv'] = _dot_tn(c['v'], c['kd'])
        for c in seqs:
            o = c['o']
            for h in range(HG_HEADS):
                o = o + hmask[h] * _dot(c['a'][h * tt:(h + 1) * tt], c['v'])
            ob[0, c['rows'], :] = o[:, 0:half]
            ob[1, c['rows'], :] = o[:, half:HG_WIDTH]
            sbd[c['b']] = c['st'] * c['el'] + jnp.where(bd, c['kv'], 0.0)
        return carry_

    lax.fori_loop(0, nb // group, seq_group, 0)

    o = jnp.concatenate([ob[0], ob[1]], axis=1)
    ms = _dot_exact(o * o, gm_ref[...], _MM) * (1.0 / HG_HD)
    o_ref[...] = (o * lax.rsqrt(ms + EPS) * nw_ref[...]
                  * _silu(zh[:, 3 * HG_WIDTH:4 * HG_WIDTH])).reshape(tt, nb, HG_WIDTH)

    @pl.when(ti == n_tiles - 1)
    def _():
        def fin(b, c):
            for h in range(HG_HEADS):
                blk = sbd[b, h * HG_HD:(h + 1) * HG_HD, :]
                s_ref[b, h] = blk[:, h * HG_HD:(h + 1) * HG_HD].T
            return c
        for b in range(nb):
            fin(b, 0)


def _hgrn_call(zh, lb, nw, gm, s0, prev, layer, bsz, t, tt, group):
    nb = SUBLANES
    n_tiles = t // tt
    st_spec = _layer_spec(layer, (nb, HG_HEADS, HG_HD, HG_HD), lambda j, i: (j, 0, 0, 0))
    tile = pltpu.VMEM((HG_WIDTH // LANES, tt * nb, LANES), F32)
    n_in = 5
    return pl.pallas_call(
        functools.partial(_hgrn_body, tt=tt, n_tiles=n_tiles, group=group, n_alias=len(prev)),
        grid=(bsz // nb, n_tiles),
        in_specs=[pl.BlockSpec((tt, nb, ZH_W), lambda j, i: (i, j, 0)),
                  _layer_full(layer, (1, HG_WIDTH)), _layer_full(layer, (1, HG_WIDTH)),
                  pl.BlockSpec((HG_WIDTH, HG_WIDTH), lambda j, i: (0, 0)), st_spec] + [_ANY] * len(prev),
        out_specs=[pl.BlockSpec((tt, nb, HG_WIDTH), lambda j, i: (i, j, 0)), st_spec],
        out_shape=[jax.ShapeDtypeStruct((t, bsz, HG_WIDTH), F32),
                   jax.ShapeDtypeStruct((DEPTH, bsz, HG_HEADS, HG_HD, HG_HD), F32)],
        input_output_aliases={n_in + k: 1 + k for k in range(len(prev))},
        scratch_shapes=[pltpu.VMEM((nb, HG_WIDTH, HG_WIDTH), F32),
                        tile, tile, tile, tile, tile, pltpu.VMEM((nb, HG_WIDTH), F32), tile],
        compiler_params=_params(2),
        name="hgrn",
    )(zh, lb, nw, gm, s0, *prev)


def _s5_body(u_ref, bh_ref, lam_ref, c_ref, d_ref, gw_ref, gb_ref, x0r_ref, x0i_ref,
             o_ref, xr_ref, xi_ref, *, bsz, steps):
    i = pl.program_id(0)

    @pl.when(i == 0)
    def _():
        xr_ref[...] = x0r_ref[...]
        xi_ref[...] = x0i_ref[...]

    u = u_ref[...].reshape(steps * bsz, SSM_WIDTH)
    bu = _dot(u, bh_ref[...])

    lr = jnp.broadcast_to(lam_ref[0:1, :], (SUBLANES, SSM_FLAT))
    li = jnp.broadcast_to(lam_ref[1:2, :], (SUBLANES, SSM_FLAT))

    n_rb = bsz // SUBLANES
    xr_all = xr_ref[...]
    xi_all = xi_ref[...]
    scanned = [[None] * n_rb for _ in range(steps)]
    last_r, last_i = [], []
    for rb in range(n_rb):
        xr = xr_all[rb * SUBLANES:(rb + 1) * SUBLANES]
        xi = xi_all[rb * SUBLANES:(rb + 1) * SUBLANES]
        for t in range(steps):
            r0 = t * bsz + rb * SUBLANES
            xr, xi = (lr * xr - li * xi + bu[r0:r0 + SUBLANES, 0:SSM_FLAT],
                      lr * xi + li * xr + bu[r0:r0 + SUBLANES, SSM_FLAT:2 * SSM_FLAT])
            scanned[t][rb] = jnp.concatenate([xr, xi], axis=1)
        last_r.append(xr)
        last_i.append(xi)
    xr_ref[...] = jnp.concatenate(last_r, axis=0)
    xi_ref[...] = jnp.concatenate(last_i, axis=0)
    xs = jnp.concatenate([scanned[t][rb] for t in range(steps) for rb in range(n_rb)], axis=0)

    y = _dot(xs, c_ref[...]) + d_ref[...] * u
    y = 0.5 * y * (1.0 + jnp.tanh(math.sqrt(2.0 / math.pi) * (y + 0.044715 * (y * y * y))))
    o_ref[...] = (y * _sigmoid(_dot(y, gw_ref[...]) + gb_ref[...])).reshape(steps, bsz, SSM_WIDTH)


def _s5_call(u, bh, lam, cm, d, gw, gb, x0r, x0i, layer, bsz, t, steps):
    rows = steps * bsz
    full = lambda shp: pl.BlockSpec(shp, lambda i: tuple(0 for _ in shp))
    lfull = lambda shp: _layer_full(layer, shp)
    return pl.pallas_call(
        functools.partial(_s5_body, bsz=bsz, steps=steps),
        grid=(t // steps,),
        in_specs=[pl.BlockSpec((steps, bsz, SSM_WIDTH), lambda i: (i, 0, 0)),
                  lfull((SSM_WIDTH, 2 * SSM_FLAT)),
                  lfull((2, SSM_FLAT)), lfull((2 * SSM_FLAT, SSM_WIDTH)), lfull((1, SSM_WIDTH)),
                  lfull((SSM_WIDTH, SSM_WIDTH)), lfull((1, SSM_WIDTH)),
                  full((bsz, SSM_FLAT)), full((bsz, SSM_FLAT))],
        out_specs=[pl.BlockSpec((steps, bsz, SSM_WIDTH), lambda i: (i, 0, 0)),
                   full((bsz, SSM_FLAT)), full((bsz, SSM_FLAT))],
        out_shape=[jax.ShapeDtypeStruct((t, bsz, SSM_WIDTH), F32),
                   jax.ShapeDtypeStruct((bsz, SSM_FLAT), F32),
                   jax.ShapeDtypeStruct((bsz, SSM_FLAT), F32)],
        compiler_params=_params(1),
        name="s5",
    )(u, bh, lam, cm, d, gw, gb, x0r, x0i)


def _tail_body(h_ref, oa_ref, ob_ref, oc_ref, wo_ref, nf_ref, wua_ref, wub_ref, cwa_ref, cwb_ref,
               wd_ref, csa_ref, csb_ref, p_ref, npl_ref, wg_ref, wp_ref,
               out_ref, cso_a_ref, cso_b_ref,
               hn, p3, car_a, car_b, *, tt, nb, n_ffc, ffc, n_sub):
    i = pl.program_id(1)
    j = pl.program_id(2)
    n = tt * nb
    hist = FF_CONV - 1
    st = tt // n_sub
    sr = st * nb
    hr = hist * nb

    @pl.when(j == 0)
    def _():
        h2 = (h_ref[...].reshape(n, D_MODEL) + _dot(oa_ref[...].reshape(n, DN_WIDTH), wo_ref[0:DN_WIDTH, :])
              + _dot(ob_ref[...].reshape(n, SSM_WIDTH), wo_ref[DN_WIDTH:DN_WIDTH + SSM_WIDTH, :])
              + _dot(oc_ref[...].reshape(n, HG_WIDTH), wo_ref[DN_WIDTH + SSM_WIDTH:D_MODEL, :]))
        out_ref[...] = h2.reshape(tt, nb, D_MODEL)
        hn[...] = _rms(h2, nf_ref[...]).astype(BF16)

    @pl.when(i == 0)
    def _():
        def cp(b, c):
            car_a[j, :, b, :] = csa_ref[b]
            car_b[j, :, b, :] = csb_ref[b]
            return c
        lax.fori_loop(0, nb, cp, 0)

    prev = [car_a[j].reshape(hr, ffc), car_b[j].reshape(hr, ffc)]
    wd = wd_ref[...]
    for sb in range(n_sub):
        hs = hn[sb * sr:(sb + 1) * sr, :]
        halves = []
        for idx, (wu, cw) in enumerate(((wua_ref, cwa_ref), (wub_ref, cwb_ref))):
            x = jnp.concatenate([prev[idx], jnp.dot(hs, wu[...], preferred_element_type=F32)], axis=0)
            w = cw[...]
            acc = w[0:1] * x[0:sr]
            for s in range(1, FF_CONV):
                acc = acc + w[s:s + 1] * x[s * nb:s * nb + sr]
            halves.append(acc)
            prev[idx] = x[sr:sr + hr]
        out_ref[sb * st:(sb + 1) * st] += _dot(_silu(halves[0]) * halves[1], wd).reshape(st, nb, D_MODEL)
    for car, cso, last in ((car_a, cso_a_ref, prev[0]), (car_b, cso_b_ref, prev[1])):
        last = last.reshape(hist, nb, ffc)
        car[j] = last
        cso[...] = last

    @pl.when(j == n_ffc - 1)
    def _():
        def cp(b, c):
            p3[:, b, :] = p_ref[b]
            return c
        lax.fori_loop(0, nb, cp, 0)
        h3 = out_ref[...].reshape(n, D_MODEL)
        gate = _sigmoid(_dot(_rms(h3, npl_ref[...]), wg_ref[...]))
        out_ref[...] = (h3 + gate * _dot(p3[...].reshape(n, PLE_DIM), wp_ref[...])).reshape(tt, nb, D_MODEL)


def _tail_call(h, oa, ob, oc, wo, nf, wu, cw, wd, cs, p, npl, wg, wp, layer, bsz, t, tt, nb, ffc, n_sub):
    n_ffc = FF_DIM // ffc
    n_t = t // tt
    hist = FF_CONV - 1
    tm = lambda wd_: pl.BlockSpec((tt, nb, wd_), lambda jb, i, j: (i, jb, 0))
    full = lambda shp: _layer_full(layer, shp, pipeline_mode=pl.Buffered(1))
    lyr = lambda blk, idx: _layer_spec(layer, blk, idx)
    wchunk = (lambda blk, idx: _layer_spec(layer, blk, idx, pipeline_mode=pl.Buffered(1))) if n_ffc == 1 else lyr
    cs_a = wchunk((nb, hist, ffc), lambda jb, i, j: (jb, 0, j))
    cs_b = wchunk((nb, hist, ffc), lambda jb, i, j: (jb, 0, n_ffc + j))
    cso = pl.BlockSpec((None, hist, nb, ffc), lambda jb, i, j: (i, 0, jb, j))
    scratch = [pltpu.VMEM((tt * nb, D_MODEL), BF16),
               pltpu.VMEM((tt, nb, PLE_DIM), F32),
               pltpu.VMEM((n_ffc, hist, nb, ffc), F32),
               pltpu.VMEM((n_ffc, hist, nb, ffc), F32)]
    return pl.pallas_call(
        functools.partial(_tail_body, tt=tt, nb=nb, n_ffc=n_ffc, ffc=ffc, n_sub=n_sub),
        grid=(bsz // nb, n_t, n_ffc),
        in_specs=[tm(D_MODEL), tm(DN_WIDTH), tm(SSM_WIDTH), tm(HG_WIDTH),
                  full((D_MODEL, D_MODEL)), full((1, D_MODEL)),
                  wchunk((D_MODEL, ffc), lambda jb, i, j: (0, j)),
                  wchunk((D_MODEL, ffc), lambda jb, i, j: (0, n_ffc + j)),
                  lyr((FF_CONV, ffc), lambda jb, i, j: (0, j)),
                  lyr((FF_CONV, ffc), lambda jb, i, j: (0, n_ffc + j)),
                  wchunk((ffc, D_MODEL), lambda jb, i, j: (j, 0)),
                  cs_a, cs_b,
                  lyr((nb, tt, PLE_DIM), lambda jb, i, j: (jb, i, 0)),
                  full((1, D_MODEL)), full((D_MODEL, D_MODEL)), full((PLE_DIM, D_MODEL))],
        out_specs=[tm(D_MODEL), cso, cso],
        out_shape=[jax.ShapeDtypeStruct((t, bsz, D_MODEL), F32),
                   jax.ShapeDtypeStruct((n_t, hist, bsz, FF_DIM), F32),
                   jax.ShapeDtypeStruct((n_t, hist, bsz, FF_DIM), F32)],
        scratch_shapes=scratch,
        compiler_params=_params(3),
        name="tail",
    )(h, oa, ob, oc, wo, nf, wu, wu, cw, cw, wd, cs, cs, p, npl, wg, wp)


def _final_body(h_ref, g_ref, o_ref, *, nb):
    g = g_ref[...]

    def one(b, c):
        o_ref[b] = _rms(h_ref[:, b, :], g)
        return c

    lax.fori_loop(0, nb, one, 0)


def _final_call(h, g, bsz, t, tt, nb):
    return pl.pallas_call(
        functools.partial(_final_body, nb=nb),
        grid=(bsz // nb, t // tt),
        in_specs=[pl.BlockSpec((tt, nb, D_MODEL), lambda j, i: (i, j, 0)),
                  pl.BlockSpec((1, D_MODEL), lambda j, i: (0, 0))],
        out_specs=pl.BlockSpec((nb, tt, D_MODEL), lambda j, i: (j, i, 0)),
        out_shape=jax.ShapeDtypeStruct((bsz, t, D_MODEL), F32),
        compiler_params=_params(2),
        name="final",
    )(h, g)


def _pack(prm):
    (norm_mix, w_in, dn_conv_w, dn_a_log, dn_dt_bias, dn_norm, ssm_lam_re, ssm_lam_im, ssm_log_step,
     ssm_b_re, ssm_b_im, ssm_c_re, ssm_c_im, ssm_d, ssm_glu_w, ssm_glu_b, lower_bounds, hg_norm, w_out,
     norm_ffn, ffn_w_up, ffn_conv_w, ffn_w_down, norm_ple, ple_w_gate, ple_w_proj) = prm
    zeros4 = jnp.zeros((DEPTH, DN_HEADS), F32)
    pad = jnp.zeros((DEPTH, BA_W - 2 * DN_HEADS), F32)
    coef = jnp.stack([jnp.concatenate([zeros4, -jnp.exp(dn_a_log.astype(F32)), pad], axis=-1),
                      jnp.concatenate([zeros4, dn_dt_bias.astype(F32), pad], axis=-1)], axis=1)

    lre = ssm_lam_re.astype(F32)
    lim = ssm_lam_im.astype(F32)
    delta = jnp.exp(ssm_log_step.astype(F32))[..., None]
    mag = jnp.exp(lre * delta)
    lbr = mag * jnp.cos(lim * delta)
    lbi = mag * jnp.sin(lim * delta)
    den = lre * lre + lim * lim
    fr = ((lbr - 1.0) * lre + lbi * lim) / den
    fi = (lbi * lre - (lbr - 1.0) * lim) / den
    bre = ssm_b_re.astype(F32)
    bim = ssm_b_im.astype(F32)
    bbr = fr[..., None] * bre - fi[..., None] * bim
    bbi = fr[..., None] * bim + fi[..., None] * bre
    eye_g = jnp.eye(SSM_GROUPS, dtype=F32)

    def bdiag_in(m):
        return jnp.einsum('dgph,gk->dghkp', m, eye_g).reshape(DEPTH, SSM_WIDTH, SSM_FLAT)

    def bdiag_out(m):
        return jnp.einsum('dghp,gk->dgpkh', m, eye_g).reshape(DEPTH, SSM_FLAT, SSM_WIDTH)

    bmat = jnp.concatenate([bdiag_in(bbr), bdiag_in(bbi)], axis=-1).astype(BF16)
    cmat = jnp.concatenate([bdiag_out(ssm_c_re.astype(F32)), -bdiag_out(ssm_c_im.astype(F32))],
                           axis=1).astype(BF16)
    lam = jnp.stack([lbr.reshape(DEPTH, SSM_FLAT), lbi.reshape(DEPTH, SSM_FLAT)], axis=1)

    gmat = jnp.kron(jnp.eye(HG_HEADS, dtype=F32), jnp.ones((HG_HD, HG_HD), F32)).astype(BF16)
    row = lambda a: a.reshape(DEPTH, 1, a.shape[-1])
    return dict(
        norm_mix=row(norm_mix), w_in=w_in, conv_w=dn_conv_w, coef=coef, dn_norm=row(dn_norm),
        bmat=bmat, lam=lam, cmat=cmat, ssm_d=row(ssm_d), glu_w=ssm_glu_w.astype(BF16), glu_b=row(ssm_glu_b),
        lb=row(lower_bounds), hg_norm=row(jnp.tile(hg_norm, (1, HG_HEADS))), gmat=gmat,
        w_out=w_out.astype(BF16), norm_ffn=row(norm_ffn), w_up=ffn_w_up.astype(BF16), ffn_conv_w=ffn_conv_w,
        w_down=ffn_w_down.astype(BF16), norm_ple=row(norm_ple), ple_gate=ple_w_gate.astype(BF16),
        ple_proj=ple_w_proj.astype(BF16))


def _tiles(bsz, t):
    if t >= DN_CHUNK:
        return dict(tok_tt=512 // bsz, tok_nb=bsz, dn_tt=DN_CHUNK, hg_tt=HG_CHUNK, s5_steps=512 // bsz,
                    ffc=FF_DIM, n_sub=1, fin_tt=128, dn_group=4, hg_group=8)
    return dict(tok_tt=t, tok_nb=512 // t, dn_tt=t, hg_tt=t, s5_steps=t, ffc=FF_DIM // 2, n_sub=4, fin_tt=t,
                dn_group=8, hg_group=8)


def _trunk(x, p, conv_qkv, delta, ssm_re, ssm_im, hgrn, conv_ffn, lp, norm_final):
    bsz, t, _ = x.shape
    tl = _tiles(bsz, t)
    sr_l, si_l, cf_l = [], [], []
    cq_prev, dn_prev, hg_prev = (), (), ()
    h = x
    for i in range(DEPTH):
        res = _proj_in(h, lp['norm_mix'], lp['w_in'], lp['conv_w'], conv_qkv, cq_prev, i, bsz, t,
                       tl['tok_tt'], tl['tok_nb'], from_btd=(i == 0))
        if i == 0:
            h = res[6]
        qkv, zg, zu, zh, zba, cq = res[:6]
        cq_prev = (cq,)
        o_a, dl = _delta_call(qkv, zg, zba, lp['coef'], lp['dn_norm'], delta, dn_prev, i, bsz, t,
                              tl['dn_tt'], tl['dn_group'])
        dn_prev = (dl,)
        o_b, sr, si = _s5_call(zu, lp['bmat'], lp['lam'], lp['cmat'], lp['ssm_d'], lp['glu_w'], lp['glu_b'],
                               ssm_re[i].reshape(bsz, SSM_FLAT), ssm_im[i].reshape(bsz, SSM_FLAT),
                               i, bsz, t, tl['s5_steps'])
        o_c, hg = _hgrn_call(zh, lp['lb'], lp['hg_norm'], lp['gmat'], hgrn, hg_prev, i, bsz, t, tl['hg_tt'],
                             tl['hg_group'])
        hg_prev = (hg,)
        h, cfa, cfb = _tail_call(h, o_a, o_b, o_c, lp['w_out'], lp['norm_ffn'], lp['w_up'], lp['ffn_conv_w'],
                                 lp['w_down'], conv_ffn, p, lp['norm_ple'], lp['ple_gate'],
                                 lp['ple_proj'], i, bsz, t, tl['tok_tt'], tl['tok_nb'], tl['ffc'], tl['n_sub'])
        sr_l.append(sr.reshape(bsz, SSM_GROUPS, SSM_STATE))
        si_l.append(si.reshape(bsz, SSM_GROUPS, SSM_STATE))
        cf_l.append(jnp.swapaxes(jnp.concatenate([cfa[-1], cfb[-1]], axis=-1), 0, 1))
    y = _final_call(h, norm_final, bsz, t, tl['fin_tt'], tl['tok_nb'])
    return (y, cq, dl, jnp.stack(sr_l), jnp.stack(si_l), hg, jnp.stack(cf_l))


def kernel(x_prompt, x_sample, p_prompt, p_sample, state_conv_qkv, state_delta, state_ssm_re, state_ssm_im, state_hgrn, state_conv_ffn, norm_mix, w_in, dn_conv_w, dn_a_log, dn_dt_bias, dn_norm, ssm_lam_re, ssm_lam_im, ssm_log_step, ssm_b_re, ssm_b_im, ssm_c_re, ssm_c_im, ssm_d, ssm_glu_w, ssm_glu_b, hg_lower, hg_norm, w_out, norm_ffn, ffn_w_up, ffn_conv_w, ffn_w_down, norm_ple, ple_w_gate, ple_w_proj, norm_final):
    lb_p = jax.nn.softmax(hg_lower.astype(F32), axis=0)
    lower_bounds = jnp.cumsum(lb_p, axis=0) - lb_p[0]
    prm = (norm_mix, w_in, dn_conv_w, dn_a_log, dn_dt_bias, dn_norm, ssm_lam_re, ssm_lam_im, ssm_log_step,
           ssm_b_re, ssm_b_im, ssm_c_re, ssm_c_im, ssm_d, ssm_glu_w, ssm_glu_b, lower_bounds, hg_norm, w_out,
           norm_ffn, ffn_w_up, ffn_conv_w, ffn_w_down, norm_ple, ple_w_gate, ple_w_proj)
    layers = _pack(prm)
    nf = norm_final.reshape(1, D_MODEL)

    bp = x_prompt.shape[0]
    z = lambda *shp: jnp.zeros((DEPTH, bp) + shp, F32)
    prompt = _trunk(x_prompt, p_prompt, z(DN_CONV - 1, QKV_W), z(DN_HEADS, DN_HD, DN_HD),
                    z(SSM_GROUPS, SSM_STATE), z(SSM_GROUPS, SSM_STATE), z(HG_HEADS, HG_HD, HG_HD),
                    z(FF_CONV - 1, 2 * FF_DIM), layers, nf)
    sample = _trunk(x_sample, p_sample, state_conv_qkv, state_delta, state_ssm_re, state_ssm_im, state_hgrn,
                    state_conv_ffn, layers, nf)
    return (prompt[0], sample[0]) + prompt[1:] + sample[1:]
```

```python
import functools
import math

import jax
import jax.numpy as jnp
from jax import lax
from jax.experimental import pallas as pl
from jax.experimental.pallas import tpu as pltpu

F32 = jnp.float32
BF16 = jnp.bfloat16

D_MODEL = 1024
DEPTH = 2
DN_HEADS = 4
DN_WIDTH = 512
DN_HD = 128
DN_CONV = 4
DN_CHUNK = 64
SSM_WIDTH = 256
SSM_GROUP = 16
SSM_GROUPS = 16
SSM_STATE = 64
SSM_FLAT = SSM_GROUPS * SSM_STATE
HG_WIDTH = 256
HG_HEADS = 4
HG_HD = 64
HG_CHUNK = 32
FF_DIM = 2816
FF_CONV = 3
PLE_DIM = 256
EPS = 1e-6

SUBLANES = 8
LANES = 128

QKV_W = 3 * DN_WIDTH
ZH_W = 4 * HG_WIDTH
BA_W = LANES
IN_PACKED = QKV_W + DN_WIDTH + SSM_WIDTH + ZH_W + BA_W
IN_RAW = QKV_W + DN_WIDTH + 2 * DN_HEADS + SSM_WIDTH + ZH_W
_Z_WIDTHS = (QKV_W, DN_WIDTH, SSM_WIDTH, ZH_W, BA_W)

VMEM_LIMIT = 56 * 1024 * 1024

_NT = (((1,), (1,)), ((), ()))
_TN = (((0,), (0,)), ((), ()))


def _dot(a, b):
    return jnp.dot(a.astype(BF16), b.astype(BF16), preferred_element_type=F32)


def _dot_nt(a, b):
    return lax.dot_general(a.astype(BF16), b.astype(BF16), _NT, preferred_element_type=F32)


def _dot_tn(a, b):
    return lax.dot_general(a.astype(BF16), b.astype(BF16), _TN, preferred_element_type=F32)


def _split3(x):
    hi = x.astype(BF16)
    r1 = x - hi.astype(F32)
    mid = r1.astype(BF16)
    lo = (r1 - mid.astype(F32)).astype(BF16)
    return hi, mid, lo


def _dot_exact(x, m, dims, x_first=True):
    if x_first:
        return sum(lax.dot_general(p, m, dims, preferred_element_type=F32) for p in _split3(x))
    return sum(lax.dot_general(m, p, dims, preferred_element_type=F32) for p in _split3(x))


_MM = (((1,), (0,)), ((), ()))


def _sigmoid_exp(x):
    return 1.0 / (1.0 + jnp.exp(-x))


def _sigmoid(x):
    return 0.5 * jnp.tanh(0.5 * x) + 0.5


def _silu(x):
    h = 0.5 * x
    return h + h * jnp.tanh(h)


def _layer_spec(layer, block, index, **kw):
    return pl.BlockSpec((None,) + tuple(block), lambda *g: (layer,) + tuple(index(*g)), **kw)


def _layer_full(layer, shape, **kw):
    return _layer_spec(layer, shape, lambda *g: (0,) * len(shape), **kw)


_ANY = pl.BlockSpec(memory_space=pl.ANY)


def _softplus(x):
    return jnp.maximum(x, 0.0) + jnp.log1p(jnp.exp(-jnp.abs(x)))


def _rms(x, g):
    ms = jnp.mean(x * x, axis=-1, keepdims=True)
    return x * lax.rsqrt(ms + EPS) * g


def _seq_tri(n, nb):
    r = lax.broadcasted_iota(jnp.int32, (n, n), 0)
    c = lax.broadcasted_iota(jnp.int32, (n, n), 1)
    return jnp.where(((r & (nb - 1)) == (c & (nb - 1))) & (c <= r), 1.0, 0.0).astype(BF16)


def _params(n_axes):
    return pltpu.CompilerParams(dimension_semantics=("arbitrary",) * n_axes, vmem_limit_bytes=VMEM_LIMIT)


def _proj_in_body(*refs, tt, nb, n_tiles, from_btd, n_alias):
    x_ref, g_ref, wraw_ref, cw_ref, cs_ref = refs[:5]
    outs = refs[5 + n_alias:]
    qkv_ref, zg_ref, zu_ref, zh_ref, zba_ref, cso_ref = outs[:6]
    buf, w_ref = outs[-2:]
    ti = pl.program_id(1)
    n = tt * nb
    hist = DN_CONV - 1

    @pl.when((pl.program_id(0) == 0) & (ti == 0))
    def _():
        o_ba = QKV_W + DN_WIDTH
        n_ba = 2 * DN_HEADS
        n_rest = SSM_WIDTH + ZH_W
        w_ref[:, 0:o_ba] = wraw_ref[:, 0:o_ba].astype(BF16)
        w_ref[:, o_ba:o_ba + n_rest] = wraw_ref[:, o_ba + n_ba:o_ba + n_ba + n_rest].astype(BF16)
        w_ref[:, o_ba + n_rest:IN_PACKED] = jnp.concatenate(
            [wraw_ref[:, o_ba:o_ba + n_ba], jnp.zeros((D_MODEL, BA_W - n_ba), F32)], axis=1).astype(BF16)
    if from_btd:
        h0 = outs[6]

        def cp(b, c):
            h0[:, b, :] = x_ref[b]
            return c

        lax.fori_loop(0, nb, cp, 0)
        x = h0[...].reshape(n, D_MODEL)
    else:
        x = x_ref[...].reshape(n, D_MODEL)
    xn = _rms(x, g_ref[...]).astype(BF16)

    @pl.when(ti == 0)
    def _():
        def cp(b, c):
            buf[:, b, :] = cs_ref[b]
            return c
        lax.fori_loop(0, nb, cp, 0)

    n_sub = 4
    st = tt // n_sub
    sr = st * nb
    hr = hist * nb
    cw = cw_ref[...]
    prev = buf[...].reshape(hr, QKV_W)
    for sb in range(n_sub):
        xs = xn[sb * sr:(sb + 1) * sr]
        ts = slice(sb * st, (sb + 1) * st)
        xcat = jnp.concatenate([prev, jnp.dot(xs, w_ref[:, 0:QKV_W], preferred_element_type=F32)], axis=0)
        c0 = QKV_W
        for ref, width in zip((zg_ref, zu_ref, zh_ref, zba_ref), _Z_WIDTHS[1:]):
            ref[ts] = jnp.dot(xs, w_ref[:, c0:c0 + width], preferred_element_type=F32).reshape(st, nb, width)
            c0 += width
        y = cw[0:1] * xcat[0:sr]
        for j in range(1, DN_CONV):
            y = y + cw[j:j + 1] * xcat[j * nb:j * nb + sr]
        prev = xcat[sr:sr + hr]
        y = _silu(y)
        for h in range(DN_HEADS):
            qs = slice(h * DN_HD, (h + 1) * DN_HD)
            ks = slice(DN_WIDTH + h * DN_HD, DN_WIDTH + (h + 1) * DN_HD)
            qh = y[:, qs]
            kh = y[:, ks]
            qn = qh * (lax.rsqrt(jnp.sum(qh * qh, axis=-1, keepdims=True) + EPS) * DN_HD ** -0.5)
            kn = kh * lax.rsqrt(jnp.sum(kh * kh, axis=-1, keepdims=True) + EPS)
            qkv_ref[ts, :, qs] = qn.reshape(st, nb, DN_HD)
            qkv_ref[ts, :, ks] = kn.reshape(st, nb, DN_HD)
        qkv_ref[ts, :, 2 * DN_WIDTH:QKV_W] = y[:, 2 * DN_WIDTH:QKV_W].reshape(st, nb, DN_WIDTH)
    buf[...] = prev.reshape(hist, nb, QKV_W)

    @pl.when(ti == n_tiles - 1)
    def _():
        def cp(b, c):
            cso_ref[b] = buf[:, b, :]
            return c
        lax.fori_loop(0, nb, cp, 0)


def _proj_in(x, g, w, cw, cs, prev, layer, bsz, t, tt, nb, from_btd):
    n_tiles = t // tt
    hist = DN_CONV - 1
    tm = lambda wd: pl.BlockSpec((tt, nb, wd), lambda j, i: (i, j, 0))
    x_spec = pl.BlockSpec((nb, tt, D_MODEL), lambda j, i: (j, i, 0)) if from_btd else tm(D_MODEL)
    cs_spec = _layer_spec(layer, (nb, hist, QKV_W), lambda j, i: (j, 0, 0))
    out_shape = [jax.ShapeDtypeStruct((t, bsz, wd), F32) for wd in _Z_WIDTHS]
    out_specs = [tm(wd) for wd in _Z_WIDTHS]
    out_shape.append(jax.ShapeDtypeStruct((DEPTH, bsz, hist, QKV_W), F32))
    out_specs.append(cs_spec)
    if from_btd:
        out_shape.append(jax.ShapeDtypeStruct((t, bsz, D_MODEL), F32))
        out_specs.append(tm(D_MODEL))
    n_in = 5
    return pl.pallas_call(
        functools.partial(_proj_in_body, tt=tt, nb=nb, n_tiles=n_tiles, from_btd=from_btd, n_alias=len(prev)),
        grid=(bsz // nb, n_tiles),
        in_specs=[x_spec, _layer_full(layer, (1, D_MODEL)),
                  _layer_full(layer, (D_MODEL, IN_RAW), pipeline_mode=pl.Buffered(1)),
                  _layer_full(layer, (DN_CONV, QKV_W)), cs_spec] + [_ANY] * len(prev),
        out_specs=out_specs,
        out_shape=out_shape,
        input_output_aliases={n_in + k: 5 + k for k in range(len(prev))},
        scratch_shapes=[pltpu.VMEM((hist, nb, QKV_W), F32), pltpu.VMEM((D_MODEL, IN_PACKED), BF16)],
        compiler_params=_params(2),
        name="proj_in",
    )(x, g, w, cw, cs, *prev)


def _delta_body(*refs, tt, group, n_alias):
    qkv_ref, zg_ref, zba_ref, coef_ref, dnw_ref, s0_ref = refs[:6]
    o_ref, s_ref, qb, kb, vb, pb, gcb, ob = refs[6 + n_alias:]
    ti = pl.program_id(1)
    nb = SUBLANES
    n = tt * nb
    levels = int(math.log2(tt))

    @pl.when(ti == 0)
    def _():
        s_ref[...] = s0_ref[...]

    for h in range(DN_HEADS):
        for dst, off in ((qb, 0), (kb, DN_WIDTH), (vb, 2 * DN_WIDTH)):
            dst[h] = qkv_ref[:, :, off + h * DN_HD:off + (h + 1) * DN_HD].reshape(n, DN_HD)

    zba = zba_ref[...].reshape(n, BA_W)
    coef = coef_ref[...]
    lane = lax.broadcasted_iota(jnp.int32, zba.shape, 1)
    g = coef[0:1] * _softplus(zba + coef[1:2])
    pb[...] = jnp.where(lane < DN_HEADS, _sigmoid(zba), g)
    gcb[...] = _dot_exact(g, _seq_tri(n, nb), _MM, x_first=False)

    ri = lax.broadcasted_iota(jnp.int32, (tt, tt), 0)
    ci = lax.broadcasted_iota(jnp.int32, (tt, tt), 1)
    causal = ci <= ri
    strict = ci < ri
    eye = jnp.where(ci == ri, 1.0, 0.0).astype(F32)
    sel = jnp.where(lax.broadcasted_iota(jnp.int32, (SUBLANES, BA_W), 0)
                    == lax.broadcasted_iota(jnp.int32, (SUBLANES, BA_W), 1), 1.0, 0.0).astype(BF16)

    def seq_group(bp, carry_):
        chains = []
        for s in range(group):
            b = bp * group + s
            seq_rows = pl.ds(b, tt, stride=nb)
            pbv = pb[seq_rows, :]
            gcv = gcb[seq_rows, :]
            gct = _dot_exact(gcv, sel, _NT, x_first=False)
            for h in range(DN_HEADS):
                chains.append(dict(b=b, h=h, rows=seq_rows, q=qb[h, seq_rows, :], k=kb[h, seq_rows, :],
                                   v=vb[h, seq_rows, :],
                                   beta=pbv[:, h:h + 1], gcol=gcv[:, DN_HEADS + h:DN_HEADS + h + 1],
                                   grow=gct[DN_HEADS + h:DN_HEADS + h + 1, :]))
        for c in chains:
            c['dec'] = jnp.where(causal, jnp.exp(jnp.where(causal, c['gcol'] - c['grow'], 0.0)), 0.0)
            c['qkk'] = _dot_nt(jnp.concatenate([c['q'], c['k']], axis=0), c['k'])
        for c in chains:
            c['p'] = -jnp.where(strict, c['beta'] * c['qkk'][tt:2 * tt] * c['dec'], 0.0)
            c['t'] = eye + c['p']
        for _ in range(levels - 1):
            for c in chains:
                c['p'] = _dot(c['p'], c['p'])
            for c in chains:
                c['t'] = c['t'] + _dot(c['p'], c['t'])
        for c in chains:
            eg = jnp.exp(c['gcol'])
            rhs = jnp.concatenate([c['beta'] * c['v'], (c['beta'] * eg) * c['k']], axis=1)
            c['uw'] = _dot(c['t'], rhs)
            c['qd'] = c['q'] * eg
            c['glast'] = c['grow'][:, tt - 1:tt]
            c['kd'] = c['k'] * jnp.exp(c['glast'] - c['gcol'])
            c['s'] = s_ref[c['b'], c['h']]
        for c in chains:
            c['r'] = _dot(jnp.concatenate([c['uw'][:, DN_HD:2 * DN_HD], c['qd']], axis=0), c['s'])
        for c in chains:
            c['vn'] = c['uw'][:, 0:DN_HD] - c['r'][0:tt]
        for c in chains:
            ob[c['h'], c['rows'], :] = c['r'][tt:2 * tt] + _dot(c['qkk'][0:tt] * c['dec'], c['vn'])
            s_ref[c['b'], c['h']] = c['s'] * jnp.exp(c['glast']) + _dot_tn(c['kd'], c['vn'])
        return carry_

    lax.fori_loop(0, nb // group, seq_group, 0)

    gate = zg_ref[...].reshape(n, DN_WIDTH)
    dnw = dnw_ref[...]
    for h in range(DN_HEADS):
        hs = slice(h * DN_HD, (h + 1) * DN_HD)
        o_ref[:, :, hs] = (_rms(ob[h], dnw) * _silu(gate[:, hs])).reshape(tt, nb, DN_HD)


def _delta_call(qkv, zg, zba, coef, dnw, s0, prev, layer, bsz, t, tt, group):
    nb = SUBLANES
    tm = lambda wd: pl.BlockSpec((tt, nb, wd), lambda j, i: (i, j, 0))
    st_spec = _layer_spec(layer, (nb, DN_HEADS, DN_HD, DN_HD), lambda j, i: (j, 0, 0, 0))
    heads = pltpu.VMEM((DN_HEADS, tt * nb, DN_HD), F32)
    rows = pltpu.VMEM((tt * nb, BA_W), F32)
    n_in = 6
    return pl.pallas_call(
        functools.partial(_delta_body, tt=tt, group=group, n_alias=len(prev)),
        grid=(bsz // nb, t // tt),
        in_specs=[tm(QKV_W), tm(DN_WIDTH), tm(BA_W), _layer_full(layer, (2, BA_W)),
                  _layer_full(layer, (1, DN_HD)), st_spec] + [_ANY] * len(prev),
        out_specs=[tm(DN_WIDTH), st_spec],
        out_shape=[jax.ShapeDtypeStruct((t, bsz, DN_WIDTH), F32),
                   jax.ShapeDtypeStruct((DEPTH, bsz, DN_HEADS, DN_HD, DN_HD), F32)],
        input_output_aliases={n_in + k: 1 + k for k in range(len(prev))},
        scratch_shapes=[heads, heads, heads, rows, rows, heads],
        compiler_params=_params(2),
        name="delta",
    )(qkv, zg, zba, coef, dnw, s0, *prev)


def _hgrn_body(*refs, tt, n_tiles, group, n_alias):
    zh_ref, lb_ref, nw_ref, gm_ref, s0_ref = refs[:5]
    o_ref, s_ref, sbd, qsb, ksb, qeb, kdb, vb, elb, ob = refs[5 + n_alias:]
    ti = pl.program_id(1)
    nb = SUBLANES
    n = tt * nb
    mid = tt // 2 - 1
    sh_h = int(math.log2(HG_HD))

    @pl.when(ti == 0)
    def _():
        zero = jnp.zeros((HG_HD, HG_HD), F32)

        def init(b, c):
            for h in range(HG_HEADS):
                parts = [zero] * HG_HEADS
                parts[h] = s0_ref[b, h].T
                sbd[b, h * HG_HD:(h + 1) * HG_HD, :] = jnp.concatenate(parts, axis=1)
            return c
        for b in range(nb):
            init(b, 0)

    zh = zh_ref[...].reshape(n, ZH_W)
    lb = lb_ref[...]
    f = lb + (1.0 - lb) * _sigmoid_exp(zh[:, HG_WIDTH:2 * HG_WIDTH])
    q = _silu(zh[:, 0:HG_WIDTH]).reshape(tt, nb, HG_WIDTH)
    k = (1.0 - f).reshape(tt, nb, HG_WIDTH)
    bc = _dot_exact(jnp.log(f), _seq_tri(n, nb), _MM, x_first=False).reshape(tt, nb, HG_WIDTH)
    bm = bc[mid]
    bl = bc[tt - 1]
    half = HG_WIDTH // 2
    qs = q * jnp.exp(bc - bm[None])
    ks = k * jnp.exp(bm[None] - bc)
    for dst, val in ((qsb, qs), (ksb, ks), (qeb, qs * jnp.exp(bm)[None]), (kdb, ks * jnp.exp(bl - bm)[None])):
        val = val.reshape(n, HG_WIDTH)
        dst[0] = val[:, 0:half]
        dst[1] = val[:, half:HG_WIDTH]
    vb[0] = zh[:, 2 * HG_WIDTH:2 * HG_WIDTH + half]
    vb[1] = zh[:, 2 * HG_WIDTH + half:3 * HG_WIDTH]
    elb[...] = jnp.exp(bl)

    ri = lax.broadcasted_iota(jnp.int32, (HG_HEADS * tt, tt), 0)
    ci = lax.broadcasted_iota(jnp.int32, (HG_HEADS * tt, tt), 1)
    causal = ci <= (ri & (tt - 1))
    lane_head = lax.broadcasted_iota(jnp.int32, (tt, HG_WIDTH), 1) >> sh_h
    hmask = [jnp.where(lane_head == h, 1.0, 0.0).astype(F32) for h in range(HG_HEADS)]
    bd = ((lax.broadcasted_iota(jnp.int32, (HG_WIDTH, HG_WIDTH), 0) >> sh_h)
          == (lax.broadcasted_iota(jnp.int32, (HG_WIDTH, HG_WIDTH), 1) >> sh_h))

    def seq_group(bp, carry_):
        seqs = []
        for s in range(group):
            b = bp * group + s
            seq_rows = pl.ds(b, tt, stride=nb)
            seq = lambda ref: jnp.concatenate([ref[0, seq_rows, :], ref[1, seq_rows, :]], axis=1)
            seqs.append(dict(b=b, rows=seq_rows, qs=seq(qsb), ks=seq(ksb), qe=seq(qeb), kd=seq(kdb),
                             v=seq(vb), st=sbd[b], el=elb[pl.ds(b, 1), :]))
        for c in seqs:
            c['a'] = jnp.where(causal, _dot_nt(jnp.concatenate([c['qs'] * m for m in hmask], axis=0), c['ks']), 0.0)
            c['o'] = _dot_nt(c['qe'], c['st'])
            c['kv'] = _dot_tn(c['v'], c['kd'])
        for c in seqs:
            o = c['o']
            for h in range(HG_HEADS):
                o = o + hmask[h] * _dot(c['a'][h * tt:(h + 1) * tt], c['v'])
            ob[0, c['rows'], :] = o[:, 0:half]
            ob[1, c['rows'], :] = o[:, half:HG_WIDTH]
            sbd[c['b']] = c['st'] * c['el'] + jnp.where(bd, c['kv'], 0.0)
        return carry_

    lax.fori_loop(0, nb // group, seq_group, 0)

    o = jnp.concatenate([ob[0], ob[1]], axis=1)
    ms = _dot_exact(o * o, gm_ref[...], _MM) * (1.0 / HG_HD)
    o_ref[...] = (o * lax.rsqrt(ms + EPS) * nw_ref[...]
                  * _silu(zh[:, 3 * HG_WIDTH:4 * HG_WIDTH])).reshape(tt, nb, HG_WIDTH)

    @pl.when(ti == n_tiles - 1)
    def _():
        def fin(b, c):
            for h in range(HG_HEADS):
                blk = sbd[b, h * HG_HD:(h + 1) * HG_HD, :]
                s_ref[b, h] = blk[:, h * HG_HD:(h + 1) * HG_HD].T
            return c
        for b in range(nb):
            fin(b, 0)


def _hgrn_call(zh, lb, nw, gm, s0, prev, layer, bsz, t, tt, group):
    nb = SUBLANES
    n_tiles = t // tt
    st_spec = _layer_spec(layer, (nb, HG_HEADS, HG_HD, HG_HD), lambda j, i: (j, 0, 0, 0))
    tile = pltpu.VMEM((HG_WIDTH // LANES, tt * nb, LANES), F32)
    n_in = 5
    return pl.pallas_call(
        functools.partial(_hgrn_body, tt=tt, n_tiles=n_tiles, group=group, n_alias=len(prev)),
        grid=(bsz // nb, n_tiles),
        in_specs=[pl.BlockSpec((tt, nb, ZH_W), lambda j, i: (i, j, 0)),
                  _layer_full(layer, (1, HG_WIDTH)), _layer_full(layer, (1, HG_WIDTH)),
                  pl.BlockSpec((HG_WIDTH, HG_WIDTH), lambda j, i: (0, 0)), st_spec] + [_ANY] * len(prev),
        out_specs=[pl.BlockSpec((tt, nb, HG_WIDTH), lambda j, i: (i, j, 0)), st_spec],
        out_shape=[jax.ShapeDtypeStruct((t, bsz, HG_WIDTH), F32),
                   jax.ShapeDtypeStruct((DEPTH, bsz, HG_HEADS, HG_HD, HG_HD), F32)],
        input_output_aliases={n_in + k: 1 + k for k in range(len(prev))},
        scratch_shapes=[pltpu.VMEM((nb, HG_WIDTH, HG_WIDTH), F32),
                        tile, tile, tile, tile, tile, pltpu.VMEM((nb, HG_WIDTH), F32), tile],
        compiler_params=_params(2),
        name="hgrn",
    )(zh, lb, nw, gm, s0, *prev)


def _s5_body(u_ref, bh_ref, lam_ref, c_ref, d_ref, gw_ref, gb_ref, x0r_ref, x0i_ref,
             o_ref, xr_ref, xi_ref, *, bsz, steps):
    i = pl.program_id(0)

    @pl.when(i == 0)
    def _():
        xr_ref[...] = x0r_ref[...]
        xi_ref[...] = x0i_ref[...]

    u = u_ref[...].reshape(steps * bsz, SSM_WIDTH)
    bu = _dot(u, bh_ref[...])

    lr = jnp.broadcast_to(lam_ref[0:1, :], (SUBLANES, SSM_FLAT))
    li = jnp.broadcast_to(lam_ref[1:2, :], (SUBLANES, SSM_FLAT))

    n_rb = bsz // SUBLANES
    xr_all = xr_ref[...]
    xi_all = xi_ref[...]
    scanned = [[None] * n_rb for _ in range(steps)]
    last_r, last_i = [], []
    for rb in range(n_rb):
        xr = xr_all[rb * SUBLANES:(rb + 1) * SUBLANES]
        xi = xi_all[rb * SUBLANES:(rb + 1) * SUBLANES]
        for t in range(steps):
            r0 = t * bsz + rb * SUBLANES
            xr, xi = (lr * xr - li * xi + bu[r0:r0 + SUBLANES, 0:SSM_FLAT],
                      lr * xi + li * xr + bu[r0:r0 + SUBLANES, SSM_FLAT:2 * SSM_FLAT])
            scanned[t][rb] = jnp.concatenate([xr, xi], axis=1)
        last_r.append(xr)
        last_i.append(xi)
    xr_ref[...] = jnp.concatenate(last_r, axis=0)
    xi_ref[...] = jnp.concatenate(last_i, axis=0)
    xs = jnp.concatenate([scanned[t][rb] for t in range(steps) for rb in range(n_rb)], axis=0)

    y = _dot(xs, c_ref[...]) + d_ref[...] * u
    y = 0.5 * y * (1.0 + jnp.tanh(math.sqrt(2.0 / math.pi) * (y + 0.044715 * (y * y * y))))
    o_ref[...] = (y * _sigmoid(_dot(y, gw_ref[...]) + gb_ref[...])).reshape(steps, bsz, SSM_WIDTH)


def _s5_call(u, bh, lam, cm, d, gw, gb, x0r, x0i, layer, bsz, t, steps):
    rows = steps * bsz
    full = lambda shp: pl.BlockSpec(shp, lambda i: tuple(0 for _ in shp))
    lfull = lambda shp: _layer_full(layer, shp)
    return pl.pallas_call(
        functools.partial(_s5_body, bsz=bsz, steps=steps),
        grid=(t // steps,),
        in_specs=[pl.BlockSpec((steps, bsz, SSM_WIDTH), lambda i: (i, 0, 0)),
                  lfull((SSM_WIDTH, 2 * SSM_FLAT)),
                  lfull((2, SSM_FLAT)), lfull((2 * SSM_FLAT, SSM_WIDTH)), lfull((1, SSM_WIDTH)),
                  lfull((SSM_WIDTH, SSM_WIDTH)), lfull((1, SSM_WIDTH)),
                  full((bsz, SSM_FLAT)), full((bsz, SSM_FLAT))],
        out_specs=[pl.BlockSpec((steps, bsz, SSM_WIDTH), lambda i: (i, 0, 0)),
                   full((bsz, SSM_FLAT)), full((bsz, SSM_FLAT))],
        out_shape=[jax.ShapeDtypeStruct((t, bsz, SSM_WIDTH), F32),
                   jax.ShapeDtypeStruct((bsz, SSM_FLAT), F32),
                   jax.ShapeDtypeStruct((bsz, SSM_FLAT), F32)],
        compiler_params=_params(1),
        name="s5",
    )(u, bh, lam, cm, d, gw, gb, x0r, x0i)


def _tail_body(*refs, tt, nb, n_ffc, ffc, n_sub, emit_final):
    (h_ref, oa_ref, ob_ref, oc_ref, wo_ref, nf_ref, wua_ref, wub_ref, cwa_ref, cwb_ref,
     wd_ref, csa_ref, csb_ref, p_ref, npl_ref, wg_ref, wp_ref) = refs[:17]
    rest = refs[17:]
    if emit_final:
        fin_ref, out_ref, cso_a_ref, cso_b_ref, y_ref, hn, p3, car_a, car_b, ys = rest
    else:
        out_ref, cso_a_ref, cso_b_ref, hn, p3, car_a, car_b = rest
    i = pl.program_id(1)
    j = pl.program_id(2)
    n = tt * nb
    hist = FF_CONV - 1
    st = tt // n_sub
    sr = st * nb
    hr = hist * nb

    @pl.when(j == 0)
    def _():
        h2 = (h_ref[...].reshape(n, D_MODEL) + _dot(oa_ref[...].reshape(n, DN_WIDTH), wo_ref[0:DN_WIDTH, :])
              + _dot(ob_ref[...].reshape(n, SSM_WIDTH), wo_ref[DN_WIDTH:DN_WIDTH + SSM_WIDTH, :])
              + _dot(oc_ref[...].reshape(n, HG_WIDTH), wo_ref[DN_WIDTH + SSM_WIDTH:D_MODEL, :]))
        out_ref[...] = h2.reshape(tt, nb, D_MODEL)
        hn[...] = _rms(h2, nf_ref[...]).astype(BF16)

    @pl.when(i == 0)
    def _():
        def cp(b, c):
            car_a[j, :, b, :] = csa_ref[b]
            car_b[j, :, b, :] = csb_ref[b]
            return c
        lax.fori_loop(0, nb, cp, 0)

    prev = [car_a[j].reshape(hr, ffc), car_b[j].reshape(hr, ffc)]
    wd = wd_ref[...]
    for sb in range(n_sub):
        hs = hn[sb * sr:(sb + 1) * sr, :]
        halves = []
        for idx, (wu, cw) in enumerate(((wua_ref, cwa_ref), (wub_ref, cwb_ref))):
            x = jnp.concatenate([prev[idx], jnp.dot(hs, wu[...], preferred_element_type=F32)], axis=0)
            w = cw[...]
            acc = w[0:1] * x[0:sr]
            for s in range(1, FF_CONV):
                acc = acc + w[s:s + 1] * x[s * nb:s * nb + sr]
            halves.append(acc)
            prev[idx] = x[sr:sr + hr]
        out_ref[sb * st:(sb + 1) * st] += _dot(_silu(halves[0]) * halves[1], wd).reshape(st, nb, D_MODEL)
    for car, cso, last in ((car_a, cso_a_ref, prev[0]), (car_b, cso_b_ref, prev[1])):
        last = last.reshape(hist, nb, ffc)
        car[j] = last
        cso[...] = last

    @pl.when(j == n_ffc - 1)
    def _():
        def cp(b, c):
            p3[:, b, :] = p_ref[b]
            return c
        lax.fori_loop(0, nb, cp, 0)
        h3 = out_ref[...].reshape(n, D_MODEL)
        gate = _sigmoid(_dot(_rms(h3, npl_ref[...]), wg_ref[...]))
        h4 = h3 + gate * _dot(p3[...].reshape(n, PLE_DIM), wp_ref[...])
        out_ref[...] = h4.reshape(tt, nb, D_MODEL)
        if emit_final:
            yv = _rms(h4, fin_ref[...])
            n_slab = D_MODEL // LANES
            for s in range(n_slab):
                ys[s] = yv[:, s * LANES:(s + 1) * LANES]

            def seq(b, c):
                rows = pl.ds(b, tt, stride=nb)
                y_ref[b] = jnp.concatenate([ys[s, rows, :] for s in range(n_slab)], axis=1)
                return c
            lax.fori_loop(0, nb, seq, 0)


def _tail_call(h, oa, ob, oc, wo, nf, wu, cw, wd, cs, p, npl, wg, wp, fin, layer, bsz, t, tt, nb, ffc, n_sub):
    n_ffc = FF_DIM // ffc
    n_t = t // tt
    hist = FF_CONV - 1
    tm = lambda wd_: pl.BlockSpec((tt, nb, wd_), lambda jb, i, j: (i, jb, 0))
    full = lambda shp: _layer_full(layer, shp, pipeline_mode=pl.Buffered(1))
    lyr = lambda blk, idx: _layer_spec(layer, blk, idx)
    wchunk = (lambda blk, idx: _layer_spec(layer, blk, idx, pipeline_mode=pl.Buffered(1))) if n_ffc == 1 else lyr
    cs_a = wchunk((nb, hist, ffc), lambda jb, i, j: (jb, 0, j))
    cs_b = wchunk((nb, hist, ffc), lambda jb, i, j: (jb, 0, n_ffc + j))
    cso = pl.BlockSpec((None, hist, nb, ffc), lambda jb, i, j: (i, 0, jb, j))
    scratch = [pltpu.VMEM((tt * nb, D_MODEL), BF16),
               pltpu.VMEM((tt, nb, PLE_DIM), F32),
               pltpu.VMEM((n_ffc, hist, nb, ffc), F32),
               pltpu.VMEM((n_ffc, hist, nb, ffc), F32)]
    emit_final = fin is not None
    out_specs = [tm(D_MODEL), cso, cso]
    out_shape = [jax.ShapeDtypeStruct((t, bsz, D_MODEL), F32),
                 jax.ShapeDtypeStruct((n_t, hist, bsz, FF_DIM), F32),
                 jax.ShapeDtypeStruct((n_t, hist, bsz, FF_DIM), F32)]
    extra_in, extra_specs = [], []
    if emit_final:
        extra_in = [fin]
        extra_specs = [pl.BlockSpec((1, D_MODEL), lambda jb, i, j: (0, 0))]
        out_specs.append(pl.BlockSpec((nb, tt, D_MODEL), lambda jb, i, j: (jb, i, 0)))
        out_shape.append(jax.ShapeDtypeStruct((bsz, t, D_MODEL), F32))
        scratch.append(pltpu.VMEM((D_MODEL // LANES, tt * nb, LANES), F32))
    return pl.pallas_call(
        functools.partial(_tail_body, tt=tt, nb=nb, n_ffc=n_ffc, ffc=ffc, n_sub=n_sub, emit_final=emit_final),
        grid=(bsz // nb, n_t, n_ffc),
        in_specs=[tm(D_MODEL), tm(DN_WIDTH), tm(SSM_WIDTH), tm(HG_WIDTH),
                  full((D_MODEL, D_MODEL)), full((1, D_MODEL)),
                  wchunk((D_MODEL, ffc), lambda jb, i, j: (0, j)),
                  wchunk((D_MODEL, ffc), lambda jb, i, j: (0, n_ffc + j)),
                  lyr((FF_CONV, ffc), lambda jb, i, j: (0, j)),
                  lyr((FF_CONV, ffc), lambda jb, i, j: (0, n_ffc + j)),
                  wchunk((ffc, D_MODEL), lambda jb, i, j: (j, 0)),
                  cs_a, cs_b,
                  lyr((nb, tt, PLE_DIM), lambda jb, i, j: (jb, i, 0)),
                  full((1, D_MODEL)), full((D_MODEL, D_MODEL)), full((PLE_DIM, D_MODEL))] + extra_specs,
        out_specs=out_specs,
        out_shape=out_shape,
        scratch_shapes=scratch,
        compiler_params=_params(3),
        name="tail",
    )(h, oa, ob, oc, wo, nf, wu, wu, cw, cw, wd, cs, cs, p, npl, wg, wp, *extra_in)


def _pack(prm):
    (norm_mix, w_in, dn_conv_w, dn_a_log, dn_dt_bias, dn_norm, ssm_lam_re, ssm_lam_im, ssm_log_step,
     ssm_b_re, ssm_b_im, ssm_c_re, ssm_c_im, ssm_d, ssm_glu_w, ssm_glu_b, lower_bounds, hg_norm, w_out,
     norm_ffn, ffn_w_up, ffn_conv_w, ffn_w_down, norm_ple, ple_w_gate, ple_w_proj) = prm
    zeros4 = jnp.zeros((DEPTH, DN_HEADS), F32)
    pad = jnp.zeros((DEPTH, BA_W - 2 * DN_HEADS), F32)
    coef = jnp.stack([jnp.concatenate([zeros4, -jnp.exp(dn_a_log.astype(F32)), pad], axis=-1),
                      jnp.concatenate([zeros4, dn_dt_bias.astype(F32), pad], axis=-1)], axis=1)

    lre = ssm_lam_re.astype(F32)
    lim = ssm_lam_im.astype(F32)
    delta = jnp.exp(ssm_log_step.astype(F32))[..., None]
    mag = jnp.exp(lre * delta)
    lbr = mag * jnp.cos(lim * delta)
    lbi = mag * jnp.sin(lim * delta)
    den = lre * lre + lim * lim
    fr = ((lbr - 1.0) * lre + lbi * lim) / den
    fi = (lbi * lre - (lbr - 1.0) * lim) / den
    bre = ssm_b_re.astype(F32)
    bim = ssm_b_im.astype(F32)
    bbr = fr[..., None] * bre - fi[..., None] * bim
    bbi = fr[..., None] * bim + fi[..., None] * bre
    eye_g = jnp.eye(SSM_GROUPS, dtype=F32)

    def bdiag_in(m):
        return jnp.einsum('dgph,gk->dghkp', m, eye_g).reshape(DEPTH, SSM_WIDTH, SSM_FLAT)

    def bdiag_out(m):
        return jnp.einsum('dghp,gk->dgpkh', m, eye_g).reshape(DEPTH, SSM_FLAT, SSM_WIDTH)

    bmat = jnp.concatenate([bdiag_in(bbr), bdiag_in(bbi)], axis=-1).astype(BF16)
    cmat = jnp.concatenate([bdiag_out(ssm_c_re.astype(F32)), -bdiag_out(ssm_c_im.astype(F32))],
                           axis=1).astype(BF16)
    lam = jnp.stack([lbr.reshape(DEPTH, SSM_FLAT), lbi.reshape(DEPTH, SSM_FLAT)], axis=1)

    gmat = jnp.kron(jnp.eye(HG_HEADS, dtype=F32), jnp.ones((HG_HD, HG_HD), F32)).astype(BF16)
    row = lambda a: a.reshape(DEPTH, 1, a.shape[-1])
    return dict(
        norm_mix=row(norm_mix), w_in=w_in, conv_w=dn_conv_w, coef=coef, dn_norm=row(dn_norm),
        bmat=bmat, lam=lam, cmat=cmat, ssm_d=row(ssm_d), glu_w=ssm_glu_w.astype(BF16), glu_b=row(ssm_glu_b),
        lb=row(lower_bounds), hg_norm=row(jnp.tile(hg_norm, (1, HG_HEADS))), gmat=gmat,
        w_out=w_out.astype(BF16), norm_ffn=row(norm_ffn), w_up=ffn_w_up.astype(BF16), ffn_conv_w=ffn_conv_w,
        w_down=ffn_w_down.astype(BF16), norm_ple=row(norm_ple), ple_gate=ple_w_gate.astype(BF16),
        ple_proj=ple_w_proj.astype(BF16))


def _tiles(bsz, t):
    if t >= DN_CHUNK:
        return dict(tok_tt=512 // bsz, tok_nb=bsz, tail_nb=bsz, dn_tt=DN_CHUNK, hg_tt=HG_CHUNK,
                    s5_steps=512 // bsz, ffc=FF_DIM, n_sub=1, dn_group=4, hg_group=8)
    return dict(tok_tt=t, tok_nb=512 // t, tail_nb=256 // t, dn_tt=t, hg_tt=t, s5_steps=t, ffc=FF_DIM // 2,
                n_sub=2, dn_group=8, hg_group=8)


def _trunk(x, p, conv_qkv, delta, ssm_re, ssm_im, hgrn, conv_ffn, lp, norm_final):
    bsz, t, _ = x.shape
    tl = _tiles(bsz, t)
    sr_l, si_l, cf_l = [], [], []
    cq_prev, dn_prev, hg_prev = (), (), ()
    h = x
    for i in range(DEPTH):
        res = _proj_in(h, lp['norm_mix'], lp['w_in'], lp['conv_w'], conv_qkv, cq_prev, i, bsz, t,
                       tl['tok_tt'], tl['tok_nb'], from_btd=(i == 0))
        if i == 0:
            h = res[6]
        qkv, zg, zu, zh, zba, cq = res[:6]
        cq_prev = (cq,)
        o_a, dl = _delta_call(qkv, zg, zba, lp['coef'], lp['dn_norm'], delta, dn_prev, i, bsz, t,
                              tl['dn_tt'], tl['dn_group'])
        dn_prev = (dl,)
        o_b, sr, si = _s5_call(zu, lp['bmat'], lp['lam'], lp['cmat'], lp['ssm_d'], lp['glu_w'], lp['glu_b'],
                               ssm_re[i].reshape(bsz, SSM_FLAT), ssm_im[i].reshape(bsz, SSM_FLAT),
                               i, bsz, t, tl['s5_steps'])
        o_c, hg = _hgrn_call(zh, lp['lb'], lp['hg_norm'], lp['gmat'], hgrn, hg_prev, i, bsz, t, tl['hg_tt'],
                             tl['hg_group'])
        hg_prev = (hg,)
        res = _tail_call(h, o_a, o_b, o_c, lp['w_out'], lp['norm_ffn'], lp['w_up'], lp['ffn_conv_w'],
                         lp['w_down'], conv_ffn, p, lp['norm_ple'], lp['ple_gate'], lp['ple_proj'],
                         norm_final if i == DEPTH - 1 else None,
                         i, bsz, t, tl['tok_tt'], tl['tail_nb'], tl['ffc'], tl['n_sub'])
        h, cfa, cfb = res[:3]
        sr_l.append(sr.reshape(bsz, SSM_GROUPS, SSM_STATE))
        si_l.append(si.reshape(bsz, SSM_GROUPS, SSM_STATE))
        cf_l.append(jnp.swapaxes(jnp.concatenate([cfa[-1], cfb[-1]], axis=-1), 0, 1))
    y = res[3]
    return (y, cq, dl, jnp.stack(sr_l), jnp.stack(si_l), hg, jnp.stack(cf_l))


def kernel(x_prompt, x_sample, p_prompt, p_sample, state_conv_qkv, state_delta, state_ssm_re, state_ssm_im, state_hgrn, state_conv_ffn, norm_mix, w_in, dn_conv_w, dn_a_log, dn_dt_bias, dn_norm, ssm_lam_re, ssm_lam_im, ssm_log_step, ssm_b_re, ssm_b_im, ssm_c_re, ssm_c_im, ssm_d, ssm_glu_w, ssm_glu_b, hg_lower, hg_norm, w_out, norm_ffn, ffn_w_up, ffn_conv_w, ffn_w_down, norm_ple, ple_w_gate, ple_w_proj, norm_final):
    lb_p = jax.nn.softmax(hg_lower.astype(F32), axis=0)
    lower_bounds = jnp.cumsum(lb_p, axis=0) - lb_p[0]
    prm = (norm_mix, w_in, dn_conv_w, dn_a_log, dn_dt_bias, dn_norm, ssm_lam_re, ssm_lam_im, ssm_log_step,
           ssm_b_re, ssm_b_im, ssm_c_re, ssm_c_im, ssm_d, ssm_glu_w, ssm_glu_b, lower_bounds, hg_norm, w_out,
           norm_ffn, ffn_w_up, ffn_conv_w, ffn_w_down, norm_ple, ple_w_gate, ple_w_proj)
    layers = _pack(prm)
    nf = norm_final.reshape(1, D_MODEL)

    bp = x_prompt.shape[0]
    z = lambda *shp: jnp.zeros((DEPTH, bp) + shp, F32)
    prompt = _trunk(x_prompt, p_prompt, z(DN_CONV - 1, QKV_W), z(DN_HEADS, DN_HD, DN_HD),
                    z(SSM_GROUPS, SSM_STATE), z(SSM_GROUPS, SSM_STATE), z(HG_HEADS, HG_HD, HG_HD),
                    z(FF_CONV - 1, 2 * FF_DIM), layers, nf)
    sample = _trunk(x_sample, p_sample, state_conv_qkv, state_delta, state_ssm_re, state_ssm_im, state_hgrn,
                    state_conv_ffn, layers, nf)
    return (prompt[0], sample[0]) + prompt[1:] + sample[1:]
```

```python
import functools
import math

import jax
import jax.numpy as jnp
from jax import lax
from jax.experimental import pallas as pl
from jax.experimental.pallas import tpu as pltpu

F32 = jnp.float32
BF16 = jnp.bfloat16

D_MODEL = 1024
DEPTH = 2
DN_HEADS = 4
DN_WIDTH = 512
DN_HD = 128
DN_CONV = 4
DN_CHUNK = 64
SSM_WIDTH = 256
SSM_GROUP = 16
SSM_GROUPS = 16
SSM_STATE = 64
SSM_FLAT = SSM_GROUPS * SSM_STATE
HG_WIDTH = 256
HG_HEADS = 4
HG_HD = 64
HG_CHUNK = 32
FF_DIM = 2816
FF_CONV = 3
PLE_DIM = 256
EPS = 1e-6

SUBLANES = 8
LANES = 128

QKV_W = 3 * DN_WIDTH
ZH_W = 4 * HG_WIDTH
BA_W = LANES
IN_PACKED = QKV_W + DN_WIDTH + SSM_WIDTH + ZH_W + BA_W
IN_RAW = QKV_W + DN_WIDTH + 2 * DN_HEADS + SSM_WIDTH + ZH_W
_Z_WIDTHS = (QKV_W, DN_WIDTH, SSM_WIDTH, ZH_W, BA_W)

VMEM_LIMIT = 56 * 1024 * 1024

_NT = (((1,), (1,)), ((), ()))
_TN = (((0,), (0,)), ((), ()))


def _dot(a, b):
    return jnp.dot(a.astype(BF16), b.astype(BF16), preferred_element_type=F32)


def _dot_nt(a, b):
    return lax.dot_general(a.astype(BF16), b.astype(BF16), _NT, preferred_element_type=F32)


def _dot_tn(a, b):
    return lax.dot_general(a.astype(BF16), b.astype(BF16), _TN, preferred_element_type=F32)


def _split3(x):
    hi = x.astype(BF16)
    r1 = x - hi.astype(F32)
    mid = r1.astype(BF16)
    lo = (r1 - mid.astype(F32)).astype(BF16)
    return hi, mid, lo


def _dot_exact(x, m, dims, x_first=True):
    if x_first:
        return sum(lax.dot_general(p, m, dims, preferred_element_type=F32) for p in _split3(x))
    return sum(lax.dot_general(m, p, dims, preferred_element_type=F32) for p in _split3(x))


_MM = (((1,), (0,)), ((), ()))


def _sigmoid_exp(x):
    return 1.0 / (1.0 + jnp.exp(-x))


def _sigmoid(x):
    return 0.5 * jnp.tanh(0.5 * x) + 0.5


def _silu(x):
    h = 0.5 * x
    return h + h * jnp.tanh(h)


def _layer_spec(layer, block, index, **kw):
    return pl.BlockSpec((None,) + tuple(block), lambda *g: (layer,) + tuple(index(*g)), **kw)


def _layer_full(layer, shape, **kw):
    return _layer_spec(layer, shape, lambda *g: (0,) * len(shape), **kw)


_ANY = pl.BlockSpec(memory_space=pl.ANY)


def _softplus(x):
    return jnp.maximum(x, 0.0) + jnp.log1p(jnp.exp(-jnp.abs(x)))


def _rms(x, g):
    ms = jnp.mean(x * x, axis=-1, keepdims=True)
    return x * lax.rsqrt(ms + EPS) * g


def _seq_tri(n, nb):
    r = lax.broadcasted_iota(jnp.int32, (n, n), 0)
    c = lax.broadcasted_iota(jnp.int32, (n, n), 1)
    return jnp.where(((r & (nb - 1)) == (c & (nb - 1))) & (c <= r), 1.0, 0.0).astype(BF16)


def _params(n_axes):
    return pltpu.CompilerParams(dimension_semantics=("arbitrary",) * n_axes, vmem_limit_bytes=VMEM_LIMIT)


def _proj_in_body(*refs, tt, nb, n_tiles, from_btd, n_alias):
    x_ref, g_ref, wraw_ref, cw_ref, cs_ref = refs[:5]
    outs = refs[5 + n_alias:]
    qkv_ref, zg_ref, zu_ref, zh_ref, zba_ref, cso_ref = outs[:6]
    buf, w_ref = outs[-2:]
    ti = pl.program_id(1)
    n = tt * nb
    hist = DN_CONV - 1

    @pl.when((pl.program_id(0) == 0) & (ti == 0))
    def _():
        o_ba = QKV_W + DN_WIDTH
        n_ba = 2 * DN_HEADS
        n_rest = SSM_WIDTH + ZH_W
        w_ref[:, 0:o_ba] = wraw_ref[:, 0:o_ba].astype(BF16)
        w_ref[:, o_ba:o_ba + n_rest] = wraw_ref[:, o_ba + n_ba:o_ba + n_ba + n_rest].astype(BF16)
        w_ref[:, o_ba + n_rest:IN_PACKED] = jnp.concatenate(
            [wraw_ref[:, o_ba:o_ba + n_ba], jnp.zeros((D_MODEL, BA_W - n_ba), F32)], axis=1).astype(BF16)
    if from_btd:
        h0 = outs[6]

        def cp(b, c):
            h0[:, b, :] = x_ref[b]
            return c

        lax.fori_loop(0, nb, cp, 0)
        x = h0[...].reshape(n, D_MODEL)
    else:
        x = x_ref[...].reshape(n, D_MODEL)
    xn = _rms(x, g_ref[...]).astype(BF16)

    @pl.when(ti == 0)
    def _():
        def cp(b, c):
            buf[:, b, :] = cs_ref[b]
            return c
        lax.fori_loop(0, nb, cp, 0)

    n_sub = 4
    st = tt // n_sub
    sr = st * nb
    hr = hist * nb
    cw = cw_ref[...]
    prev = buf[...].reshape(hr, QKV_W)
    for sb in range(n_sub):
        xs = xn[sb * sr:(sb + 1) * sr]
        ts = slice(sb * st, (sb + 1) * st)
        xcat = jnp.concatenate([prev, jnp.dot(xs, w_ref[:, 0:QKV_W], preferred_element_type=F32)], axis=0)
        c0 = QKV_W
        for ref, width in zip((zg_ref, zu_ref, zh_ref, zba_ref), _Z_WIDTHS[1:]):
            ref[ts] = jnp.dot(xs, w_ref[:, c0:c0 + width], preferred_element_type=F32).reshape(st, nb, width)
            c0 += width
        y = cw[0:1] * xcat[0:sr]
        for j in range(1, DN_CONV):
            y = y + cw[j:j + 1] * xcat[j * nb:j * nb + sr]
        prev = xcat[sr:sr + hr]
        y = _silu(y)
        for h in range(DN_HEADS):
            qs = slice(h * DN_HD, (h + 1) * DN_HD)
            ks = slice(DN_WIDTH + h * DN_HD, DN_WIDTH + (h + 1) * DN_HD)
            qh = y[:, qs]
            kh = y[:, ks]
            qn = qh * (lax.rsqrt(jnp.sum(qh * qh, axis=-1, keepdims=True) + EPS) * DN_HD ** -0.5)
            kn = kh * lax.rsqrt(jnp.sum(kh * kh, axis=-1, keepdims=True) + EPS)
            qkv_ref[ts, :, qs] = qn.reshape(st, nb, DN_HD)
            qkv_ref[ts, :, ks] = kn.reshape(st, nb, DN_HD)
        qkv_ref[ts, :, 2 * DN_WIDTH:QKV_W] = y[:, 2 * DN_WIDTH:QKV_W].reshape(st, nb, DN_WIDTH)
    buf[...] = prev.reshape(hist, nb, QKV_W)

    @pl.when(ti == n_tiles - 1)
    def _():
        def cp(b, c):
            cso_ref[b] = buf[:, b, :]
            return c
        lax.fori_loop(0, nb, cp, 0)


def _proj_in(x, g, w, cw, cs, prev, layer, bsz, t, tt, nb, from_btd):
    n_tiles = t // tt
    hist = DN_CONV - 1
    tm = lambda wd: pl.BlockSpec((tt, nb, wd), lambda j, i: (i, j, 0))
    x_spec = pl.BlockSpec((nb, tt, D_MODEL), lambda j, i: (j, i, 0)) if from_btd else tm(D_MODEL)
    cs_spec = _layer_spec(layer, (nb, hist, QKV_W), lambda j, i: (j, 0, 0))
    out_shape = [jax.ShapeDtypeStruct((t, bsz, wd), F32) for wd in _Z_WIDTHS]
    out_specs = [tm(wd) for wd in _Z_WIDTHS]
    out_shape.append(jax.ShapeDtypeStruct((DEPTH, bsz, hist, QKV_W), F32))
    out_specs.append(cs_spec)
    if from_btd:
        out_shape.append(jax.ShapeDtypeStruct((t, bsz, D_MODEL), F32))
        out_specs.append(tm(D_MODEL))
    n_in = 5
    return pl.pallas_call(
        functools.partial(_proj_in_body, tt=tt, nb=nb, n_tiles=n_tiles, from_btd=from_btd, n_alias=len(prev)),
        grid=(bsz // nb, n_tiles),
        in_specs=[x_spec, _layer_full(layer, (1, D_MODEL)),
                  _layer_full(layer, (D_MODEL, IN_RAW), pipeline_mode=pl.Buffered(1)),
                  _layer_full(layer, (DN_CONV, QKV_W)), cs_spec] + [_ANY] * len(prev),
        out_specs=out_specs,
        out_shape=out_shape,
        input_output_aliases={n_in + k: 5 + k for k in range(len(prev))},
        scratch_shapes=[pltpu.VMEM((hist, nb, QKV_W), F32), pltpu.VMEM((D_MODEL, IN_PACKED), BF16)],
        compiler_params=_params(2),
        name="proj_in",
    )(x, g, w, cw, cs, *prev)


def _delta_body(*refs, tt, group, n_alias):
    qkv_ref, zg_ref, zba_ref, coef_ref, dnw_ref, s0_ref = refs[:6]
    o_ref, s_ref, qb, kb, vb, pb, gcb, ob = refs[6 + n_alias:]
    ti = pl.program_id(1)
    nb = SUBLANES
    n = tt * nb
    levels = int(math.log2(tt))

    @pl.when(ti == 0)
    def _():
        s_ref[...] = s0_ref[...]

    for h in range(DN_HEADS):
        for dst, off in ((qb, 0), (kb, DN_WIDTH), (vb, 2 * DN_WIDTH)):
            dst[h] = qkv_ref[:, :, off + h * DN_HD:off + (h + 1) * DN_HD].reshape(n, DN_HD)

    zba = zba_ref[...].reshape(n, BA_W)
    coef = coef_ref[...]
    lane = lax.broadcasted_iota(jnp.int32, zba.shape, 1)
    g = coef[0:1] * _softplus(zba + coef[1:2])
    pb[...] = jnp.where(lane < DN_HEADS, _sigmoid(zba), g)
    gcb[...] = _dot_exact(g, _seq_tri(n, nb), _MM, x_first=False)

    ri = lax.broadcasted_iota(jnp.int32, (tt, tt), 0)
    ci = lax.broadcasted_iota(jnp.int32, (tt, tt), 1)
    causal = ci <= ri
    strict = ci < ri
    eye = jnp.where(ci == ri, 1.0, 0.0).astype(F32)
    sel = jnp.where(lax.broadcasted_iota(jnp.int32, (SUBLANES, BA_W), 0)
                    == lax.broadcasted_iota(jnp.int32, (SUBLANES, BA_W), 1), 1.0, 0.0).astype(BF16)

    def seq_group(bp, carry_):
        chains = []
        for s in range(group):
            b = bp * group + s
            seq_rows = pl.ds(b, tt, stride=nb)
            pbv = pb[seq_rows, :]
            gcv = gcb[seq_rows, :]
            gct = _dot_exact(gcv, sel, _NT, x_first=False)
            for h in range(DN_HEADS):
                chains.append(dict(b=b, h=h, rows=seq_rows, q=qb[h, seq_rows, :], k=kb[h, seq_rows, :],
                                   v=vb[h, seq_rows, :],
                                   beta=pbv[:, h:h + 1], gcol=gcv[:, DN_HEADS + h:DN_HEADS + h + 1],
                                   grow=gct[DN_HEADS + h:DN_HEADS + h + 1, :]))
        for c in chains:
            c['dec'] = jnp.where(causal, jnp.exp(jnp.where(causal, c['gcol'] - c['grow'], 0.0)), 0.0)
            c['qkk'] = _dot_nt(jnp.concatenate([c['q'], c['k']], axis=0), c['k'])
        for c in chains:
            c['p'] = -jnp.where(strict, c['beta'] * c['qkk'][tt:2 * tt] * c['dec'], 0.0)
            c['t'] = eye + c['p']
        for _ in range(levels - 1):
            for c in chains:
                c['p'] = _dot(c['p'], c['p'])
            for c in chains:
                c['t'] = c['t'] + _dot(c['p'], c['t'])
        for c in chains:
            eg = jnp.exp(c['gcol'])
            rhs = jnp.concatenate([c['beta'] * c['v'], (c['beta'] * eg) * c['k']], axis=1)
            c['uw'] = _dot(c['t'], rhs)
            c['qd'] = c['q'] * eg
            c['glast'] = c['grow'][:, tt - 1:tt]
            c['kd'] = c['k'] * jnp.exp(c['glast'] - c['gcol'])
            c['s'] = s_ref[c['b'], c['h']]
        for c in chains:
            c['r'] = _dot(jnp.concatenate([c['uw'][:, DN_HD:2 * DN_HD], c['qd']], axis=0), c['s'])
        for c in chains:
            c['vn'] = c['uw'][:, 0:DN_HD] - c['r'][0:tt]
        for c in chains:
            ob[c['h'], c['rows'], :] = c['r'][tt:2 * tt] + _dot(c['qkk'][0:tt] * c['dec'], c['vn'])
            s_ref[c['b'], c['h']] = c['s'] * jnp.exp(c['glast']) + _dot_tn(c['kd'], c['vn'])
        return carry_

    lax.fori_loop(0, nb // group, seq_group, 0)

    gate = zg_ref[...].reshape(n, DN_WIDTH)
    dnw = dnw_ref[...]
    for h in range(DN_HEADS):
        hs = slice(h * DN_HD, (h + 1) * DN_HD)
        o_ref[:, :, hs] = (_rms(ob[h], dnw) * _silu(gate[:, hs])).reshape(tt, nb, DN_HD)


def _delta_call(qkv, zg, zba, coef, dnw, s0, prev, layer, bsz, t, tt, group):
    nb = SUBLANES
    tm = lambda wd: pl.BlockSpec((tt, nb, wd), lambda j, i: (i, j, 0))
    st_spec = _layer_spec(layer, (nb, DN_HEADS, DN_HD, DN_HD), lambda j, i: (j, 0, 0, 0))
    heads = pltpu.VMEM((DN_HEADS, tt * nb, DN_HD), F32)
    rows = pltpu.VMEM((tt * nb, BA_W), F32)
    n_in = 6
    return pl.pallas_call(
        functools.partial(_delta_body, tt=tt, group=group, n_alias=len(prev)),
        grid=(bsz // nb, t // tt),
        in_specs=[tm(QKV_W), tm(DN_WIDTH), tm(BA_W), _layer_full(layer, (2, BA_W)),
                  _layer_full(layer, (1, DN_HD)), st_spec] + [_ANY] * len(prev),
        out_specs=[tm(DN_WIDTH), st_spec],
        out_shape=[jax.ShapeDtypeStruct((t, bsz, DN_WIDTH), F32),
                   jax.ShapeDtypeStruct((DEPTH, bsz, DN_HEADS, DN_HD, DN_HD), F32)],
        input_output_aliases={n_in + k: 1 + k for k in range(len(prev))},
        scratch_shapes=[heads, heads, heads, rows, rows, heads],
        compiler_params=_params(2),
        name="delta",
    )(qkv, zg, zba, coef, dnw, s0, *prev)


def _hgrn_body(*refs, tt, n_tiles, group, n_alias):
    zh_ref, lb_ref, nw_ref, gm_ref, s0_ref = refs[:5]
    o_ref, s_ref, sbd, qsb, ksb, qeb, kdb, vb, elb, ob = refs[5 + n_alias:]
    ti = pl.program_id(1)
    nb = SUBLANES
    n = tt * nb
    mid = tt // 2 - 1
    sh_h = int(math.log2(HG_HD))

    @pl.when(ti == 0)
    def _():
        zero = jnp.zeros((HG_HD, HG_HD), F32)

        def init(b, c):
            for h in range(HG_HEADS):
                parts = [zero] * HG_HEADS
                parts[h] = s0_ref[b, h].T
                sbd[b, h * HG_HD:(h + 1) * HG_HD, :] = jnp.concatenate(parts, axis=1)
            return c
        for b in range(nb):
            init(b, 0)

    zh = zh_ref[...].reshape(n, ZH_W)
    lb = lb_ref[...]
    f = lb + (1.0 - lb) * _sigmoid_exp(zh[:, HG_WIDTH:2 * HG_WIDTH])
    q = _silu(zh[:, 0:HG_WIDTH]).reshape(tt, nb, HG_WIDTH)
    k = (1.0 - f).reshape(tt, nb, HG_WIDTH)
    bc = _dot_exact(jnp.log(f), _seq_tri(n, nb), _MM, x_first=False).reshape(tt, nb, HG_WIDTH)
    bm = bc[mid]
    bl = bc[tt - 1]
    half = HG_WIDTH // 2
    qs = q * jnp.exp(bc - bm[None])
    ks = k * jnp.exp(bm[None] - bc)
    for dst, val in ((qsb, qs), (ksb, ks), (qeb, qs * jnp.exp(bm)[None]), (kdb, ks * jnp.exp(bl - bm)[None])):
        val = val.reshape(n, HG_WIDTH)
        dst[0] = val[:, 0:half]
        dst[1] = val[:, half:HG_WIDTH]
    vb[0] = zh[:, 2 * HG_WIDTH:2 * HG_WIDTH + half]
    vb[1] = zh[:, 2 * HG_WIDTH + half:3 * HG_WIDTH]
    elb[...] = jnp.exp(bl)

    ri = lax.broadcasted_iota(jnp.int32, (HG_HEADS * tt, tt), 0)
    ci = lax.broadcasted_iota(jnp.int32, (HG_HEADS * tt, tt), 1)
    causal = ci <= (ri & (tt - 1))
    lane_head = lax.broadcasted_iota(jnp.int32, (tt, HG_WIDTH), 1) >> sh_h
    hmask = [jnp.where(lane_head == h, 1.0, 0.0).astype(F32) for h in range(HG_HEADS)]
    bd = ((lax.broadcasted_iota(jnp.int32, (HG_WIDTH, HG_WIDTH), 0) >> sh_h)
          == (lax.broadcasted_iota(jnp.int32, (HG_WIDTH, HG_WIDTH), 1) >> sh_h))

    def seq_group(bp, carry_):
        seqs = []
        for s in range(group):
            b = bp * group + s
            seq_rows = pl.ds(b, tt, stride=nb)
            seq = lambda ref: jnp.concatenate([ref[0, seq_rows, :], ref[1, seq_rows, :]], axis=1)
            seqs.append(dict(b=b, rows=seq_rows, qs=seq(qsb), ks=seq(ksb), qe=seq(qeb), kd=seq(kdb),
                             v=seq(vb), st=sbd[b], el=elb[pl.ds(b, 1), :]))
        for c in seqs:
            c['a'] = jnp.where(causal, _dot_nt(jnp.concatenate([c['qs'] * m for m in hmask], axis=0), c['ks']), 0.0)
            c['o'] = _dot_nt(c['qe'], c['st'])
            c['kv'] = _dot_tn(c['v'], c['kd'])
        for c in seqs:
            o = c['o']
            for h in range(HG_HEADS):
                o = o + hmask[h] * _dot(c['a'][h * tt:(h + 1) * tt], c['v'])
            ob[0, c['rows'], :] = o[:, 0:half]
            ob[1, c['rows'], :] = o[:, half:HG_WIDTH]
            sbd[c['b']] = c['st'] * c['el'] + jnp.where(bd, c['kv'], 0.0)
        return carry_

    lax.fori_loop(0, nb // group, seq_group, 0)

    o = jnp.concatenate([ob[0], ob[1]], axis=1)
    ms = _dot_exact(o * o, gm_ref[...], _MM) * (1.0 / HG_HD)
    o_ref[...] = (o * lax.rsqrt(ms + EPS) * nw_ref[...]
                  * _silu(zh[:, 3 * HG_WIDTH:4 * HG_WIDTH])).reshape(tt, nb, HG_WIDTH)

    @pl.when(ti == n_tiles - 1)
    def _():
        def fin(b, c):
            for h in range(HG_HEADS):
                blk = sbd[b, h * HG_HD:(h + 1) * HG_HD, :]
                s_ref[b, h] = blk[:, h * HG_HD:(h + 1) * HG_HD].T
            return c
        for b in range(nb):
            fin(b, 0)


def _hgrn_call(zh, lb, nw, gm, s0, prev, layer, bsz, t, tt, group):
    nb = SUBLANES
    n_tiles = t // tt
    st_spec = _layer_spec(layer, (nb, HG_HEADS, HG_HD, HG_HD), lambda j, i: (j, 0, 0, 0))
    tile = pltpu.VMEM((HG_WIDTH // LANES, tt * nb, LANES), F32)
    n_in = 5
    return pl.pallas_call(
        functools.partial(_hgrn_body, tt=tt, n_tiles=n_tiles, group=group, n_alias=len(prev)),
        grid=(bsz // nb, n_tiles),
        in_specs=[pl.BlockSpec((tt, nb, ZH_W), lambda j, i: (i, j, 0)),
                  _layer_full(layer, (1, HG_WIDTH)), _layer_full(layer, (1, HG_WIDTH)),
                  pl.BlockSpec((HG_WIDTH, HG_WIDTH), lambda j, i: (0, 0)), st_spec] + [_ANY] * len(prev),
        out_specs=[pl.BlockSpec((tt, nb, HG_WIDTH), lambda j, i: (i, j, 0)), st_spec],
        out_shape=[jax.ShapeDtypeStruct((t, bsz, HG_WIDTH), F32),
                   jax.ShapeDtypeStruct((DEPTH, bsz, HG_HEADS, HG_HD, HG_HD), F32)],
        input_output_aliases={n_in + k: 1 + k for k in range(len(prev))},
        scratch_shapes=[pltpu.VMEM((nb, HG_WIDTH, HG_WIDTH), F32),
                        tile, tile, tile, tile, tile, pltpu.VMEM((nb, HG_WIDTH), F32), tile],
        compiler_params=_params(2),
        name="hgrn",
    )(zh, lb, nw, gm, s0, *prev)


def _s5_body(u_ref, bh_ref, lam_ref, c_ref, d_ref, gw_ref, gb_ref, x0r_ref, x0i_ref,
             o_ref, xr_ref, xi_ref, *, bsz, steps):
    i = pl.program_id(0)

    @pl.when(i == 0)
    def _():
        xr_ref[...] = x0r_ref[...]
        xi_ref[...] = x0i_ref[...]

    u = u_ref[...].reshape(steps * bsz, SSM_WIDTH)
    bu = _dot(u, bh_ref[...])

    lr = jnp.broadcast_to(lam_ref[0:1, :], (SUBLANES, SSM_FLAT))
    li = jnp.broadcast_to(lam_ref[1:2, :], (SUBLANES, SSM_FLAT))

    n_rb = bsz // SUBLANES
    xr_all = xr_ref[...]
    xi_all = xi_ref[...]
    scanned = [[None] * n_rb for _ in range(steps)]
    last_r, last_i = [], []
    for rb in range(n_rb):
        xr = xr_all[rb * SUBLANES:(rb + 1) * SUBLANES]
        xi = xi_all[rb * SUBLANES:(rb + 1) * SUBLANES]
        for t in range(steps):
            r0 = t * bsz + rb * SUBLANES
            xr, xi = (lr * xr - li * xi + bu[r0:r0 + SUBLANES, 0:SSM_FLAT],
                      lr * xi + li * xr + bu[r0:r0 + SUBLANES, SSM_FLAT:2 * SSM_FLAT])
            scanned[t][rb] = jnp.concatenate([xr, xi], axis=1)
        last_r.append(xr)
        last_i.append(xi)
    xr_ref[...] = jnp.concatenate(last_r, axis=0)
    xi_ref[...] = jnp.concatenate(last_i, axis=0)
    xs = jnp.concatenate([scanned[t][rb] for t in range(steps) for rb in range(n_rb)], axis=0)

    y = _dot(xs, c_ref[...]) + d_ref[...] * u
    y = 0.5 * y * (1.0 + jnp.tanh(math.sqrt(2.0 / math.pi) * (y + 0.044715 * (y * y * y))))
    o_ref[...] = (y * _sigmoid(_dot(y, gw_ref[...]) + gb_ref[...])).reshape(steps, bsz, SSM_WIDTH)


def _s5_call(u, bh, lam, cm, d, gw, gb, x0r, x0i, layer, bsz, t, steps):
    rows = steps * bsz
    full = lambda shp: pl.BlockSpec(shp, lambda i: tuple(0 for _ in shp))
    lfull = lambda shp: _layer_full(layer, shp)
    return pl.pallas_call(
        functools.partial(_s5_body, bsz=bsz, steps=steps),
        grid=(t // steps,),
        in_specs=[pl.BlockSpec((steps, bsz, SSM_WIDTH), lambda i: (i, 0, 0)),
                  lfull((SSM_WIDTH, 2 * SSM_FLAT)),
                  lfull((2, SSM_FLAT)), lfull((2 * SSM_FLAT, SSM_WIDTH)), lfull((1, SSM_WIDTH)),
                  lfull((SSM_WIDTH, SSM_WIDTH)), lfull((1, SSM_WIDTH)),
                  full((bsz, SSM_FLAT)), full((bsz, SSM_FLAT))],
        out_specs=[pl.BlockSpec((steps, bsz, SSM_WIDTH), lambda i: (i, 0, 0)),
                   full((bsz, SSM_FLAT)), full((bsz, SSM_FLAT))],
        out_shape=[jax.ShapeDtypeStruct((t, bsz, SSM_WIDTH), F32),
                   jax.ShapeDtypeStruct((bsz, SSM_FLAT), F32),
                   jax.ShapeDtypeStruct((bsz, SSM_FLAT), F32)],
        compiler_params=_params(1),
        name="s5",
    )(u, bh, lam, cm, d, gw, gb, x0r, x0i)


def _tail_body(*refs, tt, nb, n_ffc, ffc, n_sub, emit_final):
    (h_ref, oa_ref, ob_ref, oc_ref, wo_ref, nf_ref, wua_ref, wub_ref, cwa_ref, cwb_ref,
     wd_ref, csa_ref, csb_ref, p_ref, npl_ref, wg_ref, wp_ref) = refs[:17]
    rest = refs[17:]
    if emit_final:
        fin_ref, y_ref, cso_a_ref, cso_b_ref, hn, p3, car_a, car_b, ys, out_ref = rest
    else:
        out_ref, cso_a_ref, cso_b_ref, hn, p3, car_a, car_b = rest
    i = pl.program_id(1)
    j = pl.program_id(2)
    n = tt * nb
    hist = FF_CONV - 1
    st = tt // n_sub
    sr = st * nb
    hr = hist * nb

    @pl.when(j == 0)
    def _():
        h2 = (h_ref[...].reshape(n, D_MODEL) + _dot(oa_ref[...].reshape(n, DN_WIDTH), wo_ref[0:DN_WIDTH, :])
              + _dot(ob_ref[...].reshape(n, SSM_WIDTH), wo_ref[DN_WIDTH:DN_WIDTH + SSM_WIDTH, :])
              + _dot(oc_ref[...].reshape(n, HG_WIDTH), wo_ref[DN_WIDTH + SSM_WIDTH:D_MODEL, :]))
        out_ref[...] = h2.reshape(tt, nb, D_MODEL)
        hn[...] = _rms(h2, nf_ref[...]).astype(BF16)

    @pl.when(i == 0)
    def _():
        def cp(b, c):
            car_a[j, :, b, :] = csa_ref[b]
            car_b[j, :, b, :] = csb_ref[b]
            return c
        lax.fori_loop(0, nb, cp, 0)

    prev = [car_a[j].reshape(hr, ffc), car_b[j].reshape(hr, ffc)]
    wd = wd_ref[...]
    for sb in range(n_sub):
        hs = hn[sb * sr:(sb + 1) * sr, :]
        halves = []
        for idx, (wu, cw) in enumerate(((wua_ref, cwa_ref), (wub_ref, cwb_ref))):
            x = jnp.concatenate([prev[idx], jnp.dot(hs, wu[...], preferred_element_type=F32)], axis=0)
            w = cw[...]
            acc = w[0:1] * x[0:sr]
            for s in range(1, FF_CONV):
                acc = acc + w[s:s + 1] * x[s * nb:s * nb + sr]
            halves.append(acc)
            prev[idx] = x[sr:sr + hr]
        out_ref[sb * st:(sb + 1) * st] += _dot(_silu(halves[0]) * halves[1], wd).reshape(st, nb, D_MODEL)
    for car, cso, last in ((car_a, cso_a_ref, prev[0]), (car_b, cso_b_ref, prev[1])):
        last = last.reshape(hist, nb, ffc)
        car[j] = last
        cso[...] = last

    @pl.when(j == n_ffc - 1)
    def _():
        def cp(b, c):
            p3[:, b, :] = p_ref[b]
            return c
        lax.fori_loop(0, nb, cp, 0)
        h3 = out_ref[...].reshape(n, D_MODEL)
        gate = _sigmoid(_dot(_rms(h3, npl_ref[...]), wg_ref[...]))
        h4 = h3 + gate * _dot(p3[...].reshape(n, PLE_DIM), wp_ref[...])
        out_ref[...] = h4.reshape(tt, nb, D_MODEL)
        if emit_final:
            yv = _rms(h4, fin_ref[...])
            n_slab = D_MODEL // LANES
            for s in range(n_slab):
                ys[s] = yv[:, s * LANES:(s + 1) * LANES]

            def seq(b, c):
                rows = pl.ds(b, tt, stride=nb)
                y_ref[b] = jnp.concatenate([ys[s, rows, :] for s in range(n_slab)], axis=1)
                return c
            lax.fori_loop(0, nb, seq, 0)


def _tail_call(h, oa, ob, oc, wo, nf, wu, cw, wd, cs, p, npl, wg, wp, fin, layer, bsz, t, tt, nb, ffc, n_sub):
    n_ffc = FF_DIM // ffc
    n_t = t // tt
    hist = FF_CONV - 1
    tm = lambda wd_: pl.BlockSpec((tt, nb, wd_), lambda jb, i, j: (i, jb, 0))
    full = lambda shp: _layer_full(layer, shp, pipeline_mode=pl.Buffered(1))
    lyr = lambda blk, idx: _layer_spec(layer, blk, idx)
    wchunk = (lambda blk, idx: _layer_spec(layer, blk, idx, pipeline_mode=pl.Buffered(1))) if n_ffc == 1 else lyr
    cs_a = wchunk((nb, hist, ffc), lambda jb, i, j: (jb, 0, j))
    cs_b = wchunk((nb, hist, ffc), lambda jb, i, j: (jb, 0, n_ffc + j))
    cso = pl.BlockSpec((None, hist, nb, ffc), lambda jb, i, j: (i, 0, jb, j))
    scratch = [pltpu.VMEM((tt * nb, D_MODEL), BF16),
               pltpu.VMEM((tt, nb, PLE_DIM), F32),
               pltpu.VMEM((n_ffc, hist, nb, ffc), F32),
               pltpu.VMEM((n_ffc, hist, nb, ffc), F32)]
    emit_final = fin is not None
    out_specs = [tm(D_MODEL), cso, cso]
    out_shape = [jax.ShapeDtypeStruct((t, bsz, D_MODEL), F32),
                 jax.ShapeDtypeStruct((n_t, hist, bsz, FF_DIM), F32),
                 jax.ShapeDtypeStruct((n_t, hist, bsz, FF_DIM), F32)]
    extra_in, extra_specs = [], []
    if emit_final:
        extra_in = [fin]
        extra_specs = [pl.BlockSpec((1, D_MODEL), lambda jb, i, j: (0, 0))]
        out_specs[0] = pl.BlockSpec((nb, tt, D_MODEL), lambda jb, i, j: (jb, i, 0))
        out_shape[0] = jax.ShapeDtypeStruct((bsz, t, D_MODEL), F32)
        scratch += [pltpu.VMEM((D_MODEL // LANES, tt * nb, LANES), F32), pltpu.VMEM((tt, nb, D_MODEL), F32)]
    return pl.pallas_call(
        functools.partial(_tail_body, tt=tt, nb=nb, n_ffc=n_ffc, ffc=ffc, n_sub=n_sub, emit_final=emit_final),
        grid=(bsz // nb, n_t, n_ffc),
        in_specs=[tm(D_MODEL), tm(DN_WIDTH), tm(SSM_WIDTH), tm(HG_WIDTH),
                  full((D_MODEL, D_MODEL)), full((1, D_MODEL)),
                  wchunk((D_MODEL, ffc), lambda jb, i, j: (0, j)),
                  wchunk((D_MODEL, ffc), lambda jb, i, j: (0, n_ffc + j)),
                  lyr((FF_CONV, ffc), lambda jb, i, j: (0, j)),
                  lyr((FF_CONV, ffc), lambda jb, i, j: (0, n_ffc + j)),
                  wchunk((ffc, D_MODEL), lambda jb, i, j: (j, 0)),
                  cs_a, cs_b,
                  lyr((nb, tt, PLE_DIM), lambda jb, i, j: (jb, i, 0)),
                  full((1, D_MODEL)), full((D_MODEL, D_MODEL)), full((PLE_DIM, D_MODEL))] + extra_specs,
        out_specs=out_specs,
        out_shape=out_shape,
        scratch_shapes=scratch,
        compiler_params=_params(3),
        name="tail",
    )(h, oa, ob, oc, wo, nf, wu, wu, cw, cw, wd, cs, cs, p, npl, wg, wp, *extra_in)


def _pack(prm):
    (norm_mix, w_in, dn_conv_w, dn_a_log, dn_dt_bias, dn_norm, ssm_lam_re, ssm_lam_im, ssm_log_step,
     ssm_b_re, ssm_b_im, ssm_c_re, ssm_c_im, ssm_d, ssm_glu_w, ssm_glu_b, lower_bounds, hg_norm, w_out,
     norm_ffn, ffn_w_up, ffn_conv_w, ffn_w_down, norm_ple, ple_w_gate, ple_w_proj) = prm
    zeros4 = jnp.zeros((DEPTH, DN_HEADS), F32)
    pad = jnp.zeros((DEPTH, BA_W - 2 * DN_HEADS), F32)
    coef = jnp.stack([jnp.concatenate([zeros4, -jnp.exp(dn_a_log.astype(F32)), pad], axis=-1),
                      jnp.concatenate([zeros4, dn_dt_bias.astype(F32), pad], axis=-1)], axis=1)

    lre = ssm_lam_re.astype(F32)
    lim = ssm_lam_im.astype(F32)
    delta = jnp.exp(ssm_log_step.astype(F32))[..., None]
    mag = jnp.exp(lre * delta)
    lbr = mag * jnp.cos(lim * delta)
    lbi = mag * jnp.sin(lim * delta)
    den = lre * lre + lim * lim
    fr = ((lbr - 1.0) * lre + lbi * lim) / den
    fi = (lbi * lre - (lbr - 1.0) * lim) / den
    bre = ssm_b_re.astype(F32)
    bim = ssm_b_im.astype(F32)
    bbr = fr[..., None] * bre - fi[..., None] * bim
    bbi = fr[..., None] * bim + fi[..., None] * bre
    eye_g = jnp.eye(SSM_GROUPS, dtype=F32)

    def bdiag_in(m):
        return jnp.einsum('dgph,gk->dghkp', m, eye_g).reshape(DEPTH, SSM_WIDTH, SSM_FLAT)

    def bdiag_out(m):
        return jnp.einsum('dghp,gk->dgpkh', m, eye_g).reshape(DEPTH, SSM_FLAT, SSM_WIDTH)

    bmat = jnp.concatenate([bdiag_in(bbr), bdiag_in(bbi)], axis=-1).astype(BF16)
    cmat = jnp.concatenate([bdiag_out(ssm_c_re.astype(F32)), -bdiag_out(ssm_c_im.astype(F32))],
                           axis=1).astype(BF16)
    lam = jnp.stack([lbr.reshape(DEPTH, SSM_FLAT), lbi.reshape(DEPTH, SSM_FLAT)], axis=1)

    gmat = jnp.kron(jnp.eye(HG_HEADS, dtype=F32), jnp.ones((HG_HD, HG_HD), F32)).astype(BF16)
    row = lambda a: a.reshape(DEPTH, 1, a.shape[-1])
    return dict(
        norm_mix=row(norm_mix), w_in=w_in, conv_w=dn_conv_w, coef=coef, dn_norm=row(dn_norm),
        bmat=bmat, lam=lam, cmat=cmat, ssm_d=row(ssm_d), glu_w=ssm_glu_w.astype(BF16), glu_b=row(ssm_glu_b),
        lb=row(lower_bounds), hg_norm=row(jnp.tile(hg_norm, (1, HG_HEADS))), gmat=gmat,
        w_out=w_out.astype(BF16), norm_ffn=row(norm_ffn), w_up=ffn_w_up.astype(BF16), ffn_conv_w=ffn_conv_w,
        w_down=ffn_w_down.astype(BF16), norm_ple=row(norm_ple), ple_gate=ple_w_gate.astype(BF16),
        ple_proj=ple_w_proj.astype(BF16))


def _tiles(bsz, t):
    if t >= DN_CHUNK:
        return dict(tok_tt=512 // bsz, tok_nb=bsz, tail_nb=bsz, dn_tt=DN_CHUNK, hg_tt=HG_CHUNK,
                    s5_steps=512 // bsz, ffc=FF_DIM, n_sub=1, dn_group=4, hg_group=8)
    return dict(tok_tt=t, tok_nb=512 // t, tail_nb=256 // t, dn_tt=t, hg_tt=t, s5_steps=t, ffc=FF_DIM // 2,
                n_sub=2, dn_group=8, hg_group=8)


def _trunk(x, p, conv_qkv, delta, ssm_re, ssm_im, hgrn, conv_ffn, lp, norm_final):
    bsz, t, _ = x.shape
    tl = _tiles(bsz, t)
    sr_l, si_l, cf_l = [], [], []
    cq_prev, dn_prev, hg_prev = (), (), ()
    h = x
    for i in range(DEPTH):
        res = _proj_in(h, lp['norm_mix'], lp['w_in'], lp['conv_w'], conv_qkv, cq_prev, i, bsz, t,
                       tl['tok_tt'], tl['tok_nb'], from_btd=(i == 0))
        if i == 0:
            h = res[6]
        qkv, zg, zu, zh, zba, cq = res[:6]
        cq_prev = (cq,)
        o_a, dl = _delta_call(qkv, zg, zba, lp['coef'], lp['dn_norm'], delta, dn_prev, i, bsz, t,
                              tl['dn_tt'], tl['dn_group'])
        dn_prev = (dl,)
        o_b, sr, si = _s5_call(zu, lp['bmat'], lp['lam'], lp['cmat'], lp['ssm_d'], lp['glu_w'], lp['glu_b'],
                               ssm_re[i].reshape(bsz, SSM_FLAT), ssm_im[i].reshape(bsz, SSM_FLAT),
                               i, bsz, t, tl['s5_steps'])
        o_c, hg = _hgrn_call(zh, lp['lb'], lp['hg_norm'], lp['gmat'], hgrn, hg_prev, i, bsz, t, tl['hg_tt'],
                             tl['hg_group'])
        hg_prev = (hg,)
        res = _tail_call(h, o_a, o_b, o_c, lp['w_out'], lp['norm_ffn'], lp['w_up'], lp['ffn_conv_w'],
                         lp['w_down'], conv_ffn, p, lp['norm_ple'], lp['ple_gate'], lp['ple_proj'],
                         norm_final if i == DEPTH - 1 else None,
                         i, bsz, t, tl['tok_tt'], tl['tail_nb'], tl['ffc'], tl['n_sub'])
        h, cfa, cfb = res
        sr_l.append(sr.reshape(bsz, SSM_GROUPS, SSM_STATE))
        si_l.append(si.reshape(bsz, SSM_GROUPS, SSM_STATE))
        cf_l.append(jnp.swapaxes(jnp.concatenate([cfa[-1], cfb[-1]], axis=-1), 0, 1))
    y = h
    return (y, cq, dl, jnp.stack(sr_l), jnp.stack(si_l), hg, jnp.stack(cf_l))


def kernel(x_prompt, x_sample, p_prompt, p_sample, state_conv_qkv, state_delta, state_ssm_re, state_ssm_im, state_hgrn, state_conv_ffn, norm_mix, w_in, dn_conv_w, dn_a_log, dn_dt_bias, dn_norm, ssm_lam_re, ssm_lam_im, ssm_log_step, ssm_b_re, ssm_b_im, ssm_c_re, ssm_c_im, ssm_d, ssm_glu_w, ssm_glu_b, hg_lower, hg_norm, w_out, norm_ffn, ffn_w_up, ffn_conv_w, ffn_w_down, norm_ple, ple_w_gate, ple_w_proj, norm_final):
    lb_p = jax.nn.softmax(hg_lower.astype(F32), axis=0)
    lower_bounds = jnp.cumsum(lb_p, axis=0) - lb_p[0]
    prm = (norm_mix, w_in, dn_conv_w, dn_a_log, dn_dt_bias, dn_norm, ssm_lam_re, ssm_lam_im, ssm_log_step,
           ssm_b_re, ssm_b_im, ssm_c_re, ssm_c_im, ssm_d, ssm_glu_w, ssm_glu_b, lower_bounds, hg_norm, w_out,
           norm_ffn, ffn_w_up, ffn_conv_w, ffn_w_down, norm_ple, ple_w_gate, ple_w_proj)
    layers = _pack(prm)
    nf = norm_final.reshape(1, D_MODEL)

    bp = x_prompt.shape[0]
    z = lambda *shp: jnp.zeros((DEPTH, bp) + shp, F32)
    prompt = _trunk(x_prompt, p_prompt, z(DN_CONV - 1, QKV_W), z(DN_HEADS, DN_HD, DN_HD),
                    z(SSM_GROUPS, SSM_STATE), z(SSM_GROUPS, SSM_STATE), z(HG_HEADS, HG_HD, HG_HD),
                    z(FF_CONV - 1, 2 * FF_DIM), layers, nf)
    sample = _trunk(x_sample, p_sample, state_conv_qkv, state_delta, state_ssm_re, state_ssm_im, state_hgrn,
                    state_conv_ffn, layers, nf)
    return (prompt[0], sample[0]) + prompt[1:] + sample[1:]
```

```python
import functools
import math

import jax
import jax.numpy as jnp
from jax import lax
from jax.experimental import pallas as pl
from jax.experimental.pallas import tpu as pltpu

F32 = jnp.float32
BF16 = jnp.bfloat16

D_MODEL = 1024
DEPTH = 2
DN_HEADS = 4
DN_WIDTH = 512
DN_HD = 128
DN_CONV = 4
DN_CHUNK = 64
SSM_WIDTH = 256
SSM_GROUP = 16
SSM_GROUPS = 16
SSM_STATE = 64
SSM_FLAT = SSM_GROUPS * SSM_STATE
HG_WIDTH = 256
HG_HEADS = 4
HG_HD = 64
HG_CHUNK = 32
FF_DIM = 2816
FF_CONV = 3
PLE_DIM = 256
EPS = 1e-6

SUBLANES = 8
LANES = 128

QKV_W = 3 * DN_WIDTH
ZH_W = 4 * HG_WIDTH
BA_W = LANES
IN_PACKED = QKV_W + DN_WIDTH + SSM_WIDTH + ZH_W + BA_W
IN_RAW = QKV_W + DN_WIDTH + 2 * DN_HEADS + SSM_WIDTH + ZH_W
_Z_WIDTHS = (QKV_W, DN_WIDTH, SSM_WIDTH, ZH_W, BA_W)

VMEM_LIMIT = 56 * 1024 * 1024

_NT = (((1,), (1,)), ((), ()))
_TN = (((0,), (0,)), ((), ()))


def _dot(a, b):
    return jnp.dot(a.astype(BF16), b.astype(BF16), preferred_element_type=F32)


def _dot_nt(a, b):
    return lax.dot_general(a.astype(BF16), b.astype(BF16), _NT, preferred_element_type=F32)


def _dot_tn(a, b):
    return lax.dot_general(a.astype(BF16), b.astype(BF16), _TN, preferred_element_type=F32)


def _split3(x):
    hi = x.astype(BF16)
    r1 = x - hi.astype(F32)
    mid = r1.astype(BF16)
    lo = (r1 - mid.astype(F32)).astype(BF16)
    return hi, mid, lo


def _dot_exact(x, m, dims, x_first=True):
    if x_first:
        return sum(lax.dot_general(p, m, dims, preferred_element_type=F32) for p in _split3(x))
    return sum(lax.dot_general(m, p, dims, preferred_element_type=F32) for p in _split3(x))


_MM = (((1,), (0,)), ((), ()))


def _sigmoid_exp(x):
    return 1.0 / (1.0 + jnp.exp(-x))


def _sigmoid(x):
    return 0.5 * jnp.tanh(0.5 * x) + 0.5


def _silu(x):
    h = 0.5 * x
    return h + h * jnp.tanh(h)


def _layer_spec(layer, block, index, **kw):
    return pl.BlockSpec((None,) + tuple(block), lambda *g: (layer,) + tuple(index(*g)), **kw)


def _layer_full(layer, shape, **kw):
    return _layer_spec(layer, shape, lambda *g: (0,) * len(shape), **kw)


_ANY = pl.BlockSpec(memory_space=pl.ANY)


def _softplus(x):
    return jnp.maximum(x, 0.0) + jnp.log1p(jnp.exp(-jnp.abs(x)))


def _rms(x, g):
    ms = jnp.mean(x * x, axis=-1, keepdims=True)
    return x * lax.rsqrt(ms + EPS) * g


def _seq_tri(n, nb):
    r = lax.broadcasted_iota(jnp.int32, (n, n), 0)
    c = lax.broadcasted_iota(jnp.int32, (n, n), 1)
    return jnp.where(((r & (nb - 1)) == (c & (nb - 1))) & (c <= r), 1.0, 0.0).astype(BF16)


def _params(n_axes):
    return pltpu.CompilerParams(dimension_semantics=("arbitrary",) * n_axes, vmem_limit_bytes=VMEM_LIMIT)


def _proj_in_body(*refs, tt, nb, n_tiles, from_btd, n_alias):
    x_ref, g_ref, wraw_ref, cw_ref, cs_ref = refs[:5]
    outs = refs[5 + n_alias:]
    qkv_ref, zg_ref, zu_ref, zh_ref, zba_ref, cso_ref = outs[:6]
    buf, w_ref = outs[-2:]
    ti = pl.program_id(1)
    n = tt * nb
    hist = DN_CONV - 1

    @pl.when((pl.program_id(0) == 0) & (ti == 0))
    def _():
        o_ba = QKV_W + DN_WIDTH
        n_ba = 2 * DN_HEADS
        n_rest = SSM_WIDTH + ZH_W
        w_ref[:, 0:o_ba] = wraw_ref[:, 0:o_ba].astype(BF16)
        w_ref[:, o_ba:o_ba + n_rest] = wraw_ref[:, o_ba + n_ba:o_ba + n_ba + n_rest].astype(BF16)
        w_ref[:, o_ba + n_rest:IN_PACKED] = jnp.concatenate(
            [wraw_ref[:, o_ba:o_ba + n_ba], jnp.zeros((D_MODEL, BA_W - n_ba), F32)], axis=1).astype(BF16)
    if from_btd:
        h0 = outs[6]

        def cp(b, c):
            h0[:, b, :] = x_ref[b]
            return c

        lax.fori_loop(0, nb, cp, 0)
        x = h0[...].reshape(n, D_MODEL)
    else:
        x = x_ref[...].reshape(n, D_MODEL)
    xn = _rms(x, g_ref[...]).astype(BF16)

    @pl.when(ti == 0)
    def _():
        def cp(b, c):
            buf[:, b, :] = cs_ref[b]
            return c
        lax.fori_loop(0, nb, cp, 0)

    n_sub = 4
    st = tt // n_sub
    sr = st * nb
    hr = hist * nb
    cw = cw_ref[...]
    prev = buf[...].reshape(hr, QKV_W)
    for sb in range(n_sub):
        xs = xn[sb * sr:(sb + 1) * sr]
        ts = slice(sb * st, (sb + 1) * st)
        xcat = jnp.concatenate([prev, jnp.dot(xs, w_ref[:, 0:QKV_W], preferred_element_type=F32)], axis=0)
        c0 = QKV_W
        for ref, width in zip((zg_ref, zu_ref, zh_ref, zba_ref), _Z_WIDTHS[1:]):
            ref[ts] = jnp.dot(xs, w_ref[:, c0:c0 + width], preferred_element_type=F32).reshape(st, nb, width)
            c0 += width
        y = cw[0:1] * xcat[0:sr]
        for j in range(1, DN_CONV):
            y = y + cw[j:j + 1] * xcat[j * nb:j * nb + sr]
        prev = xcat[sr:sr + hr]
        y = _silu(y)
        for h in range(DN_HEADS):
            qs = slice(h * DN_HD, (h + 1) * DN_HD)
            ks = slice(DN_WIDTH + h * DN_HD, DN_WIDTH + (h + 1) * DN_HD)
            qh = y[:, qs]
            kh = y[:, ks]
            qn = qh * (lax.rsqrt(jnp.sum(qh * qh, axis=-1, keepdims=True) + EPS) * DN_HD ** -0.5)
            kn = kh * lax.rsqrt(jnp.sum(kh * kh, axis=-1, keepdims=True) + EPS)
            qkv_ref[ts, :, qs] = qn.reshape(st, nb, DN_HD)
            qkv_ref[ts, :, ks] = kn.reshape(st, nb, DN_HD)
        qkv_ref[ts, :, 2 * DN_WIDTH:QKV_W] = y[:, 2 * DN_WIDTH:QKV_W].reshape(st, nb, DN_WIDTH)
    buf[...] = prev.reshape(hist, nb, QKV_W)

    @pl.when(ti == n_tiles - 1)
    def _():
        def cp(b, c):
            cso_ref[b] = buf[:, b, :]
            return c
        lax.fori_loop(0, nb, cp, 0)


def _proj_in(x, g, w, cw, cs, prev, layer, bsz, t, tt, nb, from_btd):
    n_tiles = t // tt
    hist = DN_CONV - 1
    tm = lambda wd: pl.BlockSpec((tt, nb, wd), lambda j, i: (i, j, 0))
    x_spec = pl.BlockSpec((nb, tt, D_MODEL), lambda j, i: (j, i, 0)) if from_btd else tm(D_MODEL)
    cs_spec = _layer_spec(layer, (nb, hist, QKV_W), lambda j, i: (j, 0, 0))
    out_shape = [jax.ShapeDtypeStruct((t, bsz, wd), F32) for wd in _Z_WIDTHS]
    out_specs = [tm(wd) for wd in _Z_WIDTHS]
    out_shape.append(jax.ShapeDtypeStruct((DEPTH, bsz, hist, QKV_W), F32))
    out_specs.append(cs_spec)
    if from_btd:
        out_shape.append(jax.ShapeDtypeStruct((t, bsz, D_MODEL), F32))
        out_specs.append(tm(D_MODEL))
    n_in = 5
    return pl.pallas_call(
        functools.partial(_proj_in_body, tt=tt, nb=nb, n_tiles=n_tiles, from_btd=from_btd, n_alias=len(prev)),
        grid=(bsz // nb, n_tiles),
        in_specs=[x_spec, _layer_full(layer, (1, D_MODEL)),
                  _layer_full(layer, (D_MODEL, IN_RAW), pipeline_mode=pl.Buffered(1)),
                  _layer_full(layer, (DN_CONV, QKV_W)), cs_spec] + [_ANY] * len(prev),
        out_specs=out_specs,
        out_shape=out_shape,
        input_output_aliases={n_in + k: 5 + k for k in range(len(prev))},
        scratch_shapes=[pltpu.VMEM((hist, nb, QKV_W), F32), pltpu.VMEM((D_MODEL, IN_PACKED), BF16)],
        compiler_params=_params(2),
        name="proj_in",
    )(x, g, w, cw, cs, *prev)


def _delta_body(*refs, tt, group, n_alias):
    qkv_ref, zg_ref, zba_ref, coef_ref, dnw_ref, s0_ref = refs[:6]
    o_ref, s_ref, qb, kb, vb, pb, gcb, ob = refs[6 + n_alias:]
    ti = pl.program_id(1)
    nb = SUBLANES
    n = tt * nb
    levels = int(math.log2(tt))

    @pl.when(ti == 0)
    def _():
        s_ref[...] = s0_ref[...]

    for h in range(DN_HEADS):
        for dst, off in ((qb, 0), (kb, DN_WIDTH), (vb, 2 * DN_WIDTH)):
            dst[h] = qkv_ref[:, :, off + h * DN_HD:off + (h + 1) * DN_HD].reshape(n, DN_HD)

    zba = zba_ref[...].reshape(n, BA_W)
    coef = coef_ref[...]
    lane = lax.broadcasted_iota(jnp.int32, zba.shape, 1)
    g = coef[0:1] * _softplus(zba + coef[1:2])
    pb[...] = jnp.where(lane < DN_HEADS, _sigmoid(zba), g)
    gcb[...] = _dot_exact(g, _seq_tri(n, nb), _MM, x_first=False)

    ri = lax.broadcasted_iota(jnp.int32, (tt, tt), 0)
    ci = lax.broadcasted_iota(jnp.int32, (tt, tt), 1)
    causal = ci <= ri
    strict = ci < ri
    eye = jnp.where(ci == ri, 1.0, 0.0).astype(F32)
    sel = jnp.where(lax.broadcasted_iota(jnp.int32, (SUBLANES, BA_W), 0)
                    == lax.broadcasted_iota(jnp.int32, (SUBLANES, BA_W), 1), 1.0, 0.0).astype(BF16)

    def seq_group(bp, carry_):
        chains = []
        for s in range(group):
            b = bp * group + s
            seq_rows = pl.ds(b, tt, stride=nb)
            pbv = pb[seq_rows, :]
            gcv = gcb[seq_rows, :]
            gct = _dot_exact(gcv, sel, _NT, x_first=False)
            for h in range(DN_HEADS):
                chains.append(dict(b=b, h=h, rows=seq_rows, q=qb[h, seq_rows, :], k=kb[h, seq_rows, :],
                                   v=vb[h, seq_rows, :],
                                   beta=pbv[:, h:h + 1], gcol=gcv[:, DN_HEADS + h:DN_HEADS + h + 1],
                                   grow=gct[DN_HEADS + h:DN_HEADS + h + 1, :]))
        for c in chains:
            c['dec'] = jnp.where(causal, jnp.exp(jnp.where(causal, c['gcol'] - c['grow'], 0.0)), 0.0)
            c['qkk'] = _dot_nt(jnp.concatenate([c['q'], c['k']], axis=0), c['k'])
        for c in chains:
            c['p'] = -jnp.where(strict, c['beta'] * c['qkk'][tt:2 * tt] * c['dec'], 0.0)
            c['t'] = eye + c['p']
        for _ in range(levels - 1):
            for c in chains:
                c['p'] = _dot(c['p'], c['p'])
            for c in chains:
                c['t'] = c['t'] + _dot(c['p'], c['t'])
        for c in chains:
            eg = jnp.exp(c['gcol'])
            rhs = jnp.concatenate([c['beta'] * c['v'], (c['beta'] * eg) * c['k']], axis=1)
            c['uw'] = _dot(c['t'], rhs)
            c['qd'] = c['q'] * eg
            c['glast'] = c['grow'][:, tt - 1:tt]
            c['kd'] = c['k'] * jnp.exp(c['glast'] - c['gcol'])
            c['s'] = s_ref[c['b'], c['h']]
        for c in chains:
            c['r'] = _dot(jnp.concatenate([c['uw'][:, DN_HD:2 * DN_HD], c['qd']], axis=0), c['s'])
        for c in chains:
            c['vn'] = c['uw'][:, 0:DN_HD] - c['r'][0:tt]
        for c in chains:
            ob[c['h'], c['rows'], :] = c['r'][tt:2 * tt] + _dot(c['qkk'][0:tt] * c['dec'], c['vn'])
            s_ref[c['b'], c['h']] = c['s'] * jnp.exp(c['glast']) + _dot_tn(c['kd'], c['vn'])
        return carry_

    lax.fori_loop(0, nb // group, seq_group, 0)

    gate = zg_ref[...].reshape(n, DN_WIDTH)
    dnw = dnw_ref[...]
    for h in range(DN_HEADS):
        hs = slice(h * DN_HD, (h + 1) * DN_HD)
        o_ref[:, :, hs] = (_rms(ob[h], dnw) * _silu(gate[:, hs])).reshape(tt, nb, DN_HD)


def _delta_call(qkv, zg, zba, coef, dnw, s0, prev, layer, bsz, t, tt, group):
    nb = SUBLANES
    tm = lambda wd: pl.BlockSpec((tt, nb, wd), lambda j, i: (i, j, 0))
    st_spec = _layer_spec(layer, (nb, DN_HEADS, DN_HD, DN_HD), lambda j, i: (j, 0, 0, 0))
    heads = pltpu.VMEM((DN_HEADS, tt * nb, DN_HD), F32)
    rows = pltpu.VMEM((tt * nb, BA_W), F32)
    n_in = 6
    return pl.pallas_call(
        functools.partial(_delta_body, tt=tt, group=group, n_alias=len(prev)),
        grid=(bsz // nb, t // tt),
        in_specs=[tm(QKV_W), tm(DN_WIDTH), tm(BA_W), _layer_full(layer, (2, BA_W)),
                  _layer_full(layer, (1, DN_HD)), st_spec] + [_ANY] * len(prev),
        out_specs=[tm(DN_WIDTH), st_spec],
        out_shape=[jax.ShapeDtypeStruct((t, bsz, DN_WIDTH), F32),
                   jax.ShapeDtypeStruct((DEPTH, bsz, DN_HEADS, DN_HD, DN_HD), F32)],
        input_output_aliases={n_in + k: 1 + k for k in range(len(prev))},
        scratch_shapes=[heads, heads, heads, rows, rows, heads],
        compiler_params=_params(2),
        name="delta",
    )(qkv, zg, zba, coef, dnw, s0, *prev)


def _hgrn_body(*refs, tt, n_tiles, group, n_alias):
    zh_ref, lb_ref, nw_ref, gm_ref, s0_ref = refs[:5]
    o_ref, s_ref, sbd, qsb, ksb, qeb, kdb, vb, elb, ob = refs[5 + n_alias:]
    ti = pl.program_id(1)
    nb = SUBLANES
    n = tt * nb
    mid = tt // 2 - 1
    sh_h = int(math.log2(HG_HD))

    @pl.when(ti == 0)
    def _():
        zero = jnp.zeros((HG_HD, HG_HD), F32)

        def init(b, c):
            for h in range(HG_HEADS):
                parts = [zero] * HG_HEADS
                parts[h] = s0_ref[b, h].T
                sbd[b, h * HG_HD:(h + 1) * HG_HD, :] = jnp.concatenate(parts, axis=1)
            return c
        for b in range(nb):
            init(b, 0)

    zh = zh_ref[...].reshape(n, ZH_W)
    lb = lb_ref[...]
    f = lb + (1.0 - lb) * _sigmoid_exp(zh[:, HG_WIDTH:2 * HG_WIDTH])
    q = _silu(zh[:, 0:HG_WIDTH]).reshape(tt, nb, HG_WIDTH)
    k = (1.0 - f).reshape(tt, nb, HG_WIDTH)
    bc = _dot_exact(jnp.log(f), _seq_tri(n, nb), _MM, x_first=False).reshape(tt, nb, HG_WIDTH)
    bm = bc[mid]
    bl = bc[tt - 1]
    half = HG_WIDTH // 2
    qs = q * jnp.exp(bc - bm[None])
    ks = k * jnp.exp(bm[None] - bc)
    for dst, val in ((qsb, qs), (ksb, ks), (qeb, qs * jnp.exp(bm)[None]), (kdb, ks * jnp.exp(bl - bm)[None])):
        val = val.reshape(n, HG_WIDTH)
        dst[0] = val[:, 0:half]
        dst[1] = val[:, half:HG_WIDTH]
    vb[0] = zh[:, 2 * HG_WIDTH:2 * HG_WIDTH + half]
    vb[1] = zh[:, 2 * HG_WIDTH + half:3 * HG_WIDTH]
    elb[...] = jnp.exp(bl)

    ri = lax.broadcasted_iota(jnp.int32, (HG_HEADS * tt, tt), 0)
    ci = lax.broadcasted_iota(jnp.int32, (HG_HEADS * tt, tt), 1)
    causal = ci <= (ri & (tt - 1))
    lane_head = lax.broadcasted_iota(jnp.int32, (tt, HG_WIDTH), 1) >> sh_h
    hmask = [jnp.where(lane_head == h, 1.0, 0.0).astype(F32) for h in range(HG_HEADS)]
    bd = ((lax.broadcasted_iota(jnp.int32, (HG_WIDTH, HG_WIDTH), 0) >> sh_h)
          == (lax.broadcasted_iota(jnp.int32, (HG_WIDTH, HG_WIDTH), 1) >> sh_h))

    def seq_group(bp, carry_):
        seqs = []
        for s in range(group):
            b = bp * group + s
            seq_rows = pl.ds(b, tt, stride=nb)
            seq = lambda ref: jnp.concatenate([ref[0, seq_rows, :], ref[1, seq_rows, :]], axis=1)
            seqs.append(dict(b=b, rows=seq_rows, qs=seq(qsb), ks=seq(ksb), qe=seq(qeb), kd=seq(kdb),
                             v=seq(vb), st=sbd[b], el=elb[pl.ds(b, 1), :]))
        for c in seqs:
            c['a'] = jnp.where(causal, _dot_nt(jnp.concatenate([c['qs'] * m for m in hmask], axis=0), c['ks']), 0.0)
            c['o'] = _dot_nt(c['qe'], c['st'])
            c['kv'] = _dot_tn(c['v'], c['kd'])
        for c in seqs:
            o = c['o']
            for h in range(HG_HEADS):
                o = o + hmask[h] * _dot(c['a'][h * tt:(h + 1) * tt], c['v'])
            ob[0, c['rows'], :] = o[:, 0:half]
            ob[1, c['rows'], :] = o[:, half:HG_WIDTH]
            sbd[c['b']] = c['st'] * c['el'] + jnp.where(bd, c['kv'], 0.0)
        return carry_

    lax.fori_loop(0, nb // group, seq_group, 0)

    o = jnp.concatenate([ob[0], ob[1]], axis=1)
    ms = _dot_exact(o * o, gm_ref[...], _MM) * (1.0 / HG_HD)
    o_ref[...] = (o * lax.rsqrt(ms + EPS) * nw_ref[...]
                  * _silu(zh[:, 3 * HG_WIDTH:4 * HG_WIDTH])).reshape(tt, nb, HG_WIDTH)

    @pl.when(ti == n_tiles - 1)
    def _():
        def fin(b, c):
            for h in range(HG_HEADS):
                blk = sbd[b, h * HG_HD:(h + 1) * HG_HD, :]
                s_ref[b, h] = blk[:, h * HG_HD:(h + 1) * HG_HD].T
            return c
        for b in range(nb):
            fin(b, 0)


def _hgrn_call(zh, lb, nw, gm, s0, prev, layer, bsz, t, tt, group):
    nb = SUBLANES
    n_tiles = t // tt
    st_spec = _layer_spec(layer, (nb, HG_HEADS, HG_HD, HG_HD), lambda j, i: (j, 0, 0, 0))
    tile = pltpu.VMEM((HG_WIDTH // LANES, tt * nb, LANES), F32)
    n_in = 5
    return pl.pallas_call(
        functools.partial(_hgrn_body, tt=tt, n_tiles=n_tiles, group=group, n_alias=len(prev)),
        grid=(bsz // nb, n_tiles),
        in_specs=[pl.BlockSpec((tt, nb, ZH_W), lambda j, i: (i, j, 0)),
                  _layer_full(layer, (1, HG_WIDTH)), _layer_full(layer, (1, HG_WIDTH)),
                  pl.BlockSpec((HG_WIDTH, HG_WIDTH), lambda j, i: (0, 0)), st_spec] + [_ANY] * len(prev),
        out_specs=[pl.BlockSpec((tt, nb, HG_WIDTH), lambda j, i: (i, j, 0)), st_spec],
        out_shape=[jax.ShapeDtypeStruct((t, bsz, HG_WIDTH), F32),
                   jax.ShapeDtypeStruct((DEPTH, bsz, HG_HEADS, HG_HD, HG_HD), F32)],
        input_output_aliases={n_in + k: 1 + k for k in range(len(prev))},
        scratch_shapes=[pltpu.VMEM((nb, HG_WIDTH, HG_WIDTH), F32),
                        tile, tile, tile, tile, tile, pltpu.VMEM((nb, HG_WIDTH), F32), tile],
        compiler_params=_params(2),
        name="hgrn",
    )(zh, lb, nw, gm, s0, *prev)


def _s5_body(u_ref, bh_ref, lam_ref, c_ref, d_ref, gw_ref, gb_ref, x0r_ref, x0i_ref,
             o_ref, xr_ref, xi_ref, *, bsz, steps):
    i = pl.program_id(0)

    @pl.when(i == 0)
    def _():
        xr_ref[...] = x0r_ref[...]
        xi_ref[...] = x0i_ref[...]

    u = u_ref[...].reshape(steps * bsz, SSM_WIDTH)
    bu = _dot(u, bh_ref[...])

    lr = jnp.broadcast_to(lam_ref[0:1, :], (SUBLANES, SSM_FLAT))
    li = jnp.broadcast_to(lam_ref[1:2, :], (SUBLANES, SSM_FLAT))

    n_rb = bsz // SUBLANES
    xr_all = xr_ref[...]
    xi_all = xi_ref[...]
    scanned = [[None] * n_rb for _ in range(steps)]
    last_r, last_i = [], []
    for rb in range(n_rb):
        xr = xr_all[rb * SUBLANES:(rb + 1) * SUBLANES]
        xi = xi_all[rb * SUBLANES:(rb + 1) * SUBLANES]
        for t in range(steps):
            r0 = t * bsz + rb * SUBLANES
            xr, xi = (lr * xr - li * xi + bu[r0:r0 + SUBLANES, 0:SSM_FLAT],
                      lr * xi + li * xr + bu[r0:r0 + SUBLANES, SSM_FLAT:2 * SSM_FLAT])
            scanned[t][rb] = jnp.concatenate([xr, xi], axis=1)
        last_r.append(xr)
        last_i.append(xi)
    xr_ref[...] = jnp.concatenate(last_r, axis=0)
    xi_ref[...] = jnp.concatenate(last_i, axis=0)
    xs = jnp.concatenate([scanned[t][rb] for t in range(steps) for rb in range(n_rb)], axis=0)

    y = _dot(xs, c_ref[...]) + d_ref[...] * u
    y = 0.5 * y * (1.0 + jnp.tanh(math.sqrt(2.0 / math.pi) * (y + 0.044715 * (y * y * y))))
    o_ref[...] = (y * _sigmoid(_dot(y, gw_ref[...]) + gb_ref[...])).reshape(steps, bsz, SSM_WIDTH)


def _s5_call(u, bh, lam, cm, d, gw, gb, x0r, x0i, layer, bsz, t, steps):
    rows = steps * bsz
    full = lambda shp: pl.BlockSpec(shp, lambda i: tuple(0 for _ in shp))
    lfull = lambda shp: _layer_full(layer, shp)
    return pl.pallas_call(
        functools.partial(_s5_body, bsz=bsz, steps=steps),
        grid=(t // steps,),
        in_specs=[pl.BlockSpec((steps, bsz, SSM_WIDTH), lambda i: (i, 0, 0)),
                  lfull((SSM_WIDTH, 2 * SSM_FLAT)),
                  lfull((2, SSM_FLAT)), lfull((2 * SSM_FLAT, SSM_WIDTH)), lfull((1, SSM_WIDTH)),
                  lfull((SSM_WIDTH, SSM_WIDTH)), lfull((1, SSM_WIDTH)),
                  full((bsz, SSM_FLAT)), full((bsz, SSM_FLAT))],
        out_specs=[pl.BlockSpec((steps, bsz, SSM_WIDTH), lambda i: (i, 0, 0)),
                   full((bsz, SSM_FLAT)), full((bsz, SSM_FLAT))],
        out_shape=[jax.ShapeDtypeStruct((t, bsz, SSM_WIDTH), F32),
                   jax.ShapeDtypeStruct((bsz, SSM_FLAT), F32),
                   jax.ShapeDtypeStruct((bsz, SSM_FLAT), F32)],
        compiler_params=_params(1),
        name="s5",
    )(u, bh, lam, cm, d, gw, gb, x0r, x0i)


def _tail_body(*refs, tt, nb, n_ffc, ffc, n_sub, emit_final):
    (h_ref, oa_ref, ob_ref, oc_ref, wo_ref, nf_ref, wua_ref, wub_ref, cwa_ref, cwb_ref,
     wd_ref, csa_ref, csb_ref, p_ref, npl_ref, wg_ref, wp_ref) = refs[:17]
    rest = refs[17:]
    if emit_final:
        fin_ref, y_ref, cso_a_ref, cso_b_ref, hn, p3, car_a, car_b, ys, out_ref = rest
    else:
        out_ref, cso_a_ref, cso_b_ref, hn, p3, car_a, car_b = rest
    i = pl.program_id(1)
    j = pl.program_id(2)
    n = tt * nb
    hist = FF_CONV - 1
    st = tt // n_sub
    sr = st * nb
    hr = hist * nb

    @pl.when(j == 0)
    def _():
        h2 = (h_ref[...].reshape(n, D_MODEL) + _dot(oa_ref[...].reshape(n, DN_WIDTH), wo_ref[0:DN_WIDTH, :])
              + _dot(ob_ref[...].reshape(n, SSM_WIDTH), wo_ref[DN_WIDTH:DN_WIDTH + SSM_WIDTH, :])
              + _dot(oc_ref[...].reshape(n, HG_WIDTH), wo_ref[DN_WIDTH + SSM_WIDTH:D_MODEL, :]))
        out_ref[...] = h2.reshape(tt, nb, D_MODEL)
        hn[...] = _rms(h2, nf_ref[...]).astype(BF16)

    @pl.when(i == 0)
    def _():
        def cp(b, c):
            car_a[j, :, b, :] = csa_ref[b]
            car_b[j, :, b, :] = csb_ref[b]
            return c
        lax.fori_loop(0, nb, cp, 0)

    prev = [car_a[j].reshape(hr, ffc), car_b[j].reshape(hr, ffc)]
    wd = wd_ref[...]
    for sb in range(n_sub):
        hs = hn[sb * sr:(sb + 1) * sr, :]
        halves = []
        for idx, (wu, cw) in enumerate(((wua_ref, cwa_ref), (wub_ref, cwb_ref))):
            x = jnp.concatenate([prev[idx], jnp.dot(hs, wu[...], preferred_element_type=F32)], axis=0)
            w = cw[...]
            acc = w[0:1] * x[0:sr]
            for s in range(1, FF_CONV):
                acc = acc + w[s:s + 1] * x[s * nb:s * nb + sr]
            halves.append(acc)
            prev[idx] = x[sr:sr + hr]
        out_ref[sb * st:(sb + 1) * st] += _dot(_silu(halves[0]) * halves[1], wd).reshape(st, nb, D_MODEL)
    for car, cso, last in ((car_a, cso_a_ref, prev[0]), (car_b, cso_b_ref, prev[1])):
        last = last.reshape(hist, nb, ffc)
        car[j] = last
        cso[...] = last

    @pl.when(j == n_ffc - 1)
    def _():
        def cp(b, c):
            p3[:, b, :] = p_ref[b]
            return c
        lax.fori_loop(0, nb, cp, 0)
        h3 = out_ref[...].reshape(n, D_MODEL)
        gate = _sigmoid(_dot(_rms(h3, npl_ref[...]), wg_ref[...]))
        h4 = h3 + gate * _dot(p3[...].reshape(n, PLE_DIM), wp_ref[...])
        out_ref[...] = h4.reshape(tt, nb, D_MODEL)
        if emit_final:
            yv = _rms(h4, fin_ref[...])
            n_slab = D_MODEL // LANES
            for s in range(n_slab):
                ys[s] = yv[:, s * LANES:(s + 1) * LANES]

            def seq(b, c):
                rows = pl.ds(b, tt, stride=nb)
                y_ref[b] = jnp.concatenate([ys[s, rows, :] for s in range(n_slab)], axis=1)
                return c
            lax.fori_loop(0, nb, seq, 0)


def _tail_call(h, oa, ob, oc, wo, nf, wu, cw, wd, cs, p, npl, wg, wp, fin, layer, bsz, t, tt, nb, ffc, n_sub):
    n_ffc = FF_DIM // ffc
    n_t = t // tt
    hist = FF_CONV - 1
    tm = lambda wd_: pl.BlockSpec((tt, nb, wd_), lambda jb, i, j: (i, jb, 0))
    full = lambda shp: _layer_full(layer, shp, pipeline_mode=pl.Buffered(1))
    lyr = lambda blk, idx: _layer_spec(layer, blk, idx)
    wchunk = (lambda blk, idx: _layer_spec(layer, blk, idx, pipeline_mode=pl.Buffered(1))) if n_ffc == 1 else lyr
    cs_a = wchunk((nb, hist, ffc), lambda jb, i, j: (jb, 0, j))
    cs_b = wchunk((nb, hist, ffc), lambda jb, i, j: (jb, 0, n_ffc + j))
    cso = pl.BlockSpec((None, hist, nb, ffc), lambda jb, i, j: (i, 0, jb, j))
    scratch = [pltpu.VMEM((tt * nb, D_MODEL), BF16),
               pltpu.VMEM((tt, nb, PLE_DIM), F32),
               pltpu.VMEM((n_ffc, hist, nb, ffc), F32),
               pltpu.VMEM((n_ffc, hist, nb, ffc), F32)]
    emit_final = fin is not None
    out_specs = [tm(D_MODEL), cso, cso]
    out_shape = [jax.ShapeDtypeStruct((t, bsz, D_MODEL), F32),
                 jax.ShapeDtypeStruct((n_t, hist, bsz, FF_DIM), F32),
                 jax.ShapeDtypeStruct((n_t, hist, bsz, FF_DIM), F32)]
    extra_in, extra_specs = [], []
    if emit_final:
        extra_in = [fin]
        extra_specs = [pl.BlockSpec((1, D_MODEL), lambda jb, i, j: (0, 0))]
        out_specs[0] = pl.BlockSpec((nb, tt, D_MODEL), lambda jb, i, j: (jb, i, 0))
        out_shape[0] = jax.ShapeDtypeStruct((bsz, t, D_MODEL), F32)
        scratch += [pltpu.VMEM((D_MODEL // LANES, tt * nb, LANES), F32), pltpu.VMEM((tt, nb, D_MODEL), F32)]
    return pl.pallas_call(
        functools.partial(_tail_body, tt=tt, nb=nb, n_ffc=n_ffc, ffc=ffc, n_sub=n_sub, emit_final=emit_final),
        grid=(bsz // nb, n_t, n_ffc),
        in_specs=[tm(D_MODEL), tm(DN_WIDTH), tm(SSM_WIDTH), tm(HG_WIDTH),
                  full((D_MODEL, D_MODEL)), full((1, D_MODEL)),
                  wchunk((D_MODEL, ffc), lambda jb, i, j: (0, j)),
                  wchunk((D_MODEL, ffc), lambda jb, i, j: (0, n_ffc + j)),
                  lyr((FF_CONV, ffc), lambda jb, i, j: (0, j)),
                  lyr((FF_CONV, ffc), lambda jb, i, j: (0, n_ffc + j)),
                  wchunk((ffc, D_MODEL), lambda jb, i, j: (j, 0)),
                  cs_a, cs_b,
                  lyr((nb, tt, PLE_DIM), lambda jb, i, j: (jb, i, 0)),
                  full((1, D_MODEL)), full((D_MODEL, D_MODEL)), full((PLE_DIM, D_MODEL))] + extra_specs,
        out_specs=out_specs,
        out_shape=out_shape,
        scratch_shapes=scratch,
        compiler_params=_params(3),
        name="tail",
    )(h, oa, ob, oc, wo, nf, wu, wu, cw, cw, wd, cs, cs, p, npl, wg, wp, *extra_in)


def _pack(prm):
    (norm_mix, w_in, dn_conv_w, dn_a_log, dn_dt_bias, dn_norm, ssm_lam_re, ssm_lam_im, ssm_log_step,
     ssm_b_re, ssm_b_im, ssm_c_re, ssm_c_im, ssm_d, ssm_glu_w, ssm_glu_b, lower_bounds, hg_norm, w_out,
     norm_ffn, ffn_w_up, ffn_conv_w, ffn_w_down, norm_ple, ple_w_gate, ple_w_proj) = prm
    zeros4 = jnp.zeros((DEPTH, DN_HEADS), F32)
    pad = jnp.zeros((DEPTH, BA_W - 2 * DN_HEADS), F32)
    coef = jnp.stack([jnp.concatenate([zeros4, -jnp.exp(dn_a_log.astype(F32)), pad], axis=-1),
                      jnp.concatenate([zeros4, dn_dt_bias.astype(F32), pad], axis=-1)], axis=1)

    lre = ssm_lam_re.astype(F32)
    lim = ssm_lam_im.astype(F32)
    delta = jnp.exp(ssm_log_step.astype(F32))[..., None]
    mag = jnp.exp(lre * delta)
    lbr = mag * jnp.cos(lim * delta)
    lbi = mag * jnp.sin(lim * delta)
    den = lre * lre + lim * lim
    fr = ((lbr - 1.0) * lre + lbi * lim) / den
    fi = (lbi * lre - (lbr - 1.0) * lim) / den
    bre = ssm_b_re.astype(F32)
    bim = ssm_b_im.astype(F32)
    bbr = fr[..., None] * bre - fi[..., None] * bim
    bbi = fr[..., None] * bim + fi[..., None] * bre
    eye_g = jnp.eye(SSM_GROUPS, dtype=F32)

    def bdiag_in(m):
        return jnp.einsum('dgph,gk->dghkp', m, eye_g).reshape(DEPTH, SSM_WIDTH, SSM_FLAT)

    def bdiag_out(m):
        return jnp.einsum('dghp,gk->dgpkh', m, eye_g).reshape(DEPTH, SSM_FLAT, SSM_WIDTH)

    bmat = jnp.concatenate([bdiag_in(bbr), bdiag_in(bbi)], axis=-1).astype(BF16)
    cmat = jnp.concatenate([bdiag_out(ssm_c_re.astype(F32)), -bdiag_out(ssm_c_im.astype(F32))],
                           axis=1).astype(BF16)
    lam = jnp.stack([lbr.reshape(DEPTH, SSM_FLAT), lbi.reshape(DEPTH, SSM_FLAT)], axis=1)

    gmat = jnp.kron(jnp.eye(HG_HEADS, dtype=F32), jnp.ones((HG_HD, HG_HD), F32)).astype(BF16)
    row = lambda a: a.reshape(DEPTH, 1, a.shape[-1])
    return dict(
        norm_mix=row(norm_mix), w_in=w_in, conv_w=dn_conv_w, coef=coef, dn_norm=row(dn_norm),
        bmat=bmat, lam=lam, cmat=cmat, ssm_d=row(ssm_d), glu_w=ssm_glu_w.astype(BF16), glu_b=row(ssm_glu_b),
        lb=row(lower_bounds), hg_norm=row(jnp.tile(hg_norm, (1, HG_HEADS))), gmat=gmat,
        w_out=w_out.astype(BF16), norm_ffn=row(norm_ffn), w_up=ffn_w_up.astype(BF16), ffn_conv_w=ffn_conv_w,
        w_down=ffn_w_down.astype(BF16), norm_ple=row(norm_ple), ple_gate=ple_w_gate.astype(BF16),
        ple_proj=ple_w_proj.astype(BF16))


def _tiles(bsz, t):
    if t >= DN_CHUNK:
        return dict(tok_tt=512 // bsz, tok_nb=bsz, tail_nb=bsz, dn_tt=DN_CHUNK, hg_tt=HG_CHUNK,
                    s5_steps=1024 // bsz, ffc=FF_DIM, n_sub=1, dn_group=4, hg_group=8)
    return dict(tok_tt=t, tok_nb=512 // t, tail_nb=256 // t, dn_tt=t, hg_tt=t, s5_steps=t, ffc=FF_DIM // 2,
                n_sub=2, dn_group=8, hg_group=8)


def _trunk(x, p, conv_qkv, delta, ssm_re, ssm_im, hgrn, conv_ffn, lp, norm_final):
    bsz, t, _ = x.shape
    tl = _tiles(bsz, t)
    sr_l, si_l, cf_l = [], [], []
    cq_prev, dn_prev, hg_prev = (), (), ()
    h = x
    for i in range(DEPTH):
        res = _proj_in(h, lp['norm_mix'], lp['w_in'], lp['conv_w'], conv_qkv, cq_prev, i, bsz, t,
                       tl['tok_tt'], tl['tok_nb'], from_btd=(i == 0))
        if i == 0:
            h = res[6]
        qkv, zg, zu, zh, zba, cq = res[:6]
        cq_prev = (cq,)
        o_a, dl = _delta_call(qkv, zg, zba, lp['coef'], lp['dn_norm'], delta, dn_prev, i, bsz, t,
                              tl['dn_tt'], tl['dn_group'])
        dn_prev = (dl,)
        o_b, sr, si = _s5_call(zu, lp['bmat'], lp['lam'], lp['cmat'], lp['ssm_d'], lp['glu_w'], lp['glu_b'],
                               ssm_re[i].reshape(bsz, SSM_FLAT), ssm_im[i].reshape(bsz, SSM_FLAT),
                               i, bsz, t, tl['s5_steps'])
        o_c, hg = _hgrn_call(zh, lp['lb'], lp['hg_norm'], lp['gmat'], hgrn, hg_prev, i, bsz, t, tl['hg_tt'],
                             tl['hg_group'])
        hg_prev = (hg,)
        res = _tail_call(h, o_a, o_b, o_c, lp['w_out'], lp['norm_ffn'], lp['w_up'], lp['ffn_conv_w'],
                         lp['w_down'], conv_ffn, p, lp['norm_ple'], lp['ple_gate'], lp['ple_proj'],
                         norm_final if i == DEPTH - 1 else None,
                         i, bsz, t, tl['tok_tt'], tl['tail_nb'], tl['ffc'], tl['n_sub'])
        h, cfa, cfb = res
        sr_l.append(sr.reshape(bsz, SSM_GROUPS, SSM_STATE))
        si_l.append(si.reshape(bsz, SSM_GROUPS, SSM_STATE))
        cf_l.append(jnp.swapaxes(jnp.concatenate([cfa[-1], cfb[-1]], axis=-1), 0, 1))
    y = h
    return (y, cq, dl, jnp.stack(sr_l), jnp.stack(si_l), hg, jnp.stack(cf_l))


def kernel(x_prompt, x_sample, p_prompt, p_sample, state_conv_qkv, state_delta, state_ssm_re, state_ssm_im, state_hgrn, state_conv_ffn, norm_mix, w_in, dn_conv_w, dn_a_log, dn_dt_bias, dn_norm, ssm_lam_re, ssm_lam_im, ssm_log_step, ssm_b_re, ssm_b_im, ssm_c_re, ssm_c_im, ssm_d, ssm_glu_w, ssm_glu_b, hg_lower, hg_norm, w_out, norm_ffn, ffn_w_up, ffn_conv_w, ffn_w_down, norm_ple, ple_w_gate, ple_w_proj, norm_final):
    lb_p = jax.nn.softmax(hg_lower.astype(F32), axis=0)
    lower_bounds = jnp.cumsum(lb_p, axis=0) - lb_p[0]
    prm = (norm_mix, w_in, dn_conv_w, dn_a_log, dn_dt_bias, dn_norm, ssm_lam_re, ssm_lam_im, ssm_log_step,
           ssm_b_re, ssm_b_im, ssm_c_re, ssm_c_im, ssm_d, ssm_glu_w, ssm_glu_b, lower_bounds, hg_norm, w_out,
           norm_ffn, ffn_w_up, ffn_conv_w, ffn_w_down, norm_ple, ple_w_gate, ple_w_proj)
    layers = _pack(prm)
    nf = norm_final.reshape(1, D_MODEL)

    bp = x_prompt.shape[0]
    z = lambda *shp: jnp.zeros((DEPTH, bp) + shp, F32)
    prompt = _trunk(x_prompt, p_prompt, z(DN_CONV - 1, QKV_W), z(DN_HEADS, DN_HD, DN_HD),
                    z(SSM_GROUPS, SSM_STATE), z(SSM_GROUPS, SSM_STATE), z(HG_HEADS, HG_HD, HG_HD),
                    z(FF_CONV - 1, 2 * FF_DIM), layers, nf)
    sample = _trunk(x_sample, p_sample, state_conv_qkv, state_delta, state_ssm_re, state_ssm_im, state_hgrn,
                    state_conv_ffn, layers, nf)
    return (prompt[0], sample[0]) + prompt[1:] + sample[1:]
```

```python
import functools
import math

import jax
import jax.numpy as jnp
from jax import lax
from jax.experimental import pallas as pl
from jax.experimental.pallas import tpu as pltpu

F32 = jnp.float32
BF16 = jnp.bfloat16

D_MODEL = 1024
DEPTH = 2
DN_HEADS = 4
DN_WIDTH = 512
DN_HD = 128
DN_CONV = 4
DN_CHUNK = 64
SSM_WIDTH = 256
SSM_GROUP = 16
SSM_GROUPS = 16
SSM_STATE = 64
SSM_FLAT = SSM_GROUPS * SSM_STATE
HG_WIDTH = 256
HG_HEADS = 4
HG_HD = 64
HG_CHUNK = 32
FF_DIM = 2816
FF_CONV = 3
PLE_DIM = 256
EPS = 1e-6

SUBLANES = 8
LANES = 128

QKV_W = 3 * DN_WIDTH
ZH_W = 4 * HG_WIDTH
BA_W = LANES
IN_PACKED = QKV_W + DN_WIDTH + SSM_WIDTH + ZH_W + BA_W
IN_RAW = QKV_W + DN_WIDTH + 2 * DN_HEADS + SSM_WIDTH + ZH_W
_Z_WIDTHS = (QKV_W, DN_WIDTH, SSM_WIDTH, ZH_W, BA_W)

VMEM_LIMIT = 56 * 1024 * 1024

_NT = (((1,), (1,)), ((), ()))
_TN = (((0,), (0,)), ((), ()))


def _dot(a, b):
    return jnp.dot(a.astype(BF16), b.astype(BF16), preferred_element_type=F32)


def _dot_nt(a, b):
    return lax.dot_general(a.astype(BF16), b.astype(BF16), _NT, preferred_element_type=F32)


def _dot_tn(a, b):
    return lax.dot_general(a.astype(BF16), b.astype(BF16), _TN, preferred_element_type=F32)


def _split3(x):
    hi = x.astype(BF16)
    r1 = x - hi.astype(F32)
    mid = r1.astype(BF16)
    lo = (r1 - mid.astype(F32)).astype(BF16)
    return hi, mid, lo


def _dot_exact(x, m, dims, x_first=True):
    if x_first:
        return sum(lax.dot_general(p, m, dims, preferred_element_type=F32) for p in _split3(x))
    return sum(lax.dot_general(m, p, dims, preferred_element_type=F32) for p in _split3(x))


_MM = (((1,), (0,)), ((), ()))


def _sigmoid_exp(x):
    return 1.0 / (1.0 + jnp.exp(-x))


def _sigmoid(x):
    return 0.5 * jnp.tanh(0.5 * x) + 0.5


def _silu(x):
    h = 0.5 * x
    return h + h * jnp.tanh(h)


def _layer_spec(layer, block, index, **kw):
    return pl.BlockSpec((None,) + tuple(block), lambda *g: (layer,) + tuple(index(*g)), **kw)


def _layer_full(layer, shape, **kw):
    return _layer_spec(layer, shape, lambda *g: (0,) * len(shape), **kw)


_ANY = pl.BlockSpec(memory_space=pl.ANY)


def _softplus(x):
    return jnp.maximum(x, 0.0) + jnp.log1p(jnp.exp(-jnp.abs(x)))


def _rms(x, g):
    ms = jnp.mean(x * x, axis=-1, keepdims=True)
    return x * lax.rsqrt(ms + EPS) * g


def _seq_tri(n, nb):
    r = lax.broadcasted_iota(jnp.int32, (n, n), 0)
    c = lax.broadcasted_iota(jnp.int32, (n, n), 1)
    return jnp.where(((r & (nb - 1)) == (c & (nb - 1))) & (c <= r), 1.0, 0.0).astype(BF16)


def _params(n_axes):
    return pltpu.CompilerParams(dimension_semantics=("arbitrary",) * n_axes, vmem_limit_bytes=VMEM_LIMIT)


def _proj_in_body(*refs, tt, nb, n_tiles, from_btd, n_alias):
    x_ref, g_ref, wraw_ref, cw_ref, cs_ref = refs[:5]
    outs = refs[5 + n_alias:]
    qkv_ref, zg_ref, zu_ref, zh_ref, zba_ref, cso_ref = outs[:6]
    buf, w_ref = outs[-2:]
    ti = pl.program_id(1)
    n = tt * nb
    hist = DN_CONV - 1

    @pl.when((pl.program_id(0) == 0) & (ti == 0))
    def _():
        o_ba = QKV_W + DN_WIDTH
        n_ba = 2 * DN_HEADS
        n_rest = SSM_WIDTH + ZH_W
        w_ref[:, 0:o_ba] = wraw_ref[:, 0:o_ba].astype(BF16)
        w_ref[:, o_ba:o_ba + n_rest] = wraw_ref[:, o_ba + n_ba:o_ba + n_ba + n_rest].astype(BF16)
        w_ref[:, o_ba + n_rest:IN_PACKED] = jnp.concatenate(
            [wraw_ref[:, o_ba:o_ba + n_ba], jnp.zeros((D_MODEL, BA_W - n_ba), F32)], axis=1).astype(BF16)
    if from_btd:
        h0 = outs[6]

        def cp(b, c):
            h0[:, b, :] = x_ref[b]
            return c

        lax.fori_loop(0, nb, cp, 0)
        x = h0[...].reshape(n, D_MODEL)
    else:
        x = x_ref[...].reshape(n, D_MODEL)
    xn = _rms(x, g_ref[...]).astype(BF16)

    @pl.when(ti == 0)
    def _():
        def cp(b, c):
            buf[:, b, :] = cs_ref[b]
            return c
        lax.fori_loop(0, nb, cp, 0)

    n_sub = 4
    st = tt // n_sub
    sr = st * nb
    hr = hist * nb
    cw = cw_ref[...]
    prev = buf[...].reshape(hr, QKV_W)
    for sb in range(n_sub):
        xs = xn[sb * sr:(sb + 1) * sr]
        ts = slice(sb * st, (sb + 1) * st)
        xcat = jnp.concatenate([prev, jnp.dot(xs, w_ref[:, 0:QKV_W], preferred_element_type=F32)], axis=0)
        c0 = QKV_W
        for ref, width in zip((zg_ref, zu_ref, zh_ref, zba_ref), _Z_WIDTHS[1:]):
            ref[ts] = jnp.dot(xs, w_ref[:, c0:c0 + width], preferred_element_type=F32).reshape(st, nb, width)
            c0 += width
        y = cw[0:1] * xcat[0:sr]
        for j in range(1, DN_CONV):
            y = y + cw[j:j + 1] * xcat[j * nb:j * nb + sr]
        prev = xcat[sr:sr + hr]
        y = _silu(y)
        for h in range(DN_HEADS):
            qs = slice(h * DN_HD, (h + 1) * DN_HD)
            ks = slice(DN_WIDTH + h * DN_HD, DN_WIDTH + (h + 1) * DN_HD)
            qh = y[:, qs]
            kh = y[:, ks]
            qn = qh * (lax.rsqrt(jnp.sum(qh * qh, axis=-1, keepdims=True) + EPS) * DN_HD ** -0.5)
            kn = kh * lax.rsqrt(jnp.sum(kh * kh, axis=-1, keepdims=True) + EPS)
            qkv_ref[ts, :, qs] = qn.reshape(st, nb, DN_HD)
            qkv_ref[ts, :, ks] = kn.reshape(st, nb, DN_HD)
        qkv_ref[ts, :, 2 * DN_WIDTH:QKV_W] = y[:, 2 * DN_WIDTH:QKV_W].reshape(st, nb, DN_WIDTH)
    buf[...] = prev.reshape(hist, nb, QKV_W)

    @pl.when(ti == n_tiles - 1)
    def _():
        def cp(b, c):
            cso_ref[b] = buf[:, b, :]
            return c
        lax.fori_loop(0, nb, cp, 0)


def _proj_in(x, g, w, cw, cs, prev, layer, bsz, t, tt, nb, from_btd):
    n_tiles = t // tt
    hist = DN_CONV - 1
    tm = lambda wd: pl.BlockSpec((tt, nb, wd), lambda j, i: (i, j, 0))
    x_spec = pl.BlockSpec((nb, tt, D_MODEL), lambda j, i: (j, i, 0)) if from_btd else tm(D_MODEL)
    cs_spec = _layer_spec(layer, (nb, hist, QKV_W), lambda j, i: (j, 0, 0))
    out_shape = [jax.ShapeDtypeStruct((t, bsz, wd), F32) for wd in _Z_WIDTHS]
    out_specs = [tm(wd) for wd in _Z_WIDTHS]
    out_shape.append(jax.ShapeDtypeStruct((DEPTH, bsz, hist, QKV_W), F32))
    out_specs.append(cs_spec)
    if from_btd:
        out_shape.append(jax.ShapeDtypeStruct((t, bsz, D_MODEL), F32))
        out_specs.append(tm(D_MODEL))
    n_in = 5
    return pl.pallas_call(
        functools.partial(_proj_in_body, tt=tt, nb=nb, n_tiles=n_tiles, from_btd=from_btd, n_alias=len(prev)),
        grid=(bsz // nb, n_tiles),
        in_specs=[x_spec, _layer_full(layer, (1, D_MODEL)),
                  _layer_full(layer, (D_MODEL, IN_RAW), pipeline_mode=pl.Buffered(1)),
                  _layer_full(layer, (DN_CONV, QKV_W)), cs_spec] + [_ANY] * len(prev),
        out_specs=out_specs,
        out_shape=out_shape,
        input_output_aliases={n_in + k: 5 + k for k in range(len(prev))},
        scratch_shapes=[pltpu.VMEM((hist, nb, QKV_W), F32), pltpu.VMEM((D_MODEL, IN_PACKED), BF16)],
        compiler_params=_params(2),
        name="proj_in",
    )(x, g, w, cw, cs, *prev)


def _delta_body(*refs, tt, group, n_alias):
    qkv_ref, zg_ref, zba_ref, coef_ref, dnw_ref, s0_ref = refs[:6]
    o_ref, s_ref, qb, kb, vb, pb, gcb, ob = refs[6 + n_alias:]
    ti = pl.program_id(1)
    nb = SUBLANES
    n = tt * nb
    levels = int(math.log2(tt))

    @pl.when(ti == 0)
    def _():
        s_ref[...] = s0_ref[...]

    for h in range(DN_HEADS):
        for dst, off in ((qb, 0), (kb, DN_WIDTH), (vb, 2 * DN_WIDTH)):
            dst[h] = qkv_ref[:, :, off + h * DN_HD:off + (h + 1) * DN_HD].reshape(n, DN_HD)

    zba = zba_ref[...].reshape(n, BA_W)
    coef = coef_ref[...]
    lane = lax.broadcasted_iota(jnp.int32, zba.shape, 1)
    g = coef[0:1] * _softplus(zba + coef[1:2])
    pb[...] = jnp.where(lane < DN_HEADS, _sigmoid(zba), g)
    gcb[...] = _dot_exact(g, _seq_tri(n, nb), _MM, x_first=False)

    ri = lax.broadcasted_iota(jnp.int32, (tt, tt), 0)
    ci = lax.broadcasted_iota(jnp.int32, (tt, tt), 1)
    causal = ci <= ri
    strict = ci < ri
    eye = jnp.where(ci == ri, 1.0, 0.0).astype(F32)
    sel = jnp.where(lax.broadcasted_iota(jnp.int32, (SUBLANES, BA_W), 0)
                    == lax.broadcasted_iota(jnp.int32, (SUBLANES, BA_W), 1), 1.0, 0.0).astype(BF16)

    def seq_group(bp, carry_):
        chains = []
        for s in range(group):
            b = bp * group + s
            seq_rows = pl.ds(b, tt, stride=nb)
            pbv = pb[seq_rows, :]
            gcv = gcb[seq_rows, :]
            gct = _dot_exact(gcv, sel, _NT, x_first=False)
            for h in range(DN_HEADS):
                chains.append(dict(b=b, h=h, rows=seq_rows, q=qb[h, seq_rows, :], k=kb[h, seq_rows, :],
                                   v=vb[h, seq_rows, :],
                                   beta=pbv[:, h:h + 1], gcol=gcv[:, DN_HEADS + h:DN_HEADS + h + 1],
                                   grow=gct[DN_HEADS + h:DN_HEADS + h + 1, :]))
        for c in chains:
            c['dec'] = jnp.where(causal, jnp.exp(jnp.where(causal, c['gcol'] - c['grow'], 0.0)), 0.0)
            c['qkk'] = _dot_nt(jnp.concatenate([c['q'], c['k']], axis=0), c['k'])
        for c in chains:
            c['p'] = -jnp.where(strict, c['beta'] * c['qkk'][tt:2 * tt] * c['dec'], 0.0)
            c['t'] = eye + c['p']
        for _ in range(levels - 1):
            for c in chains:
                c['p'] = _dot(c['p'], c['p'])
            for c in chains:
                c['t'] = c['t'] + _dot(c['p'], c['t'])
        for c in chains:
            eg = jnp.exp(c['gcol'])
            rhs = jnp.concatenate([c['beta'] * c['v'], (c['beta'] * eg) * c['k']], axis=1)
            c['uw'] = _dot(c['t'], rhs)
            c['qd'] = c['q'] * eg
            c['glast'] = c['grow'][:, tt - 1:tt]
            c['kd'] = c['k'] * jnp.exp(c['glast'] - c['gcol'])
            c['s'] = s_ref[c['b'], c['h']]
        for c in chains:
            c['r'] = _dot(jnp.concatenate([c['uw'][:, DN_HD:2 * DN_HD], c['qd']], axis=0), c['s'])
        for c in chains:
            c['vn'] = c['uw'][:, 0:DN_HD] - c['r'][0:tt]
        for c in chains:
            ob[c['h'], c['rows'], :] = c['r'][tt:2 * tt] + _dot(c['qkk'][0:tt] * c['dec'], c['vn'])
            s_ref[c['b'], c['h']] = c['s'] * jnp.exp(c['glast']) + _dot_tn(c['kd'], c['vn'])
        return carry_

    lax.fori_loop(0, nb // group, seq_group, 0)

    gate = zg_ref[...].reshape(n, DN_WIDTH)
    dnw = dnw_ref[...]
    for h in range(DN_HEADS):
        hs = slice(h * DN_HD, (h + 1) * DN_HD)
        o_ref[:, :, hs] = (_rms(ob[h], dnw) * _silu(gate[:, hs])).reshape(tt, nb, DN_HD)


def _delta_call(qkv, zg, zba, coef, dnw, s0, prev, layer, bsz, t, tt, group):
    nb = SUBLANES
    tm = lambda wd: pl.BlockSpec((tt, nb, wd), lambda j, i: (i, j, 0))
    st_spec = _layer_spec(layer, (nb, DN_HEADS, DN_HD, DN_HD), lambda j, i: (j, 0, 0, 0))
    heads = pltpu.VMEM((DN_HEADS, tt * nb, DN_HD), F32)
    rows = pltpu.VMEM((tt * nb, BA_W), F32)
    n_in = 6
    return pl.pallas_call(
        functools.partial(_delta_body, tt=tt, group=group, n_alias=len(prev)),
        grid=(bsz // nb, t // tt),
        in_specs=[tm(QKV_W), tm(DN_WIDTH), tm(BA_W), _layer_full(layer, (2, BA_W)),
                  _layer_full(layer, (1, DN_HD)), st_spec] + [_ANY] * len(prev),
        out_specs=[tm(DN_WIDTH), st_spec],
        out_shape=[jax.ShapeDtypeStruct((t, bsz, DN_WIDTH), F32),
                   jax.ShapeDtypeStruct((DEPTH, bsz, DN_HEADS, DN_HD, DN_HD), F32)],
        input_output_aliases={n_in + k: 1 + k for k in range(len(prev))},
        scratch_shapes=[heads, heads, heads, rows, rows, heads],
        compiler_params=_params(2),
        name="delta",
    )(qkv, zg, zba, coef, dnw, s0, *prev)


def _hgrn_body(*refs, tt, n_tiles, group, n_alias):
    zh_ref, lb_ref, nw_ref, gm_ref, s0_ref = refs[:5]
    o_ref, s_ref, sbd, qsb, ksb, qeb, kdb, vb, elb, ob = refs[5 + n_alias:]
    ti = pl.program_id(1)
    nb = SUBLANES
    n = tt * nb
    mid = tt // 2 - 1
    sh_h = int(math.log2(HG_HD))

    @pl.when(ti == 0)
    def _():
        zero = jnp.zeros((HG_HD, HG_HD), F32)

        def init(b, c):
            for h in range(HG_HEADS):
                parts = [zero] * HG_HEADS
                parts[h] = s0_ref[b, h].T
                sbd[b, h * HG_HD:(h + 1) * HG_HD, :] = jnp.concatenate(parts, axis=1)
            return c
        for b in range(nb):
            init(b, 0)

    zh = zh_ref[...].reshape(n, ZH_W)
    lb = lb_ref[...]
    f = lb + (1.0 - lb) * _sigmoid_exp(zh[:, HG_WIDTH:2 * HG_WIDTH])
    q = _silu(zh[:, 0:HG_WIDTH]).reshape(tt, nb, HG_WIDTH)
    k = (1.0 - f).reshape(tt, nb, HG_WIDTH)
    bc = _dot_exact(jnp.log(f), _seq_tri(n, nb), _MM, x_first=False).reshape(tt, nb, HG_WIDTH)
    bm = bc[mid]
    bl = bc[tt - 1]
    half = HG_WIDTH // 2
    qs = q * jnp.exp(bc - bm[None])
    ks = k * jnp.exp(bm[None] - bc)
    for dst, val in ((qsb, qs), (ksb, ks), (qeb, qs * jnp.exp(bm)[None]), (kdb, ks * jnp.exp(bl - bm)[None])):
        val = val.reshape(n, HG_WIDTH)
        dst[0] = val[:, 0:half]
        dst[1] = val[:, half:HG_WIDTH]
    vb[0] = zh[:, 2 * HG_WIDTH:2 * HG_WIDTH + half]
    vb[1] = zh[:, 2 * HG_WIDTH + half:3 * HG_WIDTH]
    elb[...] = jnp.exp(bl)

    ri = lax.broadcasted_iota(jnp.int32, (HG_HEADS * tt, tt), 0)
    ci = lax.broadcasted_iota(jnp.int32, (HG_HEADS * tt, tt), 1)
    causal = ci <= (ri & (tt - 1))
    lane_head = lax.broadcasted_iota(jnp.int32, (tt, HG_WIDTH), 1) >> sh_h
    hmask = [jnp.where(lane_head == h, 1.0, 0.0).astype(F32) for h in range(HG_HEADS)]
    bd = ((lax.broadcasted_iota(jnp.int32, (HG_WIDTH, HG_WIDTH), 0) >> sh_h)
          == (lax.broadcasted_iota(jnp.int32, (HG_WIDTH, HG_WIDTH), 1) >> sh_h))

    def seq_group(bp, carry_):
        seqs = []
        for s in range(group):
            b = bp * group + s
            seq_rows = pl.ds(b, tt, stride=nb)
            seq = lambda ref: jnp.concatenate([ref[0, seq_rows, :], ref[1, seq_rows, :]], axis=1)
            seqs.append(dict(b=b, rows=seq_rows, qs=seq(qsb), ks=seq(ksb), qe=seq(qeb), kd=seq(kdb),
                             v=seq(vb), st=sbd[b], el=elb[pl.ds(b, 1), :]))
        for c in seqs:
            c['a'] = jnp.where(causal, _dot_nt(jnp.concatenate([c['qs'] * m for m in hmask], axis=0), c['ks']), 0.0)
            c['o'] = _dot_nt(c['qe'], c['st'])
            c['kv'] = _dot_tn(c['v'], c['kd'])
        for c in seqs:
            o = c['o']
            for h in range(HG_HEADS):
                o = o + hmask[h] * _dot(c['a'][h * tt:(h + 1) * tt], c['v'])
            ob[0, c['rows'], :] = o[:, 0:half]
            ob[1, c['rows'], :] = o[:, half:HG_WIDTH]
            sbd[c['b']] = c['st'] * c['el'] + jnp.where(bd, c['kv'], 0.0)
        return carry_

    lax.fori_loop(0, nb // group, seq_group, 0)

    o = jnp.concatenate([ob[0], ob[1]], axis=1)
    ms = _dot_exact(o * o, gm_ref[...], _MM) * (1.0 / HG_HD)
    o_ref[...] = (o * lax.rsqrt(ms + EPS) * nw_ref[...]
                  * _silu(zh[:, 3 * HG_WIDTH:4 * HG_WIDTH])).reshape(tt, nb, HG_WIDTH)

    @pl.when(ti == n_tiles - 1)
    def _():
        def fin(b, c):
            for h in range(HG_HEADS):
                blk = sbd[b, h * HG_HD:(h + 1) * HG_HD, :]
                s_ref[b, h] = blk[:, h * HG_HD:(h + 1) * HG_HD].T
            return c
        for b in range(nb):
            fin(b, 0)


def _hgrn_call(zh, lb, nw, gm, s0, prev, layer, bsz, t, tt, group):
    nb = SUBLANES
    n_tiles = t // tt
    st_spec = _layer_spec(layer, (nb, HG_HEADS, HG_HD, HG_HD), lambda j, i: (j, 0, 0, 0))
    tile = pltpu.VMEM((HG_WIDTH // LANES, tt * nb, LANES), F32)
    n_in = 5
    return pl.pallas_call(
        functools.partial(_hgrn_body, tt=tt, n_tiles=n_tiles, group=group, n_alias=len(prev)),
        grid=(bsz // nb, n_tiles),
        in_specs=[pl.BlockSpec((tt, nb, ZH_W), lambda j, i: (i, j, 0)),
                  _layer_full(layer, (1, HG_WIDTH)), _layer_full(layer, (1, HG_WIDTH)),
                  pl.BlockSpec((HG_WIDTH, HG_WIDTH), lambda j, i: (0, 0)), st_spec] + [_ANY] * len(prev),
        out_specs=[pl.BlockSpec((tt, nb, HG_WIDTH), lambda j, i: (i, j, 0)), st_spec],
        out_shape=[jax.ShapeDtypeStruct((t, bsz, HG_WIDTH), F32),
                   jax.ShapeDtypeStruct((DEPTH, bsz, HG_HEADS, HG_HD, HG_HD), F32)],
        input_output_aliases={n_in + k: 1 + k for k in range(len(prev))},
        scratch_shapes=[pltpu.VMEM((nb, HG_WIDTH, HG_WIDTH), F32),
                        tile, tile, tile, tile, tile, pltpu.VMEM((nb, HG_WIDTH), F32), tile],
        compiler_params=_params(2),
        name="hgrn",
    )(zh, lb, nw, gm, s0, *prev)


def _s5_body(u_ref, bh_ref, lam_ref, c_ref, d_ref, gw_ref, gb_ref, x0r_ref, x0i_ref,
             o_ref, xr_ref, xi_ref, *, bsz, steps):
    i = pl.program_id(0)

    @pl.when(i == 0)
    def _():
        xr_ref[...] = x0r_ref[...]
        xi_ref[...] = x0i_ref[...]

    u = u_ref[...].reshape(steps * bsz, SSM_WIDTH)
    bu = _dot(u, bh_ref[...])

    lr = jnp.broadcast_to(lam_ref[0:1, :], (SUBLANES, SSM_FLAT))
    li = jnp.broadcast_to(lam_ref[1:2, :], (SUBLANES, SSM_FLAT))

    n_rb = bsz // SUBLANES
    xr_all = xr_ref[...]
    xi_all = xi_ref[...]
    scanned = [[None] * n_rb for _ in range(steps)]
    last_r, last_i = [], []
    for rb in range(n_rb):
        xr = xr_all[rb * SUBLANES:(rb + 1) * SUBLANES]
        xi = xi_all[rb * SUBLANES:(rb + 1) * SUBLANES]
        for t in range(steps):
            r0 = t * bsz + rb * SUBLANES
            xr, xi = (lr * xr - li * xi + bu[r0:r0 + SUBLANES, 0:SSM_FLAT],
                      lr * xi + li * xr + bu[r0:r0 + SUBLANES, SSM_FLAT:2 * SSM_FLAT])
            scanned[t][rb] = jnp.concatenate([xr, xi], axis=1)
        last_r.append(xr)
        last_i.append(xi)
    xr_ref[...] = jnp.concatenate(last_r, axis=0)
    xi_ref[...] = jnp.concatenate(last_i, axis=0)
    xs = jnp.concatenate([scanned[t][rb] for t in range(steps) for rb in range(n_rb)], axis=0)

    y = _dot(xs, c_ref[...]) + d_ref[...] * u
    y = 0.5 * y * (1.0 + jnp.tanh(math.sqrt(2.0 / math.pi) * (y + 0.044715 * (y * y * y))))
    o_ref[...] = (y * _sigmoid(_dot(y, gw_ref[...]) + gb_ref[...])).reshape(steps, bsz, SSM_WIDTH)


def _s5_call(u, bh, lam, cm, d, gw, gb, x0r, x0i, layer, bsz, t, steps):
    rows = steps * bsz
    full = lambda shp: pl.BlockSpec(shp, lambda i: tuple(0 for _ in shp))
    lfull = lambda shp: _layer_full(layer, shp)
    return pl.pallas_call(
        functools.partial(_s5_body, bsz=bsz, steps=steps),
        grid=(t // steps,),
        in_specs=[pl.BlockSpec((steps, bsz, SSM_WIDTH), lambda i: (i, 0, 0)),
                  lfull((SSM_WIDTH, 2 * SSM_FLAT)),
                  lfull((2, SSM_FLAT)), lfull((2 * SSM_FLAT, SSM_WIDTH)), lfull((1, SSM_WIDTH)),
                  lfull((SSM_WIDTH, SSM_WIDTH)), lfull((1, SSM_WIDTH)),
                  full((bsz, SSM_FLAT)), full((bsz, SSM_FLAT))],
        out_specs=[pl.BlockSpec((steps, bsz, SSM_WIDTH), lambda i: (i, 0, 0)),
                   full((bsz, SSM_FLAT)), full((bsz, SSM_FLAT))],
        out_shape=[jax.ShapeDtypeStruct((t, bsz, SSM_WIDTH), F32),
                   jax.ShapeDtypeStruct((bsz, SSM_FLAT), F32),
                   jax.ShapeDtypeStruct((bsz, SSM_FLAT), F32)],
        compiler_params=_params(1),
        name="s5",
    )(u, bh, lam, cm, d, gw, gb, x0r, x0i)


def _tail_body(*refs, tt, nb, n_ffc, ffc, n_sub, emit_final):
    (h_ref, oa_ref, ob_ref, oc_ref, wo_ref, nf_ref, wua_ref, wub_ref, cwa_ref, cwb_ref,
     wd_ref, csa_ref, csb_ref, p_ref, npl_ref, wg_ref, wp_ref) = refs[:17]
    rest = refs[17:]
    if emit_final:
        fin_ref, y_ref, cso_a_ref, cso_b_ref, hn, p3, car_a, car_b, ys, out_ref = rest
    else:
        out_ref, cso_a_ref, cso_b_ref, hn, p3, car_a, car_b = rest
    i = pl.program_id(1)
    j = pl.program_id(2)
    n = tt * nb
    hist = FF_CONV - 1
    st = tt // n_sub
    sr = st * nb
    hr = hist * nb

    @pl.when(j == 0)
    def _():
        h2 = (h_ref[...].reshape(n, D_MODEL) + _dot(oa_ref[...].reshape(n, DN_WIDTH), wo_ref[0:DN_WIDTH, :])
              + _dot(ob_ref[...].reshape(n, SSM_WIDTH), wo_ref[DN_WIDTH:DN_WIDTH + SSM_WIDTH, :])
              + _dot(oc_ref[...].reshape(n, HG_WIDTH), wo_ref[DN_WIDTH + SSM_WIDTH:D_MODEL, :]))
        out_ref[...] = h2.reshape(tt, nb, D_MODEL)
        hn[...] = _rms(h2, nf_ref[...]).astype(BF16)

    @pl.when(i == 0)
    def _():
        def cp(b, c):
            car_a[j, :, b, :] = csa_ref[b]
            car_b[j, :, b, :] = csb_ref[b]
            return c
        lax.fori_loop(0, nb, cp, 0)

    prev = [car_a[j].reshape(hr, ffc), car_b[j].reshape(hr, ffc)]
    wd = wd_ref[...]
    for sb in range(n_sub):
        hs = hn[sb * sr:(sb + 1) * sr, :]
        halves = []
        for idx, (wu, cw) in enumerate(((wua_ref, cwa_ref), (wub_ref, cwb_ref))):
            x = jnp.concatenate([prev[idx], jnp.dot(hs, wu[...], preferred_element_type=F32)], axis=0)
            w = cw[...]
            acc = w[0:1] * x[0:sr]
            for s in range(1, FF_CONV):
                acc = acc + w[s:s + 1] * x[s * nb:s * nb + sr]
            halves.append(acc)
            prev[idx] = x[sr:sr + hr]
        out_ref[sb * st:(sb + 1) * st] += _dot(_silu(halves[0]) * halves[1], wd).reshape(st, nb, D_MODEL)
    for car, cso, last in ((car_a, cso_a_ref, prev[0]), (car_b, cso_b_ref, prev[1])):
        last = last.reshape(hist, nb, ffc)
        car[j] = last
        cso[...] = last

    @pl.when(j == n_ffc - 1)
    def _():
        def cp(b, c):
            p3[:, b, :] = p_ref[b]
            return c
        lax.fori_loop(0, nb, cp, 0)
        h3 = out_ref[...].reshape(n, D_MODEL)
        gate = _sigmoid(_dot(_rms(h3, npl_ref[...]), wg_ref[...]))
        h4 = h3 + gate * _dot(p3[...].reshape(n, PLE_DIM), wp_ref[...])
        out_ref[...] = h4.reshape(tt, nb, D_MODEL)
        if emit_final:
            yv = _rms(h4, fin_ref[...])
            n_slab = D_MODEL // LANES
            for s in range(n_slab):
                ys[s] = yv[:, s * LANES:(s + 1) * LANES]

            def seq(b, c):
                rows = pl.ds(b, tt, stride=nb)
                y_ref[b] = jnp.concatenate([ys[s, rows, :] for s in range(n_slab)], axis=1)
                return c
            lax.fori_loop(0, nb, seq, 0)


def _tail_call(h, oa, ob, oc, wo, nf, wu, cw, wd, cs, p, npl, wg, wp, fin, layer, bsz, t, tt, nb, ffc, n_sub):
    n_ffc = FF_DIM // ffc
    n_t = t // tt
    hist = FF_CONV - 1
    tm = lambda wd_: pl.BlockSpec((tt, nb, wd_), lambda jb, i, j: (i, jb, 0))
    full = lambda shp: _layer_full(layer, shp, pipeline_mode=pl.Buffered(1))
    lyr = lambda blk, idx: _layer_spec(layer, blk, idx)
    wchunk = (lambda blk, idx: _layer_spec(layer, blk, idx, pipeline_mode=pl.Buffered(1))) if n_ffc == 1 else lyr
    cs_a = wchunk((nb, hist, ffc), lambda jb, i, j: (jb, 0, j))
    cs_b = wchunk((nb, hist, ffc), lambda jb, i, j: (jb, 0, n_ffc + j))
    cso = pl.BlockSpec((None, hist, nb, ffc), lambda jb, i, j: (i, 0, jb, j))
    scratch = [pltpu.VMEM((tt * nb, D_MODEL), BF16),
               pltpu.VMEM((tt, nb, PLE_DIM), F32),
               pltpu.VMEM((n_ffc, hist, nb, ffc), F32),
               pltpu.VMEM((n_ffc, hist, nb, ffc), F32)]
    emit_final = fin is not None
    out_specs = [tm(D_MODEL), cso, cso]
    out_shape = [jax.ShapeDtypeStruct((t, bsz, D_MODEL), F32),
                 jax.ShapeDtypeStruct((n_t, hist, bsz, FF_DIM), F32),
                 jax.ShapeDtypeStruct((n_t, hist, bsz, FF_DIM), F32)]
    extra_in, extra_specs = [], []
    if emit_final:
        extra_in = [fin]
        extra_specs = [pl.BlockSpec((1, D_MODEL), lambda jb, i, j: (0, 0))]
        out_specs[0] = pl.BlockSpec((nb, tt, D_MODEL), lambda jb, i, j: (jb, i, 0))
        out_shape[0] = jax.ShapeDtypeStruct((bsz, t, D_MODEL), F32)
        scratch += [pltpu.VMEM((D_MODEL // LANES, tt * nb, LANES), F32), pltpu.VMEM((tt, nb, D_MODEL), F32)]
    return pl.pallas_call(
        functools.partial(_tail_body, tt=tt, nb=nb, n_ffc=n_ffc, ffc=ffc, n_sub=n_sub, emit_final=emit_final),
        grid=(bsz // nb, n_t, n_ffc),
        in_specs=[tm(D_MODEL), tm(DN_WIDTH), tm(SSM_WIDTH), tm(HG_WIDTH),
                  full((D_MODEL, D_MODEL)), full((1, D_MODEL)),
                  wchunk((D_MODEL, ffc), lambda jb, i, j: (0, j)),
                  wchunk((D_MODEL, ffc), lambda jb, i, j: (0, n_ffc + j)),
                  lyr((FF_CONV, ffc), lambda jb, i, j: (0, j)),
                  lyr((FF_CONV, ffc), lambda jb, i, j: (0, n_ffc + j)),
                  wchunk((ffc, D_MODEL), lambda jb, i, j: (j, 0)),
                  cs_a, cs_b,
                  lyr((nb, tt, PLE_DIM), lambda jb, i, j: (jb, i, 0)),
                  full((1, D_MODEL)), full((D_MODEL, D_MODEL)), full((PLE_DIM, D_MODEL))] + extra_specs,
        out_specs=out_specs,
        out_shape=out_shape,
        scratch_shapes=scratch,
        compiler_params=_params(3),
        name="tail",
    )(h, oa, ob, oc, wo, nf, wu, wu, cw, cw, wd, cs, cs, p, npl, wg, wp, *extra_in)


def _pack(prm):
    (norm_mix, w_in, dn_conv_w, dn_a_log, dn_dt_bias, dn_norm, ssm_lam_re, ssm_lam_im, ssm_log_step,
     ssm_b_re, ssm_b_im, ssm_c_re, ssm_c_im, ssm_d, ssm_glu_w, ssm_glu_b, lower_bounds, hg_norm, w_out,
     norm_ffn, ffn_w_up, ffn_conv_w, ffn_w_down, norm_ple, ple_w_gate, ple_w_proj) = prm
    zeros4 = jnp.zeros((DEPTH, DN_HEADS), F32)
    pad = jnp.zeros((DEPTH, BA_W - 2 * DN_HEADS), F32)
    coef = jnp.stack([jnp.concatenate([zeros4, -jnp.exp(dn_a_log.astype(F32)), pad], axis=-1),
                      jnp.concatenate([zeros4, dn_dt_bias.astype(F32), pad], axis=-1)], axis=1)

    lre = ssm_lam_re.astype(F32)
    lim = ssm_lam_im.astype(F32)
    delta = jnp.exp(ssm_log_step.astype(F32))[..., None]
    mag = jnp.exp(lre * delta)
    lbr = mag * jnp.cos(lim * delta)
    lbi = mag * jnp.sin(lim * delta)
    den = lre * lre + lim * lim
    fr = ((lbr - 1.0) * lre + lbi * lim) / den
    fi = (lbi * lre - (lbr - 1.0) * lim) / den
    bre = ssm_b_re.astype(F32)
    bim = ssm_b_im.astype(F32)
    bbr = fr[..., None] * bre - fi[..., None] * bim
    bbi = fr[..., None] * bim + fi[..., None] * bre
    eye_g = jnp.eye(SSM_GROUPS, dtype=F32)

    def bdiag_in(m):
        return jnp.einsum('dgph,gk->dghkp', m, eye_g).reshape(DEPTH, SSM_WIDTH, SSM_FLAT)

    def bdiag_out(m):
        return jnp.einsum('dghp,gk->dgpkh', m, eye_g).reshape(DEPTH, SSM_FLAT, SSM_WIDTH)

    bmat = jnp.concatenate([bdiag_in(bbr), bdiag_in(bbi)], axis=-1).astype(BF16)
    cmat = jnp.concatenate([bdiag_out(ssm_c_re.astype(F32)), -bdiag_out(ssm_c_im.astype(F32))],
                           axis=1).astype(BF16)
    lam = jnp.stack([lbr.reshape(DEPTH, SSM_FLAT), lbi.reshape(DEPTH, SSM_FLAT)], axis=1)

    gmat = jnp.kron(jnp.eye(HG_HEADS, dtype=F32), jnp.ones((HG_HD, HG_HD), F32)).astype(BF16)
    row = lambda a: a.reshape(DEPTH, 1, a.shape[-1])
    return dict(
        norm_mix=row(norm_mix), w_in=w_in, conv_w=dn_conv_w, coef=coef, dn_norm=row(dn_norm),
        bmat=bmat, lam=lam, cmat=cmat, ssm_d=row(ssm_d), glu_w=ssm_glu_w.astype(BF16), glu_b=row(ssm_glu_b),
        lb=row(lower_bounds), hg_norm=row(jnp.tile(hg_norm, (1, HG_HEADS))), gmat=gmat,
        w_out=w_out.astype(BF16), norm_ffn=row(norm_ffn), w_up=ffn_w_up.astype(BF16), ffn_conv_w=ffn_conv_w,
        w_down=ffn_w_down.astype(BF16), norm_ple=row(norm_ple), ple_gate=ple_w_gate.astype(BF16),
        ple_proj=ple_w_proj.astype(BF16))


def _tiles(bsz, t):
    if t >= DN_CHUNK:
        return dict(tok_tt=512 // bsz, tok_nb=bsz, tail_nb=bsz, dn_tt=DN_CHUNK, hg_tt=HG_CHUNK,
                    s5_steps=1024 // bsz, ffc=FF_DIM, n_sub=1, dn_group=4, hg_group=8)
    return dict(tok_tt=t, tok_nb=512 // t, tail_nb=256 // t, dn_tt=t, hg_tt=t, s5_steps=t, ffc=FF_DIM // 2,
                n_sub=2, dn_group=8, hg_group=8)


def _trunk(x, p, conv_qkv, delta, ssm_re, ssm_im, hgrn, conv_ffn, lp, norm_final):
    bsz, t, _ = x.shape
    tl = _tiles(bsz, t)
    sr_l, si_l, cf_l = [], [], []
    cq_prev, dn_prev, hg_prev = ((jnp.zeros_like(a),) for a in (conv_qkv, delta, hgrn))
    h = x
    for i in range(DEPTH):
        res = _proj_in(h, lp['norm_mix'], lp['w_in'], lp['conv_w'], conv_qkv, cq_prev, i, bsz, t,
                       tl['tok_tt'], tl['tok_nb'], from_btd=(i == 0))
        if i == 0:
            h = res[6]
        qkv, zg, zu, zh, zba, cq = res[:6]
        cq_prev = (cq,)
        o_a, dl = _delta_call(qkv, zg, zba, lp['coef'], lp['dn_norm'], delta, dn_prev, i, bsz, t,
                              tl['dn_tt'], tl['dn_group'])
        dn_prev = (dl,)
        o_b, sr, si = _s5_call(zu, lp['bmat'], lp['lam'], lp['cmat'], lp['ssm_d'], lp['glu_w'], lp['glu_b'],
                               ssm_re[i].reshape(bsz, SSM_FLAT), ssm_im[i].reshape(bsz, SSM_FLAT),
                               i, bsz, t, tl['s5_steps'])
        o_c, hg = _hgrn_call(zh, lp['lb'], lp['hg_norm'], lp['gmat'], hgrn, hg_prev, i, bsz, t, tl['hg_tt'],
                             tl['hg_group'])
        hg_prev = (hg,)
        res = _tail_call(h, o_a, o_b, o_c, lp['w_out'], lp['norm_ffn'], lp['w_up'], lp['ffn_conv_w'],
                         lp['w_down'], conv_ffn, p, lp['norm_ple'], lp['ple_gate'], lp['ple_proj'],
                         norm_final if i == DEPTH - 1 else None,
                         i, bsz, t, tl['tok_tt'], tl['tail_nb'], tl['ffc'], tl['n_sub'])
        h, cfa, cfb = res
        sr_l.append(sr.reshape(bsz, SSM_GROUPS, SSM_STATE))
        si_l.append(si.reshape(bsz, SSM_GROUPS, SSM_STATE))
        cf_l.append(jnp.swapaxes(jnp.concatenate([cfa[-1], cfb[-1]], axis=-1), 0, 1))
    y = h
    return (y, cq, dl, jnp.stack(sr_l), jnp.stack(si_l), hg, jnp.stack(cf_l))


def kernel(x_prompt, x_sample, p_prompt, p_sample, state_conv_qkv, state_delta, state_ssm_re, state_ssm_im, state_hgrn, state_conv_ffn, norm_mix, w_in, dn_conv_w, dn_a_log, dn_dt_bias, dn_norm, ssm_lam_re, ssm_lam_im, ssm_log_step, ssm_b_re, ssm_b_im, ssm_c_re, ssm_c_im, ssm_d, ssm_glu_w, ssm_glu_b, hg_lower, hg_norm, w_out, norm_ffn, ffn_w_up, ffn_conv_w, ffn_w_down, norm_ple, ple_w_gate, ple_w_proj, norm_final):
    lb_p = jax.nn.softmax(hg_lower.astype(F32), axis=0)
    lower_bounds = jnp.cumsum(lb_p, axis=0) - lb_p[0]
    prm = (norm_mix, w_in, dn_conv_w, dn_a_log, dn_dt_bias, dn_norm, ssm_lam_re, ssm_lam_im, ssm_log_step,
           ssm_b_re, ssm_b_im, ssm_c_re, ssm_c_im, ssm_d, ssm_glu_w, ssm_glu_b, lower_bounds, hg_norm, w_out,
           norm_ffn, ffn_w_up, ffn_conv_w, ffn_w_down, norm_ple, ple_w_gate, ple_w_proj)
    layers = _pack(prm)
    nf = norm_final.reshape(1, D_MODEL)

    bp = x_prompt.shape[0]
    z = lambda *shp: jnp.zeros((DEPTH, bp) + shp, F32)
    prompt = _trunk(x_prompt, p_prompt, z(DN_CONV - 1, QKV_W), z(DN_HEADS, DN_HD, DN_HD),
                    z(SSM_GROUPS, SSM_STATE), z(SSM_GROUPS, SSM_STATE), z(HG_HEADS, HG_HD, HG_HD),
                    z(FF_CONV - 1, 2 * FF_DIM), layers, nf)
    sample = _trunk(x_sample, p_sample, state_conv_qkv, state_delta, state_ssm_re, state_ssm_im, state_hgrn,
                    state_conv_ffn, layers, nf)
    return (prompt[0], sample[0]) + prompt[1:] + sample[1:]
```

```python
import functools
import math

import jax
import jax.numpy as jnp
from jax import lax
from jax.experimental import pallas as pl
from jax.experimental.pallas import tpu as pltpu

F32 = jnp.float32
BF16 = jnp.bfloat16

D_MODEL = 1024
DEPTH = 2
DN_HEADS = 4
DN_WIDTH = 512
DN_HD = 128
DN_CONV = 4
DN_CHUNK = 64
SSM_WIDTH = 256
SSM_GROUP = 16
SSM_GROUPS = 16
SSM_STATE = 64
SSM_FLAT = SSM_GROUPS * SSM_STATE
HG_WIDTH = 256
HG_HEADS = 4
HG_HD = 64
HG_CHUNK = 32
FF_DIM = 2816
FF_CONV = 3
PLE_DIM = 256
EPS = 1e-6

SUBLANES = 8
LANES = 128

QKV_W = 3 * DN_WIDTH
ZH_W = 4 * HG_WIDTH
BA_W = LANES
IN_PACKED = QKV_W + DN_WIDTH + SSM_WIDTH + ZH_W + BA_W
IN_RAW = QKV_W + DN_WIDTH + 2 * DN_HEADS + SSM_WIDTH + ZH_W
_Z_WIDTHS = (QKV_W, DN_WIDTH, SSM_WIDTH, ZH_W, BA_W)

VMEM_LIMIT = 56 * 1024 * 1024

_NT = (((1,), (1,)), ((), ()))
_TN = (((0,), (0,)), ((), ()))


def _dot(a, b):
    return jnp.dot(a.astype(BF16), b.astype(BF16), preferred_element_type=F32)


def _dot_nt(a, b):
    return lax.dot_general(a.astype(BF16), b.astype(BF16), _NT, preferred_element_type=F32)


def _dot_tn(a, b):
    return lax.dot_general(a.astype(BF16), b.astype(BF16), _TN, preferred_element_type=F32)


def _split3(x):
    hi = x.astype(BF16)
    r1 = x - hi.astype(F32)
    mid = r1.astype(BF16)
    lo = (r1 - mid.astype(F32)).astype(BF16)
    return hi, mid, lo


def _dot_exact(x, m, dims, x_first=True):
    if x_first:
        return sum(lax.dot_general(p, m, dims, preferred_element_type=F32) for p in _split3(x))
    return sum(lax.dot_general(m, p, dims, preferred_element_type=F32) for p in _split3(x))


_MM = (((1,), (0,)), ((), ()))


def _sigmoid_exp(x):
    return 1.0 / (1.0 + jnp.exp(-x))


def _sigmoid(x):
    return 0.5 * jnp.tanh(0.5 * x) + 0.5


def _silu(x):
    h = 0.5 * x
    return h + h * jnp.tanh(h)


def _layer_spec(layer, block, index, **kw):
    return pl.BlockSpec((None,) + tuple(block), lambda *g: (layer,) + tuple(index(*g)), **kw)


def _layer_full(layer, shape, **kw):
    return _layer_spec(layer, shape, lambda *g: (0,) * len(shape), **kw)


_ANY = pl.BlockSpec(memory_space=pl.ANY)


def _stacked_out_spec(layer, has_prev, block, index):
    if has_prev:
        return _layer_spec(layer, block, index)
    return pl.BlockSpec((DEPTH,) + tuple(block), lambda *g: (0,) + tuple(index(*g)))


def _own_slab(ref, layer, has_prev, first_visit):
    if has_prev:
        return ref

    @pl.when(first_visit)
    def _():
        for other in range(DEPTH):
            if other != layer:
                ref[other] = jnp.zeros(ref.shape[1:], F32)
    return ref.at[layer]


def _softplus(x):
    return jnp.maximum(x, 0.0) + jnp.log1p(jnp.exp(-jnp.abs(x)))


def _rms(x, g):
    ms = jnp.mean(x * x, axis=-1, keepdims=True)
    return x * lax.rsqrt(ms + EPS) * g


def _seq_tri(n, nb):
    r = lax.broadcasted_iota(jnp.int32, (n, n), 0)
    c = lax.broadcasted_iota(jnp.int32, (n, n), 1)
    return jnp.where(((r & (nb - 1)) == (c & (nb - 1))) & (c <= r), 1.0, 0.0).astype(BF16)


def _params(n_axes):
    return pltpu.CompilerParams(dimension_semantics=("arbitrary",) * n_axes, vmem_limit_bytes=VMEM_LIMIT)


def _proj_in_body(*refs, tt, nb, n_tiles, from_btd, n_alias, layer):
    x_ref, g_ref, wraw_ref, cw_ref, cs_ref = refs[:5]
    outs = refs[5 + n_alias:]
    qkv_ref, zg_ref, zu_ref, zh_ref, zba_ref, cso_ref = outs[:6]
    buf, w_ref = outs[-2:]
    ti = pl.program_id(1)
    cso_ref = _own_slab(cso_ref, layer, n_alias > 0, ti == 0)
    n = tt * nb
    hist = DN_CONV - 1

    @pl.when((pl.program_id(0) == 0) & (ti == 0))
    def _():
        o_ba = QKV_W + DN_WIDTH
        n_ba = 2 * DN_HEADS
        n_rest = SSM_WIDTH + ZH_W
        w_ref[:, 0:o_ba] = wraw_ref[:, 0:o_ba].astype(BF16)
        w_ref[:, o_ba:o_ba + n_rest] = wraw_ref[:, o_ba + n_ba:o_ba + n_ba + n_rest].astype(BF16)
        w_ref[:, o_ba + n_rest:IN_PACKED] = jnp.concatenate(
            [wraw_ref[:, o_ba:o_ba + n_ba], jnp.zeros((D_MODEL, BA_W - n_ba), F32)], axis=1).astype(BF16)
    if from_btd:
        h0 = outs[6]

        def cp(b, c):
            h0[:, b, :] = x_ref[b]
            return c

        lax.fori_loop(0, nb, cp, 0)
        x = h0[...].reshape(n, D_MODEL)
    else:
        x = x_ref[...].reshape(n, D_MODEL)
    xn = _rms(x, g_ref[...]).astype(BF16)

    @pl.when(ti == 0)
    def _():
        def cp(b, c):
            buf[:, b, :] = cs_ref[b]
            return c
        lax.fori_loop(0, nb, cp, 0)

    n_sub = 4
    st = tt // n_sub
    sr = st * nb
    hr = hist * nb
    cw = cw_ref[...]
    prev = buf[...].reshape(hr, QKV_W)
    for sb in range(n_sub):
        xs = xn[sb * sr:(sb + 1) * sr]
        ts = slice(sb * st, (sb + 1) * st)
        xcat = jnp.concatenate([prev, jnp.dot(xs, w_ref[:, 0:QKV_W], preferred_element_type=F32)], axis=0)
        c0 = QKV_W
        for ref, width in zip((zg_ref, zu_ref, zh_ref, zba_ref), _Z_WIDTHS[1:]):
            ref[ts] = jnp.dot(xs, w_ref[:, c0:c0 + width], preferred_element_type=F32).reshape(st, nb, width)
            c0 += width
        y = cw[0:1] * xcat[0:sr]
        for j in range(1, DN_CONV):
            y = y + cw[j:j + 1] * xcat[j * nb:j * nb + sr]
        prev = xcat[sr:sr + hr]
        y = _silu(y)
        for h in range(DN_HEADS):
            qs = slice(h * DN_HD, (h + 1) * DN_HD)
            ks = slice(DN_WIDTH + h * DN_HD, DN_WIDTH + (h + 1) * DN_HD)
            qh = y[:, qs]
            kh = y[:, ks]
            qn = qh * (lax.rsqrt(jnp.sum(qh * qh, axis=-1, keepdims=True) + EPS) * DN_HD ** -0.5)
            kn = kh * lax.rsqrt(jnp.sum(kh * kh, axis=-1, keepdims=True) + EPS)
            qkv_ref[ts, :, qs] = qn.reshape(st, nb, DN_HD)
            qkv_ref[ts, :, ks] = kn.reshape(st, nb, DN_HD)
        qkv_ref[ts, :, 2 * DN_WIDTH:QKV_W] = y[:, 2 * DN_WIDTH:QKV_W].reshape(st, nb, DN_WIDTH)
    buf[...] = prev.reshape(hist, nb, QKV_W)

    @pl.when(ti == n_tiles - 1)
    def _():
        def cp(b, c):
            cso_ref[b] = buf[:, b, :]
            return c
        lax.fori_loop(0, nb, cp, 0)


def _proj_in(x, g, w, cw, cs, prev, layer, bsz, t, tt, nb, from_btd):
    n_tiles = t // tt
    hist = DN_CONV - 1
    tm = lambda wd: pl.BlockSpec((tt, nb, wd), lambda j, i: (i, j, 0))
    x_spec = pl.BlockSpec((nb, tt, D_MODEL), lambda j, i: (j, i, 0)) if from_btd else tm(D_MODEL)
    cs_spec = _layer_spec(layer, (nb, hist, QKV_W), lambda j, i: (j, 0, 0))
    out_shape = [jax.ShapeDtypeStruct((t, bsz, wd), F32) for wd in _Z_WIDTHS]
    out_specs = [tm(wd) for wd in _Z_WIDTHS]
    out_shape.append(jax.ShapeDtypeStruct((DEPTH, bsz, hist, QKV_W), F32))
    out_specs.append(_stacked_out_spec(layer, bool(prev), (nb, hist, QKV_W), lambda j, i: (j, 0, 0)))
    if from_btd:
        out_shape.append(jax.ShapeDtypeStruct((t, bsz, D_MODEL), F32))
        out_specs.append(tm(D_MODEL))
    n_in = 5
    return pl.pallas_call(
        functools.partial(_proj_in_body, tt=tt, nb=nb, n_tiles=n_tiles, from_btd=from_btd, n_alias=len(prev),
                          layer=layer),
        grid=(bsz // nb, n_tiles),
        in_specs=[x_spec, _layer_full(layer, (1, D_MODEL)),
                  _layer_full(layer, (D_MODEL, IN_RAW), pipeline_mode=pl.Buffered(1)),
                  _layer_full(layer, (DN_CONV, QKV_W)), cs_spec] + [_ANY] * len(prev),
        out_specs=out_specs,
        out_shape=out_shape,
        input_output_aliases={n_in + k: 5 + k for k in range(len(prev))},
        scratch_shapes=[pltpu.VMEM((hist, nb, QKV_W), F32), pltpu.VMEM((D_MODEL, IN_PACKED), BF16)],
        compiler_params=_params(2),
        name="proj_in",
    )(x, g, w, cw, cs, *prev)


def _delta_body(*refs, tt, group, n_alias, layer):
    qkv_ref, zg_ref, zba_ref, coef_ref, dnw_ref, s0_ref = refs[:6]
    o_ref, s_ref, qb, kb, vb, pb, gcb, ob = refs[6 + n_alias:]
    ti = pl.program_id(1)
    s_ref = _own_slab(s_ref, layer, n_alias > 0, ti == 0)
    nb = SUBLANES
    n = tt * nb
    levels = int(math.log2(tt))

    @pl.when(ti == 0)
    def _():
        s_ref[...] = s0_ref[...]

    for h in range(DN_HEADS):
        for dst, off in ((qb, 0), (kb, DN_WIDTH), (vb, 2 * DN_WIDTH)):
            dst[h] = qkv_ref[:, :, off + h * DN_HD:off + (h + 1) * DN_HD].reshape(n, DN_HD)

    zba = zba_ref[...].reshape(n, BA_W)
    coef = coef_ref[...]
    lane = lax.broadcasted_iota(jnp.int32, zba.shape, 1)
    g = coef[0:1] * _softplus(zba + coef[1:2])
    pb[...] = jnp.where(lane < DN_HEADS, _sigmoid(zba), g)
    gcb[...] = _dot_exact(g, _seq_tri(n, nb), _MM, x_first=False)

    ri = lax.broadcasted_iota(jnp.int32, (tt, tt), 0)
    ci = lax.broadcasted_iota(jnp.int32, (tt, tt), 1)
    causal = ci <= ri
    strict = ci < ri
    eye = jnp.where(ci == ri, 1.0, 0.0).astype(F32)
    sel = jnp.where(lax.broadcasted_iota(jnp.int32, (SUBLANES, BA_W), 0)
                    == lax.broadcasted_iota(jnp.int32, (SUBLANES, BA_W), 1), 1.0, 0.0).astype(BF16)

    def seq_group(bp, carry_):
        chains = []
        for s in range(group):
            b = bp * group + s
            seq_rows = pl.ds(b, tt, stride=nb)
            pbv = pb[seq_rows, :]
            gcv = gcb[seq_rows, :]
            gct = _dot_exact(gcv, sel, _NT, x_first=False)
            for h in range(DN_HEADS):
                chains.append(dict(b=b, h=h, rows=seq_rows, q=qb[h, seq_rows, :], k=kb[h, seq_rows, :],
                                   v=vb[h, seq_rows, :],
                                   beta=pbv[:, h:h + 1], gcol=gcv[:, DN_HEADS + h:DN_HEADS + h + 1],
                                   grow=gct[DN_HEADS + h:DN_HEADS + h + 1, :]))
        for c in chains:
            c['dec'] = jnp.where(causal, jnp.exp(jnp.where(causal, c['gcol'] - c['grow'], 0.0)), 0.0)
            c['qkk'] = _dot_nt(jnp.concatenate([c['q'], c['k']], axis=0), c['k'])
        for c in chains:
            c['p'] = -jnp.where(strict, c['beta'] * c['qkk'][tt:2 * tt] * c['dec'], 0.0)
            c['t'] = eye + c['p']
        for _ in range(levels - 1):
            for c in chains:
                c['p'] = _dot(c['p'], c['p'])
            for c in chains:
                c['t'] = c['t'] + _dot(c['p'], c['t'])
        for c in chains:
            eg = jnp.exp(c['gcol'])
            rhs = jnp.concatenate([c['beta'] * c['v'], (c['beta'] * eg) * c['k']], axis=1)
            c['uw'] = _dot(c['t'], rhs)
            c['qd'] = c['q'] * eg
            c['glast'] = c['grow'][:, tt - 1:tt]
            c['kd'] = c['k'] * jnp.exp(c['glast'] - c['gcol'])
            c['s'] = s_ref[c['b'], c['h']]
        for c in chains:
            c['r'] = _dot(jnp.concatenate([c['uw'][:, DN_HD:2 * DN_HD], c['qd']], axis=0), c['s'])
        for c in chains:
            c['vn'] = c['uw'][:, 0:DN_HD] - c['r'][0:tt]
        for c in chains:
            ob[c['h'], c['rows'], :] = c['r'][tt:2 * tt] + _dot(c['qkk'][0:tt] * c['dec'], c['vn'])
            s_ref[c['b'], c['h']] = c['s'] * jnp.exp(c['glast']) + _dot_tn(c['kd'], c['vn'])
        return carry_

    lax.fori_loop(0, nb // group, seq_group, 0)

    gate = zg_ref[...].reshape(n, DN_WIDTH)
    dnw = dnw_ref[...]
    for h in range(DN_HEADS):
        hs = slice(h * DN_HD, (h + 1) * DN_HD)
        o_ref[:, :, hs] = (_rms(ob[h], dnw) * _silu(gate[:, hs])).reshape(tt, nb, DN_HD)


def _delta_call(qkv, zg, zba, coef, dnw, s0, prev, layer, bsz, t, tt, group):
    nb = SUBLANES
    tm = lambda wd: pl.BlockSpec((tt, nb, wd), lambda j, i: (i, j, 0))
    st_spec = _layer_spec(layer, (nb, DN_HEADS, DN_HD, DN_HD), lambda j, i: (j, 0, 0, 0))
    heads = pltpu.VMEM((DN_HEADS, tt * nb, DN_HD), F32)
    rows = pltpu.VMEM((tt * nb, BA_W), F32)
    n_in = 6
    return pl.pallas_call(
        functools.partial(_delta_body, tt=tt, group=group, n_alias=len(prev), layer=layer),
        grid=(bsz // nb, t // tt),
        in_specs=[tm(QKV_W), tm(DN_WIDTH), tm(BA_W), _layer_full(layer, (2, BA_W)),
                  _layer_full(layer, (1, DN_HD)), st_spec] + [_ANY] * len(prev),
        out_specs=[tm(DN_WIDTH), _stacked_out_spec(layer, bool(prev), (nb, DN_HEADS, DN_HD, DN_HD),
                                                   lambda j, i: (j, 0, 0, 0))],
        out_shape=[jax.ShapeDtypeStruct((t, bsz, DN_WIDTH), F32),
                   jax.ShapeDtypeStruct((DEPTH, bsz, DN_HEADS, DN_HD, DN_HD), F32)],
        input_output_aliases={n_in + k: 1 + k for k in range(len(prev))},
        scratch_shapes=[heads, heads, heads, rows, rows, heads],
        compiler_params=_params(2),
        name="delta",
    )(qkv, zg, zba, coef, dnw, s0, *prev)


def _hgrn_body(*refs, tt, n_tiles, group, n_alias, layer):
    zh_ref, lb_ref, nw_ref, gm_ref, s0_ref = refs[:5]
    o_ref, s_ref, sbd, qsb, ksb, qeb, kdb, vb, elb, ob = refs[5 + n_alias:]
    ti = pl.program_id(1)
    s_ref = _own_slab(s_ref, layer, n_alias > 0, ti == 0)
    nb = SUBLANES
    n = tt * nb
    mid = tt // 2 - 1
    sh_h = int(math.log2(HG_HD))

    @pl.when(ti == 0)
    def _():
        zero = jnp.zeros((HG_HD, HG_HD), F32)

        def init(b, c):
            for h in range(HG_HEADS):
                parts = [zero] * HG_HEADS
                parts[h] = s0_ref[b, h].T
                sbd[b, h * HG_HD:(h + 1) * HG_HD, :] = jnp.concatenate(parts, axis=1)
            return c
        for b in range(nb):
            init(b, 0)

    zh = zh_ref[...].reshape(n, ZH_W)
    lb = lb_ref[...]
    f = lb + (1.0 - lb) * _sigmoid_exp(zh[:, HG_WIDTH:2 * HG_WIDTH])
    q = _silu(zh[:, 0:HG_WIDTH]).reshape(tt, nb, HG_WIDTH)
    k = (1.0 - f).reshape(tt, nb, HG_WIDTH)
    bc = _dot_exact(jnp.log(f), _seq_tri(n, nb), _MM, x_first=False).reshape(tt, nb, HG_WIDTH)
    bm = bc[mid]
    bl = bc[tt - 1]
    half = HG_WIDTH // 2
    qs = q * jnp.exp(bc - bm[None])
    ks = k * jnp.exp(bm[None] - bc)
    for dst, val in ((qsb, qs), (ksb, ks), (qeb, qs * jnp.exp(bm)[None]), (kdb, ks * jnp.exp(bl - bm)[None])):
        val = val.reshape(n, HG_WIDTH)
        dst[0] = val[:, 0:half]
        dst[1] = val[:, half:HG_WIDTH]
    vb[0] = zh[:, 2 * HG_WIDTH:2 * HG_WIDTH + half]
    vb[1] = zh[:, 2 * HG_WIDTH + half:3 * HG_WIDTH]
    elb[...] = jnp.exp(bl)

    ri = lax.broadcasted_iota(jnp.int32, (HG_HEADS * tt, tt), 0)
    ci = lax.broadcasted_iota(jnp.int32, (HG_HEADS * tt, tt), 1)
    causal = ci <= (ri & (tt - 1))
    lane_head = lax.broadcasted_iota(jnp.int32, (tt, HG_WIDTH), 1) >> sh_h
    hmask = [jnp.where(lane_head == h, 1.0, 0.0).astype(F32) for h in range(HG_HEADS)]
    bd = ((lax.broadcasted_iota(jnp.int32, (HG_WIDTH, HG_WIDTH), 0) >> sh_h)
          == (lax.broadcasted_iota(jnp.int32, (HG_WIDTH, HG_WIDTH), 1) >> sh_h))

    def seq_group(bp, carry_):
        seqs = []
        for s in range(group):
            b = bp * group + s
            seq_rows = pl.ds(b, tt, stride=nb)
            seq = lambda ref: jnp.concatenate([ref[0, seq_rows, :], ref[1, seq_rows, :]], axis=1)
            seqs.append(dict(b=b, rows=seq_rows, qs=seq(qsb), ks=seq(ksb), qe=seq(qeb), kd=seq(kdb),
                             v=seq(vb), st=sbd[b], el=elb[pl.ds(b, 1), :]))
        for c in seqs:
            c['a'] = jnp.where(causal, _dot_nt(jnp.concatenate([c['qs'] * m for m in hmask], axis=0), c['ks']), 0.0)
            c['o'] = _dot_nt(c['qe'], c['st'])
            c['kv'] = _dot_tn(c['v'], c['kd'])
        for c in seqs:
            o = c['o']
            for h in range(HG_HEADS):
                o = o + hmask[h] * _dot(c['a'][h * tt:(h + 1) * tt], c['v'])
            ob[0, c['rows'], :] = o[:, 0:half]
            ob[1, c['rows'], :] = o[:, half:HG_WIDTH]
            sbd[c['b']] = c['st'] * c['el'] + jnp.where(bd, c['kv'], 0.0)
        return carry_

    lax.fori_loop(0, nb // group, seq_group, 0)

    o = jnp.concatenate([ob[0], ob[1]], axis=1)
    ms = _dot_exact(o * o, gm_ref[...], _MM) * (1.0 / HG_HD)
    o_ref[...] = (o * lax.rsqrt(ms + EPS) * nw_ref[...]
                  * _silu(zh[:, 3 * HG_WIDTH:4 * HG_WIDTH])).reshape(tt, nb, HG_WIDTH)

    @pl.when(ti == n_tiles - 1)
    def _():
        def fin(b, c):
            for h in range(HG_HEADS):
                blk = sbd[b, h * HG_HD:(h + 1) * HG_HD, :]
                s_ref[b, h] = blk[:, h * HG_HD:(h + 1) * HG_HD].T
            return c
        for b in range(nb):
            fin(b, 0)


def _hgrn_call(zh, lb, nw, gm, s0, prev, layer, bsz, t, tt, group):
    nb = SUBLANES
    n_tiles = t // tt
    st_spec = _layer_spec(layer, (nb, HG_HEADS, HG_HD, HG_HD), lambda j, i: (j, 0, 0, 0))
    tile = pltpu.VMEM((HG_WIDTH // LANES, tt * nb, LANES), F32)
    n_in = 5
    return pl.pallas_call(
        functools.partial(_hgrn_body, tt=tt, n_tiles=n_tiles, group=group, n_alias=len(prev), layer=layer),
        grid=(bsz // nb, n_tiles),
        in_specs=[pl.BlockSpec((tt, nb, ZH_W), lambda j, i: (i, j, 0)),
                  _layer_full(layer, (1, HG_WIDTH)), _layer_full(layer, (1, HG_WIDTH)),
                  pl.BlockSpec((HG_WIDTH, HG_WIDTH), lambda j, i: (0, 0)), st_spec] + [_ANY] * len(prev),
        out_specs=[pl.BlockSpec((tt, nb, HG_WIDTH), lambda j, i: (i, j, 0)),
                   _stacked_out_spec(layer, bool(prev), (nb, HG_HEADS, HG_HD, HG_HD), lambda j, i: (j, 0, 0, 0))],
        out_shape=[jax.ShapeDtypeStruct((t, bsz, HG_WIDTH), F32),
                   jax.ShapeDtypeStruct((DEPTH, bsz, HG_HEADS, HG_HD, HG_HD), F32)],
        input_output_aliases={n_in + k: 1 + k for k in range(len(prev))},
        scratch_shapes=[pltpu.VMEM((nb, HG_WIDTH, HG_WIDTH), F32),
                        tile, tile, tile, tile, tile, pltpu.VMEM((nb, HG_WIDTH), F32), tile],
        compiler_params=_params(2),
        name="hgrn",
    )(zh, lb, nw, gm, s0, *prev)


def _s5_body(u_ref, bh_ref, lam_ref, c_ref, d_ref, gw_ref, gb_ref, x0r_ref, x0i_ref,
             o_ref, xr_ref, xi_ref, *, bsz, steps):
    i = pl.program_id(0)

    @pl.when(i == 0)
    def _():
        xr_ref[...] = x0r_ref[...]
        xi_ref[...] = x0i_ref[...]

    u = u_ref[...].reshape(steps * bsz, SSM_WIDTH)
    bu = _dot(u, bh_ref[...])

    lr = jnp.broadcast_to(lam_ref[0:1, :], (SUBLANES, SSM_FLAT))
    li = jnp.broadcast_to(lam_ref[1:2, :], (SUBLANES, SSM_FLAT))

    n_rb = bsz // SUBLANES
    xr_all = xr_ref[...]
    xi_all = xi_ref[...]
    scanned = [[None] * n_rb for _ in range(steps)]
    last_r, last_i = [], []
    for rb in range(n_rb):
        xr = xr_all[rb * SUBLANES:(rb + 1) * SUBLANES]
        xi = xi_all[rb * SUBLANES:(rb + 1) * SUBLANES]
        for t in range(steps):
            r0 = t * bsz + rb * SUBLANES
            xr, xi = (lr * xr - li * xi + bu[r0:r0 + SUBLANES, 0:SSM_FLAT],
                      lr * xi + li * xr + bu[r0:r0 + SUBLANES, SSM_FLAT:2 * SSM_FLAT])
            scanned[t][rb] = jnp.concatenate([xr, xi], axis=1)
        last_r.append(xr)
        last_i.append(xi)
    xr_ref[...] = jnp.concatenate(last_r, axis=0)
    xi_ref[...] = jnp.concatenate(last_i, axis=0)
    xs = jnp.concatenate([scanned[t][rb] for t in range(steps) for rb in range(n_rb)], axis=0)

    y = _dot(xs, c_ref[...]) + d_ref[...] * u
    y = 0.5 * y * (1.0 + jnp.tanh(math.sqrt(2.0 / math.pi) * (y + 0.044715 * (y * y * y))))
    o_ref[...] = (y * _sigmoid(_dot(y, gw_ref[...]) + gb_ref[...])).reshape(steps, bsz, SSM_WIDTH)


def _s5_call(u, bh, lam, cm, d, gw, gb, x0r, x0i, layer, bsz, t, steps):
    rows = steps * bsz
    full = lambda shp: pl.BlockSpec(shp, lambda i: tuple(0 for _ in shp))
    lfull = lambda shp: _layer_full(layer, shp)
    return pl.pallas_call(
        functools.partial(_s5_body, bsz=bsz, steps=steps),
        grid=(t // steps,),
        in_specs=[pl.BlockSpec((steps, bsz, SSM_WIDTH), lambda i: (i, 0, 0)),
                  lfull((SSM_WIDTH, 2 * SSM_FLAT)),
                  lfull((2, SSM_FLAT)), lfull((2 * SSM_FLAT, SSM_WIDTH)), lfull((1, SSM_WIDTH)),
                  lfull((SSM_WIDTH, SSM_WIDTH)), lfull((1, SSM_WIDTH)),
                  full((bsz, SSM_FLAT)), full((bsz, SSM_FLAT))],
        out_specs=[pl.BlockSpec((steps, bsz, SSM_WIDTH), lambda i: (i, 0, 0)),
                   full((bsz, SSM_FLAT)), full((bsz, SSM_FLAT))],
        out_shape=[jax.ShapeDtypeStruct((t, bsz, SSM_WIDTH), F32),
                   jax.ShapeDtypeStruct((bsz, SSM_FLAT), F32),
                   jax.ShapeDtypeStruct((bsz, SSM_FLAT), F32)],
        compiler_params=_params(1),
        name="s5",
    )(u, bh, lam, cm, d, gw, gb, x0r, x0i)


def _tail_body(*refs, tt, nb, n_ffc, ffc, n_sub, emit_final):
    (h_ref, oa_ref, ob_ref, oc_ref, wo_ref, nf_ref, wua_ref, wub_ref, cwa_ref, cwb_ref,
     wd_ref, csa_ref, csb_ref, p_ref, npl_ref, wg_ref, wp_ref) = refs[:17]
    rest = refs[17:]
    if emit_final:
        fin_ref, y_ref, cso_a_ref, cso_b_ref, hn, p3, car_a, car_b, ys, out_ref = rest
    else:
        out_ref, cso_a_ref, cso_b_ref, hn, p3, car_a, car_b = rest
    i = pl.program_id(1)
    j = pl.program_id(2)
    n = tt * nb
    hist = FF_CONV - 1
    st = tt // n_sub
    sr = st * nb
    hr = hist * nb

    @pl.when(j == 0)
    def _():
        h2 = (h_ref[...].reshape(n, D_MODEL) + _dot(oa_ref[...].reshape(n, DN_WIDTH), wo_ref[0:DN_WIDTH, :])
              + _dot(ob_ref[...].reshape(n, SSM_WIDTH), wo_ref[DN_WIDTH:DN_WIDTH + SSM_WIDTH, :])
              + _dot(oc_ref[...].reshape(n, HG_WIDTH), wo_ref[DN_WIDTH + SSM_WIDTH:D_MODEL, :]))
        out_ref[...] = h2.reshape(tt, nb, D_MODEL)
        hn[...] = _rms(h2, nf_ref[...]).astype(BF16)

    @pl.when(i == 0)
    def _():
        def cp(b, c):
            car_a[j, :, b, :] = csa_ref[b]
            car_b[j, :, b, :] = csb_ref[b]
            return c
        lax.fori_loop(0, nb, cp, 0)

    prev = [car_a[j].reshape(hr, ffc), car_b[j].reshape(hr, ffc)]
    wd = wd_ref[...]
    for sb in range(n_sub):
        hs = hn[sb * sr:(sb + 1) * sr, :]
        halves = []
        for idx, (wu, cw) in enumerate(((wua_ref, cwa_ref), (wub_ref, cwb_ref))):
            x = jnp.concatenate([prev[idx], jnp.dot(hs, wu[...], preferred_element_type=F32)], axis=0)
            w = cw[...]
            acc = w[0:1] * x[0:sr]
            for s in range(1, FF_CONV):
                acc = acc + w[s:s + 1] * x[s * nb:s * nb + sr]
            halves.append(acc)
            prev[idx] = x[sr:sr + hr]
        out_ref[sb * st:(sb + 1) * st] += _dot(_silu(halves[0]) * halves[1], wd).reshape(st, nb, D_MODEL)
    for car, cso, last in ((car_a, cso_a_ref, prev[0]), (car_b, cso_b_ref, prev[1])):
        last = last.reshape(hist, nb, ffc)
        car[j] = last
        cso[...] = last

    @pl.when(j == n_ffc - 1)
    def _():
        def cp(b, c):
            p3[:, b, :] = p_ref[b]
            return c
        lax.fori_loop(0, nb, cp, 0)
        h3 = out_ref[...].reshape(n, D_MODEL)
        gate = _sigmoid(_dot(_rms(h3, npl_ref[...]), wg_ref[...]))
        h4 = h3 + gate * _dot(p3[...].reshape(n, PLE_DIM), wp_ref[...])
        out_ref[...] = h4.reshape(tt, nb, D_MODEL)
        if emit_final:
            yv = _rms(h4, fin_ref[...])
            n_slab = D_MODEL // LANES
            for s in range(n_slab):
                ys[s] = yv[:, s * LANES:(s + 1) * LANES]

            def seq(b, c):
                rows = pl.ds(b, tt, stride=nb)
                y_ref[b] = jnp.concatenate([ys[s, rows, :] for s in range(n_slab)], axis=1)
                return c
            lax.fori_loop(0, nb, seq, 0)


def _tail_call(h, oa, ob, oc, wo, nf, wu, cw, wd, cs, p, npl, wg, wp, fin, layer, bsz, t, tt, nb, ffc, n_sub):
    n_ffc = FF_DIM // ffc
    n_t = t // tt
    hist = FF_CONV - 1
    tm = lambda wd_: pl.BlockSpec((tt, nb, wd_), lambda jb, i, j: (i, jb, 0))
    full = lambda shp: _layer_full(layer, shp, pipeline_mode=pl.Buffered(1))
    lyr = lambda blk, idx: _layer_spec(layer, blk, idx)
    wchunk = (lambda blk, idx: _layer_spec(layer, blk, idx, pipeline_mode=pl.Buffered(1))) if n_ffc == 1 else lyr
    cs_a = wchunk((nb, hist, ffc), lambda jb, i, j: (jb, 0, j))
    cs_b = wchunk((nb, hist, ffc), lambda jb, i, j: (jb, 0, n_ffc + j))
    cso = pl.BlockSpec((None, hist, nb, ffc), lambda jb, i, j: (i, 0, jb, j))
    scratch = [pltpu.VMEM((tt * nb, D_MODEL), BF16),
               pltpu.VMEM((tt, nb, PLE_DIM), F32),
               pltpu.VMEM((n_ffc, hist, nb, ffc), F32),
               pltpu.VMEM((n_ffc, hist, nb, ffc), F32)]
    emit_final = fin is not None
    out_specs = [tm(D_MODEL), cso, cso]
    out_shape = [jax.ShapeDtypeStruct((t, bsz, D_MODEL), F32),
                 jax.ShapeDtypeStruct((n_t, hist, bsz, FF_DIM), F32),
                 jax.ShapeDtypeStruct((n_t, hist, bsz, FF_DIM), F32)]
    extra_in, extra_specs = [], []
    if emit_final:
        extra_in = [fin]
        extra_specs = [pl.BlockSpec((1, D_MODEL), lambda jb, i, j: (0, 0))]
        out_specs[0] = pl.BlockSpec((nb, tt, D_MODEL), lambda jb, i, j: (jb, i, 0))
        out_shape[0] = jax.ShapeDtypeStruct((bsz, t, D_MODEL), F32)
        scratch += [pltpu.VMEM((D_MODEL // LANES, tt * nb, LANES), F32), pltpu.VMEM((tt, nb, D_MODEL), F32)]
    return pl.pallas_call(
        functools.partial(_tail_body, tt=tt, nb=nb, n_ffc=n_ffc, ffc=ffc, n_sub=n_sub, emit_final=emit_final),
        grid=(bsz // nb, n_t, n_ffc),
        in_specs=[tm(D_MODEL), tm(DN_WIDTH), tm(SSM_WIDTH), tm(HG_WIDTH),
                  full((D_MODEL, D_MODEL)), full((1, D_MODEL)),
                  wchunk((D_MODEL, ffc), lambda jb, i, j: (0, j)),
                  wchunk((D_MODEL, ffc), lambda jb, i, j: (0, n_ffc + j)),
                  lyr((FF_CONV, ffc), lambda jb, i, j: (0, j)),
                  lyr((FF_CONV, ffc), lambda jb, i, j: (0, n_ffc + j)),
                  wchunk((ffc, D_MODEL), lambda jb, i, j: (j, 0)),
                  cs_a, cs_b,
                  lyr((nb, tt, PLE_DIM), lambda jb, i, j: (jb, i, 0)),
                  full((1, D_MODEL)), full((D_MODEL, D_MODEL)), full((PLE_DIM, D_MODEL))] + extra_specs,
        out_specs=out_specs,
        out_shape=out_shape,
        scratch_shapes=scratch,
        compiler_params=_params(3),
        name="tail",
    )(h, oa, ob, oc, wo, nf, wu, wu, cw, cw, wd, cs, cs, p, npl, wg, wp, *extra_in)


def _pack(prm):
    (norm_mix, w_in, dn_conv_w, dn_a_log, dn_dt_bias, dn_norm, ssm_lam_re, ssm_lam_im, ssm_log_step,
     ssm_b_re, ssm_b_im, ssm_c_re, ssm_c_im, ssm_d, ssm_glu_w, ssm_glu_b, lower_bounds, hg_norm, w_out,
     norm_ffn, ffn_w_up, ffn_conv_w, ffn_w_down, norm_ple, ple_w_gate, ple_w_proj) = prm
    zeros4 = jnp.zeros((DEPTH, DN_HEADS), F32)
    pad = jnp.zeros((DEPTH, BA_W - 2 * DN_HEADS), F32)
    coef = jnp.stack([jnp.concatenate([zeros4, -jnp.exp(dn_a_log.astype(F32)), pad], axis=-1),
                      jnp.concatenate([zeros4, dn_dt_bias.astype(F32), pad], axis=-1)], axis=1)

    lre = ssm_lam_re.astype(F32)
    lim = ssm_lam_im.astype(F32)
    delta = jnp.exp(ssm_log_step.astype(F32))[..., None]
    mag = jnp.exp(lre * delta)
    lbr = mag * jnp.cos(lim * delta)
    lbi = mag * jnp.sin(lim * delta)
    den = lre * lre + lim * lim
    fr = ((lbr - 1.0) * lre + lbi * lim) / den
    fi = (lbi * lre - (lbr - 1.0) * lim) / den
    bre = ssm_b_re.astype(F32)
    bim = ssm_b_im.astype(F32)
    bbr = fr[..., None] * bre - fi[..., None] * bim
    bbi = fr[..., None] * bim + fi[..., None] * bre
    eye_g = jnp.eye(SSM_GROUPS, dtype=F32)

    def bdiag_in(m):
        return jnp.einsum('dgph,gk->dghkp', m, eye_g).reshape(DEPTH, SSM_WIDTH, SSM_FLAT)

    def bdiag_out(m):
        return jnp.einsum('dghp,gk->dgpkh', m, eye_g).reshape(DEPTH, SSM_FLAT, SSM_WIDTH)

    bmat = jnp.concatenate([bdiag_in(bbr), bdiag_in(bbi)], axis=-1).astype(BF16)
    cmat = jnp.concatenate([bdiag_out(ssm_c_re.astype(F32)), -bdiag_out(ssm_c_im.astype(F32))],
                           axis=1).astype(BF16)
    lam = jnp.stack([lbr.reshape(DEPTH, SSM_FLAT), lbi.reshape(DEPTH, SSM_FLAT)], axis=1)

    gmat = jnp.kron(jnp.eye(HG_HEADS, dtype=F32), jnp.ones((HG_HD, HG_HD), F32)).astype(BF16)
    row = lambda a: a.reshape(DEPTH, 1, a.shape[-1])
    return dict(
        norm_mix=row(norm_mix), w_in=w_in, conv_w=dn_conv_w, coef=coef, dn_norm=row(dn_norm),
        bmat=bmat, lam=lam, cmat=cmat, ssm_d=row(ssm_d), glu_w=ssm_glu_w.astype(BF16), glu_b=row(ssm_glu_b),
        lb=row(lower_bounds), hg_norm=row(jnp.tile(hg_norm, (1, HG_HEADS))), gmat=gmat,
        w_out=w_out.astype(BF16), norm_ffn=row(norm_ffn), w_up=ffn_w_up.astype(BF16), ffn_conv_w=ffn_conv_w,
        w_down=ffn_w_down.astype(BF16), norm_ple=row(norm_ple), ple_gate=ple_w_gate.astype(BF16),
        ple_proj=ple_w_proj.astype(BF16))


def _tiles(bsz, t):
    if t >= DN_CHUNK:
        return dict(tok_tt=512 // bsz, tok_nb=bsz, tail_nb=bsz, dn_tt=DN_CHUNK, hg_tt=HG_CHUNK,
                    s5_steps=1024 // bsz, ffc=FF_DIM, n_sub=1, dn_group=4, hg_group=8)
    return dict(tok_tt=t, tok_nb=512 // t, tail_nb=256 // t, dn_tt=t, hg_tt=t, s5_steps=t, ffc=FF_DIM // 2,
                n_sub=2, dn_group=8, hg_group=8)


def _trunk(x, p, conv_qkv, delta, ssm_re, ssm_im, hgrn, conv_ffn, lp, norm_final):
    bsz, t, _ = x.shape
    tl = _tiles(bsz, t)
    sr_l, si_l, cf_l = [], [], []
    cq_prev, dn_prev, hg_prev = (), (), ()
    h = x
    for i in range(DEPTH):
        res = _proj_in(h, lp['norm_mix'], lp['w_in'], lp['conv_w'], conv_qkv, cq_prev, i, bsz, t,
                       tl['tok_tt'], tl['tok_nb'], from_btd=(i == 0))
        if i == 0:
            h = res[6]
        qkv, zg, zu, zh, zba, cq = res[:6]
        cq_prev = (cq,)
        o_a, dl = _delta_call(qkv, zg, zba, lp['coef'], lp['dn_norm'], delta, dn_prev, i, bsz, t,
                              tl['dn_tt'], tl['dn_group'])
        dn_prev = (dl,)
        o_b, sr, si = _s5_call(zu, lp['bmat'], lp['lam'], lp['cmat'], lp['ssm_d'], lp['glu_w'], lp['glu_b'],
                               ssm_re[i].reshape(bsz, SSM_FLAT), ssm_im[i].reshape(bsz, SSM_FLAT),
                               i, bsz, t, tl['s5_steps'])
        o_c, hg = _hgrn_call(zh, lp['lb'], lp['hg_norm'], lp['gmat'], hgrn, hg_prev, i, bsz, t, tl['hg_tt'],
                             tl['hg_group'])
        hg_prev = (hg,)
        res = _tail_call(h, o_a, o_b, o_c, lp['w_out'], lp['norm_ffn'], lp['w_up'], lp['ffn_conv_w'],
                         lp['w_down'], conv_ffn, p, lp['norm_ple'], lp['ple_gate'], lp['ple_proj'],
                         norm_final if i == DEPTH - 1 else None,
                         i, bsz, t, tl['tok_tt'], tl['tail_nb'], tl['ffc'], tl['n_sub'])
        h, cfa, cfb = res
        sr_l.append(sr.reshape(bsz, SSM_GROUPS, SSM_STATE))
        si_l.append(si.reshape(bsz, SSM_GROUPS, SSM_STATE))
        cf_l.append(jnp.swapaxes(jnp.concatenate([cfa[-1], cfb[-1]], axis=-1), 0, 1))
    y = h
    return (y, cq, dl, jnp.stack(sr_l), jnp.stack(si_l), hg, jnp.stack(cf_l))


def kernel(x_prompt, x_sample, p_prompt, p_sample, state_conv_qkv, state_delta, state_ssm_re, state_ssm_im, state_hgrn, state_conv_ffn, norm_mix, w_in, dn_conv_w, dn_a_log, dn_dt_bias, dn_norm, ssm_lam_re, ssm_lam_im, ssm_log_step, ssm_b_re, ssm_b_im, ssm_c_re, ssm_c_im, ssm_d, ssm_glu_w, ssm_glu_b, hg_lower, hg_norm, w_out, norm_ffn, ffn_w_up, ffn_conv_w, ffn_w_down, norm_ple, ple_w_gate, ple_w_proj, norm_final):
    lb_p = jax.nn.softmax(hg_lower.astype(F32), axis=0)
    lower_bounds = jnp.cumsum(lb_p, axis=0) - lb_p[0]
    prm = (norm_mix, w_in, dn_conv_w, dn_a_log, dn_dt_bias, dn_norm, ssm_lam_re, ssm_lam_im, ssm_log_step,
           ssm_b_re, ssm_b_im, ssm_c_re, ssm_c_im, ssm_d, ssm_glu_w, ssm_glu_b, lower_bounds, hg_norm, w_out,
           norm_ffn, ffn_w_up, ffn_conv_w, ffn_w_down, norm_ple, ple_w_gate, ple_w_proj)
    layers = _pack(prm)
    nf = norm_final.reshape(1, D_MODEL)

    bp = x_prompt.shape[0]
    z = lambda *shp: jnp.zeros((DEPTH, bp) + shp, F32)
    prompt = _trunk(x_prompt, p_prompt, z(DN_CONV - 1, QKV_W), z(DN_HEADS, DN_HD, DN_HD),
                    z(SSM_GROUPS, SSM_STATE), z(SSM_GROUPS, SSM_STATE), z(HG_HEADS, HG_HD, HG_HD),
                    z(FF_CONV - 1, 2 * FF_DIM), layers, nf)
    sample = _trunk(x_sample, p_sample, state_conv_qkv, state_delta, state_ssm_re, state_ssm_im, state_hgrn,
                    state_conv_ffn, layers, nf)
    return (prompt[0], sample[0]) + prompt[1:] + sample[1:]
```
